```python
import math
import jax, jax.numpy as jnp
from jax import lax
import numpy as np

D_MODEL = 1024
BATCH = 8
SEQ = 2048
DEPTH = 1
DEC_BATCH = 128
DEC_SEQ = 8
PAST_LEN = 2048
PAGE_SIZE = 128

DA_HEADS = 8
DA_DK = 64
DA_DV = 2 * DA_DK
DA_QK_W = DA_HEADS * 2 * DA_DK
DA_V_W = DA_HEADS * DA_DV
Q_BLOCK = 128
CHUNK = 128
SG_GROUPS = 8
SG_WIDTH = 1024
SG_GDIM = SG_WIDTH // SG_GROUPS
N_MEM = 256
MX_HEADS = 4
MX_DH = 256
MX_W = MX_HEADS * MX_DH
N_BUCKETS = 32
MAX_DISTANCE = 128
N_BRANCH = 3
BR_IN = DA_V_W + SG_WIDTH + MX_W
IN_W = 2 * DA_QK_W + DA_V_W + 2 * SG_WIDTH + MX_W + N_BRANCH * D_MODEL
D_FF = -(-8 * D_MODEL // (3 * 256)) * 256
RMS_EPS = 1e-6
NEG_INF = -1e30

kernel_name = "gated_parallel_diffattn_chunkgmlp_memxattn_step"


def _rmsnorm(x, g):
    xf = x.astype(jnp.float32)
    y = xf * lax.rsqrt(jnp.mean(jnp.square(xf), axis=-1, keepdims=True) + RMS_EPS)
    return (y * g.astype(jnp.float32)).astype(x.dtype)


def _t5_bucket(dist):
    n = jnp.maximum(dist, 0)
    max_exact = N_BUCKETS // 2
    nf = jnp.maximum(n, 1).astype(jnp.float32)
    large = max_exact + (jnp.log(nf / max_exact) / math.log(MAX_DISTANCE / max_exact)
                         * (N_BUCKETS - max_exact)).astype(jnp.int32)
    large = jnp.minimum(large, N_BUCKETS - 1)
    return jnp.where(n < max_exact, n, large)


def _diff_attn_core(q, k, v, q_pos, k_pos, rel_bias, lam, lam_init, subln_g):
    dist = q_pos[:, None] - k_pos[None, :]
    bias = jnp.transpose(rel_bias[_t5_bucket(dist)], (2, 0, 1)).astype(jnp.float32)
    s = jnp.einsum('bqhcd,bkhcd->bchqk', q, k, preferred_element_type=jnp.float32) * (DA_DK ** -0.5) + bias
    s = jnp.where(dist >= 0, s, NEG_INF)
    p = jax.nn.softmax(s, axis=-1)
    a = p[:, 0] - lam * p[:, 1]
    o = jnp.einsum('bhqk,bkhd->bqhd', a.astype(v.dtype), v)
    return _rmsnorm(o, subln_g) * (1.0 - lam_init)


def _da_prompt(q, k, v, rel_bias, lam, lam_init, subln_g):
    B, S = q.shape[0], q.shape[1]
    nb = S // Q_BLOCK
    pos = jnp.arange(S, dtype=jnp.int32)
    qb = jnp.moveaxis(q.reshape(B, nb, Q_BLOCK, DA_HEADS, 2, DA_DK), 1, 0)

    def blk(args):
        qi, pi = args
        return _diff_attn_core(qi, k, v, pi, pos, rel_bias, lam, lam_init, subln_g)

    ob = lax.map(blk, (qb, pos.reshape(nb, Q_BLOCK)))
    return jnp.moveaxis(ob, 0, 1).reshape(B, S, DA_HEADS, DA_DV)


def _da_sample(q, k, v, past_k, past_v, rel_bias, lam, lam_init, subln_g):
    T = q.shape[1]
    P = past_k.shape[1]
    k_all = jnp.concatenate([past_k.astype(k.dtype), k], axis=1)
    v_all = jnp.concatenate([past_v.astype(v.dtype), v], axis=1)
    q_pos = P + jnp.arange(T, dtype=jnp.int32)
    k_pos = jnp.arange(P + T, dtype=jnp.int32)
    return _diff_attn_core(q, k_all, v_all, q_pos, k_pos, rel_bias, lam, lam_init, subln_g)


def _spatial_gate(u, v, sg_norm, sg_w, sg_b):
    B, T = u.shape[0], u.shape[1]
    L = min(T, CHUNK)
    nc = T // L
    vn = _rmsnorm(v, sg_norm)
    vg = vn.reshape(B, nc, L, SG_GROUPS, SG_GDIM)
    causal = jnp.tril(jnp.ones((L, L), dtype=bool))
    w = jnp.where(causal, sg_w[:, :L, :L], 0.0).astype(v.dtype)
    mix = jnp.einsum('gts,bnsgd->bntgd', w, vg) + jnp.transpose(sg_b[:, :L])[None, None, :, :, None].astype(v.dtype)
    out = u.reshape(B, nc, L, SG_GROUPS, SG_GDIM) * mix
    return out.reshape(B, T, SG_WIDTH), vn


def _cross_attn(q, k, v):
    s = jnp.einsum('bqhd,bkhd->bhqk', q, k, preferred_element_type=jnp.float32) * (MX_DH ** -0.5)
    p = jax.nn.softmax(s, axis=-1)
    return jnp.einsum('bhqk,bkhd->bqhd', p.astype(v.dtype), v)


def _mem_kv(mem, g_mem, w_mem_kv):
    B = mem.shape[0]
    kv = (_rmsnorm(mem, g_mem) @ w_mem_kv).reshape(B, mem.shape[1], 2, MX_HEADS, MX_DH)
    return kv[:, :, 0], kv[:, :, 1]


def _layer(x, l, past_k, past_v, mem_k, mem_v, g_attn, w_in, da_lam, da_subln, rel_bias,
           sg_norm, sg_w, sg_b, w_branch, w_out, g_ffn, w_up, w_down):
    B, T = x.shape[0], x.shape[1]
    h = _rmsnorm(x, g_attn)
    z = h @ w_in
    sizes = (DA_QK_W, DA_QK_W, DA_V_W, SG_WIDTH, SG_WIDTH, MX_W, N_BRANCH * D_MODEL)
    offs = np.cumsum(sizes)[:-1].tolist()
    zq, zk, zv, zu, zs, zm, zg = jnp.split(z, offs, axis=-1)

    q = zq.reshape(B, T, DA_HEADS, 2, DA_DK)
    k = zk.reshape(B, T, DA_HEADS, 2, DA_DK)
    v = zv.reshape(B, T, DA_HEADS, DA_DV)
    lam_init = 0.8 - 0.6 * math.exp(-0.3 * l)
    lp = da_lam.astype(jnp.float32)
    lam = jnp.exp(jnp.sum(lp[0] * lp[1])) - jnp.exp(jnp.sum(lp[2] * lp[3])) + lam_init
    if past_k is None:
        o_da = _da_prompt(q, k, v, rel_bias, lam, lam_init, da_subln)
    else:
        o_da = _da_sample(q, k, v, past_k, past_v, rel_bias, lam, lam_init, da_subln)

    o_sg, vn = _spatial_gate(jax.nn.gelu(zu), jax.nn.gelu(zs), sg_norm, sg_w, sg_b)

    o_mx = _cross_attn(zm.reshape(B, T, MX_HEADS, MX_DH), mem_k, mem_v)

    gates = jax.nn.sigmoid(zg).reshape(B, T, N_BRANCH, D_MODEL)
    merged = (gates[:, :, 0] * (o_da.reshape(B, T, DA_V_W) @ w_branch[:DA_V_W])
              + gates[:, :, 1] * (o_sg @ w_branch[DA_V_W:DA_V_W + SG_WIDTH])
              + gates[:, :, 2] * (o_mx.reshape(B, T, MX_W) @ w_branch[DA_V_W + SG_WIDTH:]))
    x = x + merged @ w_out

    a, b = jnp.split(_rmsnorm(x, g_ffn) @ w_up, 2, axis=-1)
    x = x + (jax.nn.silu(a) * b) @ w_down
    return x, zk.reshape(B, T, DA_HEADS, 2 * DA_DK), v, vn


def setup_inputs(seed: int = 0) -> dict:
    key = jax.random.key(seed)
    ks = jax.random.split(key, 32)
    n_pages = PAST_LEN // PAGE_SIZE
    used = DEC_BATCH * n_pages
    n_pool = used + (-(-used // 4))
    f32 = jnp.float32

    def nrm(k, shape, scale=1.0):
        return jax.random.normal(k, shape, f32) * scale

    page_table = jax.random.permutation(ks[7], n_pool)[:used].reshape(DEC_BATCH, n_pages).astype(jnp.int32)
    return {
        "x_prompt": nrm(ks[0], (BATCH, SEQ, D_MODEL)),
        "x_sample": nrm(ks[1], (DEC_BATCH, DEC_SEQ, D_MODEL)),
        "mem_prompt": nrm(ks[2], (BATCH, N_MEM, D_MODEL)),
        "cache_da_k": nrm(ks[3], (DEPTH, n_pool, PAGE_SIZE, DA_HEADS, 2 * DA_DK)),
        "cache_da_v": nrm(ks[4], (DEPTH, n_pool, PAGE_SIZE, DA_HEADS, DA_DV)),
        "cache_mem_k": nrm(ks[5], (DEPTH, DEC_BATCH, N_MEM, MX_HEADS, MX_DH)),
        "cache_mem_v": nrm(ks[6], (DEPTH, DEC_BATCH, N_MEM, MX_HEADS, MX_DH)),
        "page_table": page_table,
        "g_attn": 1.0 + nrm(ks[8], (DEPTH, D_MODEL), 0.05),
        "w_in": nrm(ks[9], (DEPTH, D_MODEL, IN_W), D_MODEL ** -0.5),
        "da_lam": nrm(ks[10], (DEPTH, 4, DA_DK), 0.1),
        "da_subln": 1.0 + nrm(ks[11], (DEPTH, DA_DV), 0.05),
        "rel_bias": nrm(ks[12], (N_BUCKETS, DA_HEADS), 0.5),
        "sg_norm": 1.0 + nrm(ks[13], (DEPTH, SG_WIDTH), 0.05),
        "sg_w": nrm(ks[14], (DEPTH, SG_GROUPS, CHUNK, CHUNK), CHUNK ** -0.5),
        "sg_b": 1.0 + nrm(ks[15], (DEPTH, SG_GROUPS, CHUNK), 0.1),
        "g_mem": 1.0 + nrm(ks[16], (DEPTH, D_MODEL), 0.05),
        "w_mem_kv": nrm(ks[17], (DEPTH, D_MODEL, 2 * MX_W), D_MODEL ** -0.5),
        "w_branch": nrm(ks[18], (DEPTH, BR_IN, D_MODEL), (BR_IN // N_BRANCH) ** -0.5),
        "w_out": nrm(ks[19], (DEPTH, D_MODEL, D_MODEL), D_MODEL ** -0.5),
        "g_ffn": 1.0 + nrm(ks[20], (DEPTH, D_MODEL), 0.05),
        "w_up": nrm(ks[21], (DEPTH, D_MODEL, 2 * D_FF), D_MODEL ** -0.5),
        "w_down": nrm(ks[22], (DEPTH, D_FF, D_MODEL), D_FF ** -0.5),
        "g_final": 1.0 + nrm(ks[23], (D_MODEL,), 0.05),
    }


def reference(x_prompt, x_sample, mem_prompt, cache_da_k, cache_da_v, cache_mem_k, cache_mem_v,
              page_table, g_attn, w_in, da_lam, da_subln, rel_bias, sg_norm, sg_w, sg_b,
              g_mem, w_mem_kv, w_branch, w_out, g_ffn, w_up, w_down, g_final):
    xp, xs = x_prompt, x_sample
    dkp, dvp, dks, dvs, mkp, mvp, sgs = [], [], [], [], [], [], []
    for l in range(DEPTH):
        mk, mv = _mem_kv(mem_prompt, g_mem[l], w_mem_kv[l])
        xp, kp, vp, _ = _layer(xp, l, None, None, mk, mv, g_attn[l], w_in[l], da_lam[l], da_subln[l],
                               rel_bias, sg_norm[l], sg_w[l], sg_b[l], w_branch[l], w_out[l],
                               g_ffn[l], w_up[l], w_down[l])
        past_k = cache_da_k[l][page_table].reshape(DEC_BATCH, -1, DA_HEADS, 2, DA_DK)
        past_v = cache_da_v[l][page_table].reshape(DEC_BATCH, -1, DA_HEADS, DA_DV)
        xs, ksm, vsm, vns = _layer(xs, l, past_k, past_v, cache_mem_k[l], cache_mem_v[l], g_attn[l], w_in[l],
                                   da_lam[l], da_subln[l], rel_bias, sg_norm[l], sg_w[l], sg_b[l],
                                   w_branch[l], w_out[l], g_ffn[l], w_up[l], w_down[l])
        dkp.append(kp); dvp.append(vp); dks.append(ksm); dvs.append(vsm)
        mkp.append(mk); mvp.append(mv); sgs.append(vns)
    y_prompt = _rmsnorm(xp, g_final)
    y_sample = _rmsnorm(xs, g_final)
    return (y_prompt, y_sample, jnp.stack(dkp), jnp.stack(dvp), jnp.stack(dks), jnp.stack(dvs),
            jnp.stack(mkp), jnp.stack(mvp), jnp.stack(sgs))
```

```python
import functools
import math

import jax
import jax.numpy as jnp
from jax import lax
from jax.experimental import pallas as pl
from jax.experimental.pallas import tpu as pltpu

F32 = jnp.float32
BF16 = jnp.bfloat16

DA_HEADS = 8
DA_DK = 64
DA_DV = 2 * DA_DK
SG_GROUPS = 8
CHUNK = 128
MX_HEADS = 4
MX_DH = 256
N_BUCKETS = 32
MAX_DISTANCE = 128
N_BRANCH = 3
RMS_EPS = 1e-6
NEG_INF = -1e30

LANES = 128
BF16_SUBLANES = 16
VMEM_LIMIT_BYTES = 56 * 1024 * 1024


def _cparams(*sem):
    return pltpu.CompilerParams(dimension_semantics=sem, vmem_limit_bytes=VMEM_LIMIT_BYTES)


def _rms(x, g):
    ms = jnp.mean(x * x, axis=-1, keepdims=True)
    return x * lax.rsqrt(ms + RMS_EPS) * g


def _div_pow2(x, n):
    assert n > 0 and n & (n - 1) == 0, n
    return x >> (n.bit_length() - 1)


def _nt_dot(a, b):
    return lax.dot_general(a, b, (((1,), (1,)), ((), ())), preferred_element_type=F32)


def _norm_matmul_kernel(x_ref, g_ref, w_ref, sgn_ref, z_ref, *rest, sections, n_copy):
    copy_refs, hn_ref = rest[:n_copy], rest[n_copy]
    j = pl.program_id(1)

    @pl.when(j == 0)
    def _():
        hn_ref[...] = _rms(x_ref[...], g_ref[...]).astype(BF16)

    acc = jnp.dot(hn_ref[...], w_ref[...], preferred_element_type=F32)

    for sec, (act, copy_idx) in enumerate(sections):
        @pl.when(j == sec)
        def _(act=act, copy_idx=copy_idx):
            if act[0] == "scale":
                val = acc * act[1]
            elif act[0] == "gelu":
                val = jax.nn.gelu(acc)
            elif act[0] == "gelu_rms":
                val = _rms(jax.nn.gelu(acc), sgn_ref[...])
            elif act[0] == "sigmoid":
                val = jax.nn.sigmoid(acc)
            else:
                val = acc
            z_ref[...] = val.astype(z_ref.dtype)
            if copy_idx is not None:
                cref = copy_refs[copy_idx]
                hd = cref.shape[2]
                for hh in range(cref.shape[1]):
                    cref[:, hh, :] = val[:, hh * hd:(hh + 1) * hd]


def _norm_matmul(x, g, w, sgn, sections, z_dtype, tm, name, copy_heads=1):
    T, D = x.shape
    n_sec = len(sections)
    tn = w.shape[1] // n_sec
    n_copy = sum(1 for _, c in sections if c is not None)
    out_shape = [jax.ShapeDtypeStruct((T, n_sec * tn), z_dtype)]
    out_specs = [pl.BlockSpec((tm, tn), lambda i, j: (i, j))]
    for _ in range(n_copy):
        out_shape.append(jax.ShapeDtypeStruct((T, copy_heads, tn // copy_heads), F32))
        out_specs.append(pl.BlockSpec((tm, copy_heads, tn // copy_heads), lambda i, j: (i, 0, 0)))
    return pl.pallas_call(
        functools.partial(_norm_matmul_kernel, sections=tuple(sections), n_copy=n_copy),
        grid=(T // tm, n_sec),
        in_specs=[
            pl.BlockSpec((tm, D), lambda i, j: (i, 0)),
            pl.BlockSpec((1, D), lambda i, j: (0, 0)),
            pl.BlockSpec((D, tn), lambda i, j: (0, j)),
            pl.BlockSpec((1, tn), lambda i, j: (0, 0)),
        ],
        out_specs=out_specs,
        out_shape=out_shape,
        scratch_shapes=[pltpu.VMEM((tm, D), BF16)],
        compiler_params=_cparams("parallel", "arbitrary"),
        name=name,
    )(x, g, w, sgn)


def _t5_bias(dist, rb_ref, h):
    n = jnp.maximum(dist, 0)
    max_exact = N_BUCKETS // 2
    nf = jnp.maximum(n, 1).astype(F32)
    large = max_exact + (jnp.log(nf / max_exact) / math.log(MAX_DISTANCE / max_exact)
                         * (N_BUCKETS - max_exact)).astype(jnp.int32)
    large = jnp.minimum(large, N_BUCKETS - 1)
    bucket = jnp.where(n < max_exact, n, large)
    out = jnp.zeros(dist.shape, F32)
    for b in range(N_BUCKETS):
        out = jnp.where(bucket == b, rb_ref[b, h], out)
    return jnp.where(dist >= 0, out, NEG_INF)


def _prompt_bias_kernel(rb_ref, diag_ref, sub_ref, *, t):
    h = pl.program_id(0)
    d = lax.broadcasted_iota(jnp.int32, (t, t), 0) - lax.broadcasted_iota(jnp.int32, (t, t), 1)
    bd = _t5_bias(d, rb_ref, h)
    bs = _t5_bias(d + t, rb_ref, h)
    diag_ref[...] = jnp.concatenate([bd, bd], axis=0)
    sub_ref[...] = jnp.concatenate([bs, bs], axis=0)


def _prompt_bias(rel_bias, t):
    H = rel_bias.shape[1]
    return pl.pallas_call(
        functools.partial(_prompt_bias_kernel, t=t),
        grid=(H,),
        in_specs=[pl.BlockSpec(memory_space=pltpu.SMEM)],
        out_specs=[pl.BlockSpec((None, 2 * t, t), lambda h: (h, 0, 0))] * 2,
        out_shape=[jax.ShapeDtypeStruct((H, 2 * t, t), F32)] * 2,
        compiler_params=_cparams("arbitrary"),
        name="prompt_bias",
    )(rel_bias)


def _sample_bias_kernel(rb_ref, lam_p_ref, far_ref, last_ref, new_ref, lam_ref, *, n_tok, page, lam_init):
    rows = 2 * n_tok
    tok = lax.broadcasted_iota(jnp.int32, (rows, page), 0) & (n_tok - 1)
    col = lax.broadcasted_iota(jnp.int32, (rows, page), 1)
    for h in range(DA_HEADS):
        sl = slice(h * rows, (h + 1) * rows)
        far_ref[sl, :] = _t5_bias(jnp.full((rows, page), MAX_DISTANCE, jnp.int32), rb_ref, h)
        last_ref[sl, :] = _t5_bias(page + tok - col, rb_ref, h)
        new_ref[sl, :] = jnp.where(col < n_tok, _t5_bias(tok - col, rb_ref, h), NEG_INF)
    lp = lam_p_ref[...]
    s1 = jnp.sum(lp[0:1, :] * lp[1:2, :], axis=-1, keepdims=True)
    s2 = jnp.sum(lp[2:3, :] * lp[3:4, :], axis=-1, keepdims=True)
    lam = jnp.exp(s1) - jnp.exp(s2) + lam_init
    lam_ref[...] = jnp.broadcast_to(lam, lam_ref.shape)


def _sample_bias(rel_bias, da_lam_l, n_tok, page, lam_init):
    assert n_tok & (n_tok - 1) == 0
    rows = DA_HEADS * 2 * n_tok
    tile = jax.ShapeDtypeStruct((rows, page), F32)
    return pl.pallas_call(
        functools.partial(_sample_bias_kernel, n_tok=n_tok, page=page, lam_init=lam_init),
        in_specs=[pl.BlockSpec(memory_space=pltpu.SMEM), pl.BlockSpec(memory_space=pltpu.VMEM)],
        out_shape=[tile, tile, tile, jax.ShapeDtypeStruct((8, LANES), F32)],
        name="sample_bias",
    )(rel_bias, da_lam_l)


def _split_components(q):
    lane = lax.broadcasted_iota(jnp.int32, q.shape, 1)
    zero = jnp.zeros_like(q)
    return jnp.concatenate([jnp.where(lane < DA_DK, q, zero), jnp.where(lane >= DA_DK, q, zero)], axis=0)


def _diff_out(o1, o2, lam_row, subln, lam_init):
    o = o1 - lam_row * o2
    return _rms(o, subln) * (1.0 - lam_init)


def _softmax_step(s, v, m_ref, l_ref, acc_ref):
    m_old = m_ref[...]
    m_new = jnp.maximum(m_old, jnp.max(s, axis=-1, keepdims=True))
    alpha = jnp.exp(m_old - m_new)
    e = jnp.exp(s - m_new)
    l_ref[...] = alpha * l_ref[...] + jnp.sum(e, axis=-1, keepdims=True)
    acc_ref[...] = alpha * acc_ref[...] + jnp.dot(e.astype(BF16), v, preferred_element_type=F32)
    m_ref[...] = m_new


def _attn_prompt_kernel(rb_ref, lam_ref, q_ref, k_ref, v_ref, bd_ref, bs_ref, sub_ref, o_ref,
                        m_ref, l_ref, acc_ref, *, t, lam_init):
    h = pl.program_id(1)
    qi = pl.program_id(2)
    qq = _split_components(q_ref[...])

    m_ref[...] = jnp.full(m_ref.shape, -jnp.inf, F32)
    l_ref[...] = jnp.zeros(l_ref.shape, F32)
    acc_ref[...] = jnp.zeros(acc_ref.shape, F32)

    def tile_scores(kt):
        off = pl.multiple_of(kt * t, t)
        return _nt_dot(qq, k_ref[pl.ds(off, t), :]), v_ref[pl.ds(off, t), :]

    far = rb_ref[N_BUCKETS - 1, h]

    def far_body(kt, c):
        s, v = tile_scores(kt)
        _softmax_step(s + far, v, m_ref, l_ref, acc_ref)
        return c

    lax.fori_loop(0, jnp.maximum(qi - 1, 0), far_body, 0)

    @pl.when(qi >= 1)
    def _():
        s, v = tile_scores(qi - 1)
        _softmax_step(s + bs_ref[...], v, m_ref, l_ref, acc_ref)

    s, v = tile_scores(qi)
    _softmax_step(s + bd_ref[...], v, m_ref, l_ref, acc_ref)

    acc = acc_ref[...]
    l = l_ref[...]
    o = _diff_out(acc[:t] / l[:t], acc[t:] / l[t:], lam_ref[0:1, :], sub_ref[...], lam_init)
    o_ref[...] = o.astype(o_ref.dtype)


def _attn_prompt(z, rel_bias, lam, bias_diag, bias_sub, subln, B, S, t, lam_init):
    assert t >= MAX_DISTANCE
    nq = S // t
    H = DA_HEADS
    return pl.pallas_call(
        functools.partial(_attn_prompt_kernel, t=t, lam_init=lam_init),
        grid=(B, H, nq),
        in_specs=[
            pl.BlockSpec(memory_space=pltpu.SMEM),
            pl.BlockSpec((8, LANES), lambda b, h, qi: (0, 0)),
            pl.BlockSpec((t, DA_DV), lambda b, h, qi: (b * nq + qi, h)),
            pl.BlockSpec((S, DA_DV), lambda b, h, qi: (b, H + h)),
            pl.BlockSpec((S, DA_DV), lambda b, h, qi: (b, 2 * H + h)),
            pl.BlockSpec((None, 2 * t, t), lambda b, h, qi: (h, 0, 0)),
            pl.BlockSpec((None, 2 * t, t), lambda b, h, qi: (h, 0, 0)),
            pl.BlockSpec((1, DA_DV), lambda b, h, qi: (0, 0)),
        ],
        out_specs=pl.BlockSpec((t, DA_DV), lambda b, h, qi: (b * nq + qi, h)),
        out_shape=jax.ShapeDtypeStruct((B * S, H * DA_DV), BF16),
        scratch_shapes=[pltpu.VMEM((2 * t, 1), F32), pltpu.VMEM((2 * t, 1), F32), pltpu.VMEM((2 * t, DA_DV), F32)],
        compiler_params=_cparams("parallel", "parallel", "arbitrary"),
        name="attn_prompt",
    )(rel_bias, lam, z, z, z, bias_diag, bias_sub, subln)


def _attn_sample_kernel(pt_ref, lam_ref, q_ref, kn_ref, vn_ref, far_ref, last_ref, new_ref, sub_ref, *rest,
                        pps, n_tok, page, lam_init):
    kp_refs, vp_refs = rest[:pps], rest[pps:2 * pps]
    o_ref, qq_ref, m_ref, l_ref, acc_ref = rest[2 * pps:]
    p = pl.program_id(1)
    last = pl.num_programs(1) - 1
    hr = 2 * n_tok

    def head_rows(h):
        return slice(h * hr, (h + 1) * hr)

    def head_cols(h):
        return slice(h * DA_DV, (h + 1) * DA_DV)

    @pl.when(p == 0)
    def _():
        q = q_ref[...]
        for h in range(DA_HEADS):
            qq_ref[head_rows(h), :] = _split_components(q[:, head_cols(h)]).astype(BF16)
        m_ref[...] = jnp.full(m_ref.shape, -jnp.inf, F32)
        l_ref[...] = jnp.zeros(l_ref.shape, F32)
        acc_ref[...] = jnp.zeros(acc_ref.shape, F32)

    def scores(k_of_head):
        return jnp.concatenate([_nt_dot(qq_ref[head_rows(h), :], k_of_head(h)) for h in range(DA_HEADS)], axis=0)

    def update(s_list, v_of_head_list):
        m_old = m_ref[...]
        m_new = m_old
        for s in s_list:
            m_new = jnp.maximum(m_new, jnp.max(s, axis=-1, keepdims=True))
        alpha = jnp.exp(m_old - m_new)
        l_new = alpha * l_ref[...]
        acc = alpha * acc_ref[...]
        for s, v_of_head in zip(s_list, v_of_head_list):
            e = jnp.exp(s - m_new)
            l_new = l_new + jnp.sum(e, axis=-1, keepdims=True)
            eb = e.astype(BF16)
            acc = acc + jnp.concatenate(
                [jnp.dot(eb[head_rows(h), :], v_of_head(h), preferred_element_type=F32) for h in range(DA_HEADS)], axis=0)
        m_ref[...] = m_new
        l_ref[...] = l_new
        acc_ref[...] = acc

    far = far_ref[...]
    s_list, v_list = [], []
    for r_ in range(pps):
        s = scores(lambda h, ref=kp_refs[r_]: ref[:, h, :].astype(BF16))
        if r_ == pps - 1:
            s = s + jnp.where(p == last, last_ref[...], far)
        else:
            s = s + far
        s_list.append(s)
        v_list.append(lambda h, ref=vp_refs[r_]: ref[:, h, :].astype(BF16))
    update(s_list, v_list)

    @pl.when(p == last)
    def _():
        pad = jnp.zeros((page - n_tok, DA_DV), F32)

        def padded(ref):
            return lambda h: jnp.concatenate([ref[:, head_cols(h)], pad], axis=0).astype(BF16)

        update([scores(padded(kn_ref)) + new_ref[...]], [padded(vn_ref)])
        acc = acc_ref[...]
        l = l_ref[...]
        for h in range(DA_HEADS):
            blk, lh = acc[head_rows(h)], l[head_rows(h)]
            o = _diff_out(blk[:n_tok] / lh[:n_tok], blk[n_tok:] / lh[n_tok:], lam_ref[0:1, :], sub_ref[...], lam_init)
            o_ref[:, head_cols(h)] = o.astype(o_ref.dtype)


def _attn_sample(z, page_idx, cache_k, cache_v, lam, far, last, new, subln, N, n_tok, lam_init, pps):
    n_pages = page_idx.shape[1]
    page = cache_k.shape[1]
    width = DA_HEADS * DA_DV
    rows = DA_HEADS * 2 * n_tok
    assert (2 * n_tok) % BF16_SUBLANES == 0

    def const(shape):
        return pl.BlockSpec(shape, lambda n, p, pt: (0,) * len(shape))

    def page_spec(r):
        return pl.BlockSpec((None, page, DA_HEADS, DA_DV), lambda n, p, pt: (pt[n, p * pps + r], 0, 0, 0))

    grid_spec = pltpu.PrefetchScalarGridSpec(
        num_scalar_prefetch=1,
        grid=(N, n_pages // pps),
        in_specs=[
            const((8, LANES)),
            pl.BlockSpec((n_tok, width), lambda n, p, pt: (n, 0)),
            pl.BlockSpec((n_tok, width), lambda n, p, pt: (n, 1)),
            pl.BlockSpec((n_tok, width), lambda n, p, pt: (n, 2)),
            const((rows, page)), const((rows, page)), const((rows, page)),
            const((1, DA_DV)),
        ] + [page_spec(r) for r in range(pps)] + [page_spec(r) for r in range(pps)],
        out_specs=pl.BlockSpec((n_tok, width), lambda n, p, pt: (n, 0)),
        scratch_shapes=[pltpu.VMEM((rows, DA_DV), BF16), pltpu.VMEM((rows, 1), F32), pltpu.VMEM((rows, 1), F32),
                        pltpu.VMEM((rows, DA_DV), F32)],
    )
    return pl.pallas_call(
        functools.partial(_attn_sample_kernel, pps=pps, n_tok=n_tok, page=page, lam_init=lam_init),
        grid_spec=grid_spec,
        out_shape=jax.ShapeDtypeStruct((N * n_tok, width), F32),
        compiler_params=_cparams("parallel", "arbitrary"),
        name="attn_sample",
    )(page_idx, lam, z, z, z, far, last, new, subln, *([cache_k] * pps), *([cache_v] * pps))


def _sgate_kernel(u_ref, vn_ref, w_ref, b_ref, o_ref, *, seg, n_chunk):
    i = lax.broadcasted_iota(jnp.int32, (CHUNK, CHUNK), 0)
    j = lax.broadcasted_iota(jnp.int32, (CHUNK, CHUNK), 1)
    keep = (_div_pow2(i, seg) == _div_pow2(j, seg)) & (j <= i)
    gd = u_ref.shape[1] // SG_GROUPS
    for g in range(SG_GROUPS):
        w = jnp.where(keep, w_ref[g], 0.0).astype(BF16)
        cols = slice(g * gd, (g + 1) * gd)
        for c in range(n_chunk):
            rws = slice(c * CHUNK, (c + 1) * CHUNK)
            mix = jnp.dot(w, vn_ref[rws, cols].astype(BF16), preferred_element_type=F32) + b_ref[:, cols]
            o_ref[rws, cols] = (u_ref[rws, cols].astype(F32) * mix).astype(o_ref.dtype)


def _sgate(z, w, b, seg, u_sec, vn_sec, n_chunk):
    T = z.shape[0]
    width = b.shape[1]
    rb = n_chunk * CHUNK
    return pl.pallas_call(
        functools.partial(_sgate_kernel, seg=seg, n_chunk=n_chunk),
        grid=(T // rb,),
        in_specs=[
            pl.BlockSpec((rb, width), lambda i: (i, u_sec)),
            pl.BlockSpec((rb, width), lambda i: (i, vn_sec)),
            pl.BlockSpec((SG_GROUPS, CHUNK, CHUNK), lambda i: (0, 0, 0)),
            pl.BlockSpec((CHUNK, width), lambda i: (0, 0)),
        ],
        out_specs=pl.BlockSpec((rb, width), lambda i: (i, 0)),
        out_shape=jax.ShapeDtypeStruct((T, width), BF16),
        compiler_params=_cparams("parallel"),
        name="sgate",
    )(z, z, w, b)


def _xattn_prompt_kernel(q_ref, k_ref, v_ref, o_ref):
    for h in range(MX_HEADS):
        cols = slice(h * MX_DH, (h + 1) * MX_DH)
        s = _nt_dot(q_ref[:, cols], k_ref[:, cols])
        e = jnp.exp(s - jnp.max(s, axis=-1, keepdims=True))
        o = jnp.dot(e.astype(BF16), v_ref[:, cols], preferred_element_type=F32)
        o_ref[:, cols] = (o / jnp.sum(e, axis=-1, keepdims=True)).astype(o_ref.dtype)


def _xattn_prompt(z, mem_kv, B, S, tq, q_sec):
    nq = S // tq
    n_mem = mem_kv.shape[0] // B
    width = MX_HEADS * MX_DH
    return pl.pallas_call(
        _xattn_prompt_kernel,
        grid=(B, nq),
        in_specs=[
            pl.BlockSpec((tq, width), lambda b, i: (b * nq + i, q_sec)),
            pl.BlockSpec((n_mem, width), lambda b, i: (b, 0)),
            pl.BlockSpec((n_mem, width), lambda b, i: (b, 1)),
        ],
        out_specs=pl.BlockSpec((tq, width), lambda b, i: (b * nq + i, 0)),
        out_shape=jax.ShapeDtypeStruct((B * S, width), BF16),
        compiler_params=_cparams("parallel", "arbitrary"),
        name="xattn_prompt",
    )(z, mem_kv, mem_kv)


def _xattn_sample_kernel(q_ref, k_ref, v_ref, o_ref, *, n_tok):
    q = q_ref[...]
    pad = jnp.zeros((BF16_SUBLANES - n_tok, MX_DH), F32)
    for h in range(MX_HEADS):
        cols = slice(h * MX_DH, (h + 1) * MX_DH)
        qh = jnp.concatenate([q[:, cols], pad], axis=0).astype(BF16)
        s = _nt_dot(qh, k_ref[:, h, :].astype(BF16))
        e = jnp.exp(s - jnp.max(s, axis=-1, keepdims=True))
        o = jnp.dot(e.astype(BF16), v_ref[:, h, :].astype(BF16), preferred_element_type=F32)
        o = o / jnp.sum(e, axis=-1, keepdims=True)
        o_ref[:, cols] = o[:n_tok].astype(o_ref.dtype)


def _xattn_sample(z, mem_k, mem_v, N, n_tok, q_sec, seq_off):
    n_mem = mem_k.shape[1]
    width = MX_HEADS * MX_DH
    assert n_tok <= BF16_SUBLANES
    mem_spec = pl.BlockSpec((None, n_mem, MX_HEADS, MX_DH), lambda n: (seq_off + n, 0, 0, 0))
    return pl.pallas_call(
        functools.partial(_xattn_sample_kernel, n_tok=n_tok),
        grid=(N,),
        in_specs=[pl.BlockSpec((n_tok, width), lambda n: (n, q_sec)), mem_spec, mem_spec],
        out_specs=pl.BlockSpec((n_tok, width), lambda n: (n, 0)),
        out_shape=jax.ShapeDtypeStruct((N * n_tok, width), F32),
        compiler_params=_cparams("parallel"),
        name="xattn_sample",
    )(z, mem_k, mem_v)


def _post_kernel(x_ref, da_ref, sg_ref, mx_ref, gate_ref, wb_ref, wo_ref, gf_ref, wu_ref, wd_ref, gfin_ref,
                 o_ref, y_ref, *, d, d_ff, fc, final_norm):
    branches = (da_ref, sg_ref, mx_ref)
    merged = None
    for k in range(N_BRANCH):
        br = jnp.dot(branches[k][...].astype(BF16), wb_ref[k * d:(k + 1) * d, :], preferred_element_type=F32)
        term = gate_ref[:, k * d:(k + 1) * d].astype(F32) * br
        merged = term if merged is None else merged + term
    x1 = x_ref[...] + jnp.dot(merged.astype(BF16), wo_ref[...], preferred_element_type=F32)
    h2 = _rms(x1, gf_ref[...]).astype(BF16)
    y_ref[...] = x1
    for c in range(d_ff // fc):
        a = jnp.dot(h2, wu_ref[:, c * fc:(c + 1) * fc], preferred_element_type=F32)
        b = jnp.dot(h2, wu_ref[:, d_ff + c * fc:d_ff + (c + 1) * fc], preferred_element_type=F32)
        act = (jax.nn.silu(a) * b).astype(BF16)
        y_ref[...] += jnp.dot(act, wd_ref[c * fc:(c + 1) * fc, :], preferred_element_type=F32)
    y = y_ref[...]
    o_ref[...] = _rms(y, gfin_ref[...]) if final_norm else y


def _post(x, o_da, o_sg, o_mx, z, gate_sec, wb, wo, g_ffn, wu, wd, g_final, tm, final_norm):
    T, d = x.shape
    d_ff = wd.shape[0]
    fc = 256
    row = lambda i: (i, 0)
    whole = lambda i: (0, 0)

    def resident(shape):
        return pl.BlockSpec(shape, whole, pipeline_mode=pl.Buffered(1))

    return pl.pallas_call(
        functools.partial(_post_kernel, d=d, d_ff=d_ff, fc=fc, final_norm=final_norm),
        grid=(T // tm,),
        in_specs=[
            pl.BlockSpec((tm, d), row), pl.BlockSpec((tm, d), row), pl.BlockSpec((tm, d), row), pl.BlockSpec((tm, d), row),
            pl.BlockSpec((tm, N_BRANCH * d), lambda i: (i, gate_sec)),
            resident(wb.shape), resident(wo.shape), resident((1, d)), resident(wu.shape), resident(wd.shape),
            resident((1, d)),
        ],
        out_specs=pl.BlockSpec((tm, d), row),
        out_shape=jax.ShapeDtypeStruct((T, d), F32),
        scratch_shapes=[pltpu.VMEM((tm, d), F32)],
        compiler_params=_cparams("parallel"),
        name="post",
    )(x, o_da, o_sg, o_mx, z, wb, wo, g_ffn, wu, wd, g_final)


_IN_SECTIONS = ("q", "k", "v", "u", "s", "m", "g0", "g1", "g2")


def _in_sections(copy_kv):
    acts = {
        "q": ("scale", DA_DK ** -0.5), "k": ("none",), "v": ("none",), "u": ("gelu",), "s": ("gelu_rms",),
        "m": ("scale", MX_DH ** -0.5), "g0": ("sigmoid",), "g1": ("sigmoid",), "g2": ("sigmoid",),
    }
    copies = {"k": 0, "v": 1} if copy_kv else {}
    return [(acts[n], copies.get(n)) for n in _IN_SECTIONS]


def kernel(x_prompt, x_sample, mem_prompt, cache_da_k, cache_da_v, cache_mem_k, cache_mem_v, page_table, g_attn, w_in,
           da_lam, da_subln, rel_bias, sg_norm, sg_w, sg_b, g_mem, w_mem_kv, w_branch, w_out, g_ffn, w_up, w_down,
           g_final):
    B, S, D = x_prompt.shape
    N, n_tok, _ = x_sample.shape
    depth, n_pool, page = cache_da_k.shape[:3]
    n_mem = mem_prompt.shape[1]
    n_pages = page_table.shape[1]
    width = DA_HEADS * DA_DV
    assert D == width == SG_GROUPS * CHUNK == MX_HEADS * MX_DH and page == CHUNK and n_tok <= CHUNK

    t_attn = 256
    pps = 4 if n_pages % 4 == 0 else 1
    tm_p = 512
    tm_s = min(512, N * n_tok)
    tm_post = 256
    sec = {n: i for i, n in enumerate(_IN_SECTIONS)}

    xp = x_prompt.reshape(B * S, D)
    xs = x_sample.reshape(N * n_tok, D)
    mem = mem_prompt.reshape(B * n_mem, D)
    ck = cache_da_k.reshape(depth * n_pool, page, DA_HEADS, DA_DV)
    cv = cache_da_v.reshape(depth * n_pool, page, DA_HEADS, DA_DV)
    cmk = cache_mem_k.reshape(depth * N, n_mem, MX_HEADS, MX_DH)
    cmv = cache_mem_v.reshape(depth * N, n_mem, MX_HEADS, MX_DH)
    row = lambda a: a.reshape(1, -1)

    bias_diag, bias_sub = _prompt_bias(rel_bias, t_attn)
    seg_s = n_tok
    outs = {k: [] for k in ("dkp", "dvp", "dks", "dvs", "mkp", "mvp", "sgs")}
    for l in range(depth):
        lam_init = 0.8 - 0.6 * math.exp(-0.3 * l)
        w_in_l = w_in[l].astype(BF16)
        wb_l, wo_l = w_branch[l].astype(BF16), w_out[l].astype(BF16)
        wu_l, wd_l = w_up[l].astype(BF16), w_down[l].astype(BF16)
        far, last, new, lam = _sample_bias(rel_bias, da_lam[l], n_tok, page, lam_init)
        subln = row(da_subln[l])
        sgw_p = sg_w[l]
        sgb_p = jnp.repeat(sg_b[l].T, CHUNK, axis=1)
        sgw_s = jnp.tile(sg_w[l][:, :seg_s, :seg_s], (1, CHUNK // seg_s, CHUNK // seg_s))
        sgb_s = jnp.repeat(jnp.tile(sg_b[l][:, :seg_s].T, (CHUNK // seg_s, 1)), CHUNK, axis=1)

        mkv16, mk32, mv32 = _norm_matmul(mem, row(g_mem[l]), w_mem_kv[l].astype(BF16), row(sg_norm[l]),
                                         [(("none",), 0), (("none",), 1)], BF16, min(512, B * n_mem), "mem_kv",
                                         copy_heads=MX_HEADS)
        zp, kp32, vp32 = _norm_matmul(xp, row(g_attn[l]), w_in_l, row(sg_norm[l]), _in_sections(True), BF16, tm_p,
                                      "inproj_prompt", copy_heads=DA_HEADS)
        oda_p = _attn_prompt(zp, rel_bias, lam, bias_diag, bias_sub, subln, B, S, t_attn, lam_init)
        osg_p = _sgate(zp, sgw_p, sgb_p, CHUNK, sec["u"], sec["s"], 4)
        omx_p = _xattn_prompt(zp, mkv16, B, S, 512, sec["m"])
        xp = _post(xp, oda_p, osg_p, omx_p, zp, sec["g0"] // N_BRANCH, wb_l, wo_l, row(g_ffn[l]), wu_l, wd_l,
                   row(g_final), tm_post, l == depth - 1)

        zs, = _norm_matmul(xs, row(g_attn[l]), w_in_l, row(sg_norm[l]), _in_sections(False), F32, tm_s,
                           "inproj_sample")
        oda_s = _attn_sample(zs, page_table + l * n_pool, ck, cv, lam, far, last, new, subln, N, n_tok, lam_init, pps)
        osg_s = _sgate(zs, sgw_s, sgb_s, seg_s, sec["u"], sec["s"], 1)
        omx_s = _xattn_sample(zs, cmk, cmv, N, n_tok, sec["m"], l * N)
        xs = _post(xs, oda_s, osg_s, omx_s, zs, sec["g0"] // N_BRANCH, wb_l, wo_l, row(g_ffn[l]), wu_l, wd_l,
                   row(g_final), min(tm_post, N * n_tok), l == depth - 1)

        outs["dkp"].append(kp32.reshape(B, S, DA_HEADS, 2 * DA_DK))
        outs["dvp"].append(vp32.reshape(B, S, DA_HEADS, DA_DV))
        outs["dks"].append(zs[:, sec["k"] * D:(sec["k"] + 1) * D].reshape(N, n_tok, DA_HEADS, 2 * DA_DK))
        outs["dvs"].append(zs[:, sec["v"] * D:(sec["v"] + 1) * D].reshape(N, n_tok, DA_HEADS, DA_DV))
        outs["mkp"].append(mk32.reshape(B, n_mem, MX_HEADS, MX_DH))
        outs["mvp"].append(mv32.reshape(B, n_mem, MX_HEADS, MX_DH))
        outs["sgs"].append(zs[:, sec["s"] * D:(sec["s"] + 1) * D].reshape(N, n_tok, D))

    return (xp.reshape(B, S, D), xs.reshape(N, n_tok, D), jnp.stack(outs["dkp"]), jnp.stack(outs["dvp"]),
            jnp.stack(outs["dks"]), jnp.stack(outs["dvs"]), jnp.stack(outs["mkp"]), jnp.stack(outs["mvp"]),
            jnp.stack(outs["sgs"]))
```

```python
import functools
import math

import jax
import jax.numpy as jnp
from jax import lax
from jax.experimental import pallas as pl
from jax.experimental.pallas import tpu as pltpu

F32 = jnp.float32
BF16 = jnp.bfloat16

DA_HEADS = 8
DA_DK = 64
DA_DV = 2 * DA_DK
SG_GROUPS = 8
CHUNK = 128
MX_HEADS = 4
MX_DH = 256
N_BUCKETS = 32
MAX_DISTANCE = 128
N_BRANCH = 3
RMS_EPS = 1e-6
NEG_INF = -1e30
LOG2E = math.log2(math.e)

LANES = 128
BF16_SUBLANES = 16
VMEM_LIMIT_BYTES = 56 * 1024 * 1024


def _cparams(*sem):
    return pltpu.CompilerParams(dimension_semantics=sem, vmem_limit_bytes=VMEM_LIMIT_BYTES)


def _rms(x, g):
    ms = jnp.mean(x * x, axis=-1, keepdims=True)
    return x * lax.rsqrt(ms + RMS_EPS) * g


def _div_pow2(x, n):
    assert n > 0 and n & (n - 1) == 0, n
    return x >> (n.bit_length() - 1)


def _nt_dot(a, b):
    return lax.dot_general(a, b, (((1,), (1,)), ((), ())), preferred_element_type=F32)


def _lane_tile(x, n):
    return x if n == 1 else jnp.concatenate([x] * n, axis=1)


def _norm_matmul_kernel(x_ref, g_ref, w_ref, sgn_ref, z_ref, *rest, sections, n_copy):
    copy_refs, hn_ref = rest[:n_copy], rest[n_copy]
    j = pl.program_id(1)

    @pl.when(j == 0)
    def _():
        hn_ref[...] = _rms(x_ref[...], g_ref[...]).astype(BF16)

    acc = jnp.dot(hn_ref[...], w_ref[...], preferred_element_type=F32)

    for sec, (act, copy_idx) in enumerate(sections):
        @pl.when(j == sec)
        def _(act=act, copy_idx=copy_idx):
            if act[0] == "scale":
                val = acc * act[1]
            elif act[0] == "gelu":
                val = jax.nn.gelu(acc)
            elif act[0] == "gelu_rms":
                val = _rms(jax.nn.gelu(acc), sgn_ref[...])
            elif act[0] == "sigmoid":
                val = jax.nn.sigmoid(acc)
            else:
                val = acc
            z_ref[...] = val.astype(z_ref.dtype)
            if copy_idx is not None:
                cref = copy_refs[copy_idx]
                hd = cref.shape[2]
                for hh in range(cref.shape[1]):
                    cref[:, hh, :] = val[:, hh * hd:(hh + 1) * hd]


def _norm_matmul(x, g, w, sgn, sections, z_dtype, tm, name, copy_heads=1):
    T, D = x.shape
    n_sec = len(sections)
    tn = w.shape[1] // n_sec
    n_copy = sum(1 for _, c in sections if c is not None)
    out_shape = [jax.ShapeDtypeStruct((T, n_sec * tn), z_dtype)]
    out_specs = [pl.BlockSpec((tm, tn), lambda i, j: (i, j))]
    for _ in range(n_copy):
        out_shape.append(jax.ShapeDtypeStruct((T, copy_heads, tn // copy_heads), F32))
        out_specs.append(pl.BlockSpec((tm, copy_heads, tn // copy_heads), lambda i, j: (i, 0, 0)))
    return pl.pallas_call(
        functools.partial(_norm_matmul_kernel, sections=tuple(sections), n_copy=n_copy),
        grid=(T // tm, n_sec),
        in_specs=[
            pl.BlockSpec((tm, D), lambda i, j: (i, 0)),
            pl.BlockSpec((1, D), lambda i, j: (0, 0)),
            pl.BlockSpec((D, tn), lambda i, j: (0, j)),
            pl.BlockSpec((1, tn), lambda i, j: (0, 0)),
        ],
        out_specs=out_specs,
        out_shape=out_shape,
        scratch_shapes=[pltpu.VMEM((tm, D), BF16)],
        compiler_params=_cparams("parallel", "arbitrary"),
        name=name,
    )(x, g, w, sgn)


def _t5_bias(dist, rb_ref, h):
    n = jnp.maximum(dist, 0)
    max_exact = N_BUCKETS // 2
    nf = jnp.maximum(n, 1).astype(F32)
    large = max_exact + (jnp.log(nf / max_exact) / math.log(MAX_DISTANCE / max_exact)
                         * (N_BUCKETS - max_exact)).astype(jnp.int32)
    large = jnp.minimum(large, N_BUCKETS - 1)
    bucket = jnp.where(n < max_exact, n, large)
    out = jnp.zeros(dist.shape, F32)
    for b in range(N_BUCKETS):
        out = jnp.where(bucket == b, rb_ref[b, h], out)
    return jnp.where(dist >= 0, out * LOG2E, NEG_INF)


def _far_bias(rb_ref, h):
    return rb_ref[N_BUCKETS - 1, h] * LOG2E


def _prompt_bias_kernel(rb_ref, diag_ref, sub_ref, *, t):
    h = pl.program_id(0)
    d = lax.broadcasted_iota(jnp.int32, (CHUNK, CHUNK), 0) - lax.broadcasted_iota(jnp.int32, (CHUNK, CHUNK), 1)
    near = (_t5_bias(d, rb_ref, h), _t5_bias(d + CHUNK, rb_ref, h))
    far = jnp.full((CHUNK, CHUNK), _far_bias(rb_ref, h), F32)
    masked = jnp.full((CHUNK, CHUNK), NEG_INF, F32)

    def block(bd):
        return masked if bd < 0 else near[bd] if bd < 2 else far

    nb = t // CHUNK
    for bi in range(nb):
        for bj in range(nb):
            sl = (slice(bi * CHUNK, (bi + 1) * CHUNK), slice(bj * CHUNK, (bj + 1) * CHUNK))
            diag_ref[sl] = block(bi - bj)
            sub_ref[sl] = block(nb + bi - bj)


def _prompt_bias(rel_bias, t):
    assert MAX_DISTANCE <= CHUNK + 1 and t % CHUNK == 0
    H = rel_bias.shape[1]
    return pl.pallas_call(
        functools.partial(_prompt_bias_kernel, t=t),
        grid=(H,),
        in_specs=[pl.BlockSpec(memory_space=pltpu.SMEM)],
        out_specs=[pl.BlockSpec((None, t, t), lambda h: (h, 0, 0))] * 2,
        out_shape=[jax.ShapeDtypeStruct((H, t, t), F32)] * 2,
        compiler_params=_cparams("arbitrary"),
        name="prompt_bias",
    )(rel_bias)


def _sample_bias_kernel(rb_ref, lam_p_ref, far_ref, last_ref, new_ref, lam_ref, *, n_tok, page, lam_init):
    hr = 2 * n_tok

    def grid(cols):
        tok = lax.broadcasted_iota(jnp.int32, (hr, cols), 0) & (n_tok - 1)
        col = lax.broadcasted_iota(jnp.int32, (hr, cols), 1)
        return tok, col, _div_pow2(col, DA_HEADS), col & (DA_HEADS - 1)

    tok, _, key, kh = grid(page * DA_HEADS)
    tok_n, col_n, key_n, kh_n = grid(page)
    for h in range(DA_HEADS):
        sl = slice(h * hr, (h + 1) * hr)
        far_ref[sl, :] = jnp.where(kh == h, _far_bias(rb_ref, h), NEG_INF)
        last_ref[sl, :] = jnp.where(kh == h, _t5_bias(page + tok - key, rb_ref, h), NEG_INF)
        new_ref[sl, :] = jnp.where((kh_n == h) & (col_n < n_tok * DA_HEADS), _t5_bias(tok_n - key_n, rb_ref, h), NEG_INF)
    lp = lam_p_ref[...]
    s1 = jnp.sum(lp[0:1, :] * lp[1:2, :], axis=-1, keepdims=True)
    s2 = jnp.sum(lp[2:3, :] * lp[3:4, :], axis=-1, keepdims=True)
    lam = jnp.exp(s1) - jnp.exp(s2) + lam_init
    lam_ref[...] = jnp.broadcast_to(lam, lam_ref.shape)


def _sample_bias(rel_bias, da_lam_l, n_tok, page, lam_init):
    assert n_tok & (n_tok - 1) == 0 and n_tok * DA_HEADS <= page
    rows = DA_HEADS * 2 * n_tok
    wide = jax.ShapeDtypeStruct((rows, page * DA_HEADS), F32)
    return pl.pallas_call(
        functools.partial(_sample_bias_kernel, n_tok=n_tok, page=page, lam_init=lam_init),
        in_specs=[pl.BlockSpec(memory_space=pltpu.SMEM), pl.BlockSpec(memory_space=pltpu.VMEM)],
        out_shape=[wide, wide, jax.ShapeDtypeStruct((rows, page), F32), jax.ShapeDtypeStruct((8, LANES), F32)],
        name="sample_bias",
    )(rel_bias, da_lam_l)


def _split_components(q):
    lane = lax.broadcasted_iota(jnp.int32, q.shape, 1)
    zero = jnp.zeros_like(q)
    return jnp.concatenate([jnp.where(lane < DA_DK, q, zero), jnp.where(lane >= DA_DK, q, zero)], axis=0)


def _diff_out(o1, o2, lam_row, subln, lam_init):
    o = o1 - lam_row * o2
    return _rms(o, subln) * (1.0 - lam_init)


def _softmax_update(m, acc, s_list, v_list):
    m_new = m
    for s in s_list:
        m_new = jnp.maximum(m_new, jnp.max(s, axis=-1, keepdims=True))
    acc = acc * _lane_tile(jnp.exp2(m - m_new), 2)
    for s, v in zip(s_list, v_list):
        e = jnp.exp2(s - _lane_tile(m_new, s.shape[1] // LANES)).astype(BF16)
        v1 = jnp.concatenate([v, jnp.ones(v.shape, BF16)], axis=1)
        acc = acc + jnp.dot(e, v1, preferred_element_type=F32)
    return m_new, acc


def _attn_prompt_kernel(rb_ref, lam_ref, q_ref, k_ref, v_ref, bd_ref, bs_ref, sub_ref, o_ref, *, t, nq, lam_init):
    far = _far_bias(rb_ref, pl.program_id(1))
    for qi in range(nq):
        rows = slice(qi * t, (qi + 1) * t)
        qq = _split_components(q_ref[rows, :])
        m = jnp.full((2 * t, LANES), -jnp.inf, F32)
        acc = jnp.zeros((2 * t, 2 * LANES), F32)
        for kt in range(qi + 1):
            keys = slice(kt * t, (kt + 1) * t)
            s = _nt_dot(qq, k_ref[keys, :])
            if kt < qi - 1:
                s = s + far
            else:
                b = bd_ref[...] if kt == qi else bs_ref[...]
                s = jnp.concatenate([s[:t] + b, s[t:] + b], axis=0)
            m, acc = _softmax_update(m, acc, [s], [v_ref[keys, :]])
        o = acc[:, :DA_DV] / acc[:, DA_DV:]
        out = _diff_out(o[:t], o[t:], lam_ref[0:1, :], sub_ref[...], lam_init)
        o_ref[rows, :] = out.astype(o_ref.dtype)


def _attn_prompt(z, rel_bias, lam, bias_diag, bias_sub, subln, B, S, t, lam_init):
    assert t >= MAX_DISTANCE and S % t == 0
    H = DA_HEADS
    return pl.pallas_call(
        functools.partial(_attn_prompt_kernel, t=t, nq=S // t, lam_init=lam_init),
        grid=(B, H),
        in_specs=[
            pl.BlockSpec(memory_space=pltpu.SMEM),
            pl.BlockSpec((8, LANES), lambda b, h: (0, 0)),
            pl.BlockSpec((S, DA_DV), lambda b, h: (b, h)),
            pl.BlockSpec((S, DA_DV), lambda b, h: (b, H + h)),
            pl.BlockSpec((S, DA_DV), lambda b, h: (b, 2 * H + h)),
            pl.BlockSpec((None, t, t), lambda b, h: (h, 0, 0)),
            pl.BlockSpec((None, t, t), lambda b, h: (h, 0, 0)),
            pl.BlockSpec((1, DA_DV), lambda b, h: (0, 0)),
        ],
        out_specs=pl.BlockSpec((S, DA_DV), lambda b, h: (b, h)),
        out_shape=jax.ShapeDtypeStruct((B * S, H * DA_DV), BF16),
        compiler_params=_cparams("parallel", "arbitrary"),
        name="attn_prompt",
    )(rel_bias, lam, z, z, z, bias_diag, bias_sub, subln)


def _attn_sample_kernel(pt_ref, lam_ref, q_ref, kn_ref, vn_ref, far_ref, last_ref, new_ref, sub_ref, *rest,
                        pps, n_tok, page, lam_init):
    kp_refs, vp_refs = rest[:pps], rest[pps:2 * pps]
    o_ref, qq_ref, m_ref, acc_ref = rest[2 * pps:]
    p = pl.program_id(1)
    last = pl.num_programs(1) - 1
    hr = 2 * n_tok
    flat = page * DA_HEADS

    @pl.when(p == 0)
    def _():
        q = q_ref[...]
        for h in range(DA_HEADS):
            qq_ref[h * hr:(h + 1) * hr, :] = _split_components(q[:, h * DA_DV:(h + 1) * DA_DV]).astype(BF16)
        m_ref[...] = jnp.full(m_ref.shape, -jnp.inf, F32)
        acc_ref[...] = jnp.zeros(acc_ref.shape, F32)

    qq = qq_ref[...]
    far = far_ref[...]
    s_list, v_list = [], []
    for r in range(pps):
        s = _nt_dot(qq, kp_refs[r][...].reshape(flat, DA_DV).astype(BF16))
        if r == pps - 1:
            s = s + jnp.where(p == last, last_ref[...], far)
        else:
            s = s + far
        s_list.append(s)
        v_list.append(vp_refs[r][...].reshape(flat, DA_DV).astype(BF16))
    m, acc = _softmax_update(m_ref[...], acc_ref[...], s_list, v_list)
    m_ref[...] = m
    acc_ref[...] = acc

    @pl.when(p == last)
    def _():
        pad = jnp.zeros((page - n_tok * DA_HEADS, DA_DV), F32)

        def new_rows(ref):
            return jnp.concatenate([ref[...].reshape(n_tok * DA_HEADS, DA_DV), pad], axis=0).astype(BF16)

        _, acc2 = _softmax_update(m, acc, [_nt_dot(qq, new_rows(kn_ref)) + new_ref[...]], [new_rows(vn_ref)])
        o = acc2[:, :DA_DV] / acc2[:, DA_DV:]
        for h in range(DA_HEADS):
            oh = o[h * hr:(h + 1) * hr]
            out = _diff_out(oh[:n_tok], oh[n_tok:], lam_ref[0:1, :], sub_ref[...], lam_init)
            o_ref[:, h * DA_DV:(h + 1) * DA_DV] = out.astype(o_ref.dtype)


def _attn_sample(z, k_new, v_new, page_idx, cache_k, cache_v, lam, far, last, new, subln, N, n_tok, lam_init, pps):
    n_pages = page_idx.shape[1]
    page = cache_k.shape[1]
    width = DA_HEADS * DA_DV
    rows = DA_HEADS * 2 * n_tok
    assert (2 * n_tok) % BF16_SUBLANES == 0 and n_pages % pps == 0

    def const(shape):
        return pl.BlockSpec(shape, lambda n, p, pt: (0,) * len(shape))

    def page_spec(r):
        return pl.BlockSpec((None, page, DA_HEADS, DA_DV), lambda n, p, pt: (pt[n, p * pps + r], 0, 0, 0))

    new_spec = pl.BlockSpec((n_tok, DA_HEADS, DA_DV), lambda n, p, pt: (n, 0, 0))
    grid_spec = pltpu.PrefetchScalarGridSpec(
        num_scalar_prefetch=1,
        grid=(N, n_pages // pps),
        in_specs=[
            const((8, LANES)),
            pl.BlockSpec((n_tok, width), lambda n, p, pt: (n, 0)),
            new_spec, new_spec,
            const((rows, page * DA_HEADS)), const((rows, page * DA_HEADS)), const((rows, page)),
            const((1, DA_DV)),
        ] + [page_spec(r) for r in range(pps)] + [page_spec(r) for r in range(pps)],
        out_specs=pl.BlockSpec((n_tok, width), lambda n, p, pt: (n, 0)),
        scratch_shapes=[pltpu.VMEM((rows, DA_DV), BF16), pltpu.VMEM((rows, LANES), F32),
                        pltpu.VMEM((rows, 2 * LANES), F32)],
    )
    return pl.pallas_call(
        functools.partial(_attn_sample_kernel, pps=pps, n_tok=n_tok, page=page, lam_init=lam_init),
        grid_spec=grid_spec,
        out_shape=jax.ShapeDtypeStruct((N * n_tok, width), F32),
        compiler_params=_cparams("parallel", "arbitrary"),
        name="attn_sample",
    )(page_idx, lam, z, k_new, v_new, far, last, new, subln, *([cache_k] * pps), *([cache_v] * pps))


def _sgate_kernel(u_ref, vn_ref, w_ref, b_ref, o_ref, *, seg, n_chunk):
    i = lax.broadcasted_iota(jnp.int32, (CHUNK, CHUNK), 0)
    j = lax.broadcasted_iota(jnp.int32, (CHUNK, CHUNK), 1)
    keep = (_div_pow2(i, seg) == _div_pow2(j, seg)) & (j <= i)
    gd = u_ref.shape[1] // SG_GROUPS
    for g in range(SG_GROUPS):
        w = jnp.where(keep, w_ref[g], 0.0).astype(BF16)
        cols = slice(g * gd, (g + 1) * gd)
        for c in range(n_chunk):
            rws = slice(c * CHUNK, (c + 1) * CHUNK)
            mix = jnp.dot(w, vn_ref[rws, cols].astype(BF16), preferred_element_type=F32) + b_ref[:, cols]
            o_ref[rws, cols] = (u_ref[rws, cols].astype(F32) * mix).astype(o_ref.dtype)


def _sgate(z, w, b, seg, u_sec, vn_sec, n_chunk):
    T = z.shape[0]
    width = b.shape[1]
    rb = n_chunk * CHUNK
    return pl.pallas_call(
        functools.partial(_sgate_kernel, seg=seg, n_chunk=n_chunk),
        grid=(T // rb,),
        in_specs=[
            pl.BlockSpec((rb, width), lambda i: (i, u_sec)),
            pl.BlockSpec((rb, width), lambda i: (i, vn_sec)),
            pl.BlockSpec((SG_GROUPS, CHUNK, CHUNK), lambda i: (0, 0, 0)),
            pl.BlockSpec((CHUNK, width), lambda i: (0, 0)),
        ],
        out_specs=pl.BlockSpec((rb, width), lambda i: (i, 0)),
        out_shape=jax.ShapeDtypeStruct((T, width), BF16),
        compiler_params=_cparams("parallel"),
        name="sgate",
    )(z, z, w, b)


def _xattn_prompt_kernel(q_ref, k_ref, v_ref, o_ref):
    for h in range(MX_HEADS):
        cols = slice(h * MX_DH, (h + 1) * MX_DH)
        s = _nt_dot(q_ref[:, cols], k_ref[:, cols])
        e = jnp.exp(s - jnp.max(s, axis=-1, keepdims=True))
        o = jnp.dot(e.astype(BF16), v_ref[:, cols], preferred_element_type=F32)
        o_ref[:, cols] = (o / jnp.sum(e, axis=-1, keepdims=True)).astype(o_ref.dtype)


def _xattn_prompt(z, mem_kv, B, S, tq, q_sec):
    nq = S // tq
    n_mem = mem_kv.shape[0] // B
    width = MX_HEADS * MX_DH
    return pl.pallas_call(
        _xattn_prompt_kernel,
        grid=(B, nq),
        in_specs=[
            pl.BlockSpec((tq, width), lambda b, i: (b * nq + i, q_sec)),
            pl.BlockSpec((n_mem, width), lambda b, i: (b, 0)),
            pl.BlockSpec((n_mem, width), lambda b, i: (b, 1)),
        ],
        out_specs=pl.BlockSpec((tq, width), lambda b, i: (b * nq + i, 0)),
        out_shape=jax.ShapeDtypeStruct((B * S, width), BF16),
        compiler_params=_cparams("parallel", "arbitrary"),
        name="xattn_prompt",
    )(z, mem_kv, mem_kv)


def _xattn_sample_kernel(q_ref, k_ref, v_ref, o_ref, *, n_tok):
    n_mem = k_ref.shape[0]
    rows, flat = MX_HEADS * n_tok, n_mem * MX_HEADS
    q = q_ref[...]
    qq = jnp.concatenate([q[:, h * MX_DH:(h + 1) * MX_DH] for h in range(MX_HEADS)], axis=0).astype(BF16)
    r = lax.broadcasted_iota(jnp.int32, (rows, flat), 0)
    c = lax.broadcasted_iota(jnp.int32, (rows, flat), 1)
    own = (c & (MX_HEADS - 1)) == _div_pow2(r, n_tok)
    s = jnp.where(own, _nt_dot(qq, k_ref[...].reshape(flat, MX_DH).astype(BF16)), NEG_INF)
    e = jnp.exp(s - jnp.max(s, axis=-1, keepdims=True))
    o = jnp.dot(e.astype(BF16), v_ref[...].reshape(flat, MX_DH).astype(BF16), preferred_element_type=F32)
    o = o / jnp.sum(e, axis=-1, keepdims=True)
    for h in range(MX_HEADS):
        o_ref[:, h * MX_DH:(h + 1) * MX_DH] = o[h * n_tok:(h + 1) * n_tok].astype(o_ref.dtype)


def _xattn_sample(z, mem_k, mem_v, N, n_tok, q_sec, seq_off):
    n_mem = mem_k.shape[1]
    width = MX_HEADS * MX_DH
    assert (MX_HEADS * n_tok) % BF16_SUBLANES == 0
    mem_spec = pl.BlockSpec((None, n_mem, MX_HEADS, MX_DH), lambda n: (seq_off + n, 0, 0, 0))
    return pl.pallas_call(
        functools.partial(_xattn_sample_kernel, n_tok=n_tok),
        grid=(N,),
        in_specs=[pl.BlockSpec((n_tok, width), lambda n: (n, q_sec)), mem_spec, mem_spec],
        out_specs=pl.BlockSpec((n_tok, width), lambda n: (n, 0)),
        out_shape=jax.ShapeDtypeStruct((N * n_tok, width), F32),
        compiler_params=_cparams("parallel"),
        name="xattn_sample",
    )(z, mem_k, mem_v)


def _post_kernel(x_ref, da_ref, sg_ref, mx_ref, gate_ref, wb_ref, wo_ref, gf_ref, wu_ref, wd_ref, gfin_ref,
                 o_ref, y_ref, *, d, d_ff, fc, final_norm):
    branches = (da_ref, sg_ref, mx_ref)
    merged = None
    for k in range(N_BRANCH):
        br = jnp.dot(branches[k][...].astype(BF16), wb_ref[k * d:(k + 1) * d, :], preferred_element_type=F32)
        term = gate_ref[:, k * d:(k + 1) * d].astype(F32) * br
        merged = term if merged is None else merged + term
    x1 = x_ref[...] + jnp.dot(merged.astype(BF16), wo_ref[...], preferred_element_type=F32)
    h2 = _rms(x1, gf_ref[...]).astype(BF16)
    y_ref[...] = x1
    for c in range(d_ff // fc):
        a = jnp.dot(h2, wu_ref[:, c * fc:(c + 1) * fc], preferred_element_type=F32)
        b = jnp.dot(h2, wu_ref[:, d_ff + c * fc:d_ff + (c + 1) * fc], preferred_element_type=F32)
        act = (jax.nn.silu(a) * b).astype(BF16)
        y_ref[...] += jnp.dot(act, wd_ref[c * fc:(c + 1) * fc, :], preferred_element_type=F32)
    y = y_ref[...]
    o_ref[...] = _rms(y, gfin_ref[...]) if final_norm else y


def _post(x, o_da, o_sg, o_mx, z, gate_sec, wb, wo, g_ffn, wu, wd, g_final, tm, final_norm):
    T, d = x.shape
    d_ff = wd.shape[0]
    fc = 256
    row = lambda i: (i, 0)
    whole = lambda i: (0, 0)

    def resident(shape):
        return pl.BlockSpec(shape, whole, pipeline_mode=pl.Buffered(1))

    return pl.pallas_call(
        functools.partial(_post_kernel, d=d, d_ff=d_ff, fc=fc, final_norm=final_norm),
        grid=(T // tm,),
        in_specs=[
            pl.BlockSpec((tm, d), row), pl.BlockSpec((tm, d), row), pl.BlockSpec((tm, d), row), pl.BlockSpec((tm, d), row),
            pl.BlockSpec((tm, N_BRANCH * d), lambda i: (i, gate_sec)),
            resident(wb.shape), resident(wo.shape), resident((1, d)), resident(wu.shape), resident(wd.shape),
            resident((1, d)),
        ],
        out_specs=pl.BlockSpec((tm, d), row),
        out_shape=jax.ShapeDtypeStruct((T, d), F32),
        scratch_shapes=[pltpu.VMEM((tm, d), F32)],
        compiler_params=_cparams("parallel"),
        name="post",
    )(x, o_da, o_sg, o_mx, z, wb, wo, g_ffn, wu, wd, g_final)


_IN_SECTIONS = ("q", "k", "v", "u", "s", "m", "g0", "g1", "g2")
_IN_ACTS = {
    "q": ("scale", DA_DK ** -0.5 * LOG2E), "k": ("none",), "v": ("none",), "u": ("gelu",), "s": ("gelu_rms",),
    "m": ("scale", MX_DH ** -0.5), "g0": ("sigmoid",), "g1": ("sigmoid",), "g2": ("sigmoid",),
}
_IN_COPIES = {"k": 0, "v": 1}


def kernel(x_prompt, x_sample, mem_prompt, cache_da_k, cache_da_v, cache_mem_k, cache_mem_v, page_table, g_attn, w_in,
           da_lam, da_subln, rel_bias, sg_norm, sg_w, sg_b, g_mem, w_mem_kv, w_branch, w_out, g_ffn, w_up, w_down,
           g_final):
    B, S, D = x_prompt.shape
    N, n_tok, _ = x_sample.shape
    depth, n_pool, page = cache_da_k.shape[:3]
    n_mem = mem_prompt.shape[1]
    n_pages = page_table.shape[1]
    width = DA_HEADS * DA_DV
    assert D == width == SG_GROUPS * CHUNK == MX_HEADS * MX_DH and page == CHUNK and n_tok <= CHUNK

    t_attn = min(512, S)
    pps = 4 if n_pages % 4 == 0 else 1
    tm_p = 512
    tm_s = min(512, N * n_tok)
    tm_post = 256
    sec = {n: i for i, n in enumerate(_IN_SECTIONS)}
    in_sections = [(_IN_ACTS[n], _IN_COPIES.get(n)) for n in _IN_SECTIONS]

    xp = x_prompt.reshape(B * S, D)
    xs = x_sample.reshape(N * n_tok, D)
    mem = mem_prompt.reshape(B * n_mem, D)
    ck = cache_da_k.reshape(depth * n_pool, page, DA_HEADS, DA_DV)
    cv = cache_da_v.reshape(depth * n_pool, page, DA_HEADS, DA_DV)
    cmk = cache_mem_k.reshape(depth * N, n_mem, MX_HEADS, MX_DH)
    cmv = cache_mem_v.reshape(depth * N, n_mem, MX_HEADS, MX_DH)
    row = lambda a: a.reshape(1, -1)

    bias_diag, bias_sub = _prompt_bias(rel_bias, t_attn)
    seg_s = n_tok
    outs = {k: [] for k in ("dkp", "dvp", "dks", "dvs", "mkp", "mvp", "sgs")}
    for l in range(depth):
        lam_init = 0.8 - 0.6 * math.exp(-0.3 * l)
        w_in_l = w_in[l].astype(BF16)
        wb_l, wo_l = w_branch[l].astype(BF16), w_out[l].astype(BF16)
        wu_l, wd_l = w_up[l].astype(BF16), w_down[l].astype(BF16)
        far, last, new, lam = _sample_bias(rel_bias, da_lam[l], n_tok, page, lam_init)
        subln = row(da_subln[l])
        sgw_p = sg_w[l]
        sgb_p = jnp.repeat(sg_b[l].T, CHUNK, axis=1)
        sgw_s = jnp.tile(sg_w[l][:, :seg_s, :seg_s], (1, CHUNK // seg_s, CHUNK // seg_s))
        sgb_s = jnp.repeat(jnp.tile(sg_b[l][:, :seg_s].T, (CHUNK // seg_s, 1)), CHUNK, axis=1)

        mkv16, mk32, mv32 = _norm_matmul(mem, row(g_mem[l]), w_mem_kv[l].astype(BF16), row(sg_norm[l]),
                                         [(("none",), 0), (("none",), 1)], BF16, min(512, B * n_mem), "mem_kv",
                                         copy_heads=MX_HEADS)
        zp, kp32, vp32 = _norm_matmul(xp, row(g_attn[l]), w_in_l, row(sg_norm[l]), in_sections, BF16, tm_p,
                                      "inproj_prompt", copy_heads=DA_HEADS)
        oda_p = _attn_prompt(zp, rel_bias, lam, bias_diag, bias_sub, subln, B, S, t_attn, lam_init)
        osg_p = _sgate(zp, sgw_p, sgb_p, CHUNK, sec["u"], sec["s"], 4)
        omx_p = _xattn_prompt(zp, mkv16, B, S, 512, sec["m"])
        xp = _post(xp, oda_p, osg_p, omx_p, zp, sec["g0"] // N_BRANCH, wb_l, wo_l, row(g_ffn[l]), wu_l, wd_l,
                   row(g_final), tm_post, l == depth - 1)

        zs, ks32, vs32 = _norm_matmul(xs, row(g_attn[l]), w_in_l, row(sg_norm[l]), in_sections, F32, tm_s,
                                      "inproj_sample", copy_heads=DA_HEADS)
        oda_s = _attn_sample(zs, ks32, vs32, page_table + l * n_pool, ck, cv, lam, far, last, new, subln, N, n_tok,
                             lam_init, pps)
        osg_s = _sgate(zs, sgw_s, sgb_s, seg_s, sec["u"], sec["s"], 1)
        omx_s = _xattn_sample(zs, cmk, cmv, N, n_tok, sec["m"], l * N)
        xs = _post(xs, oda_s, osg_s, omx_s, zs, sec["g0"] // N_BRANCH, wb_l, wo_l, row(g_ffn[l]), wu_l, wd_l,
                   row(g_final), min(tm_post, N * n_tok), l == depth - 1)

        outs["dkp"].append(kp32.reshape(B, S, DA_HEADS, 2 * DA_DK))
        outs["dvp"].append(vp32.reshape(B, S, DA_HEADS, DA_DV))
        outs["dks"].append(ks32.reshape(N, n_tok, DA_HEADS, 2 * DA_DK))
        outs["dvs"].append(vs32.reshape(N, n_tok, DA_HEADS, DA_DV))
        outs["mkp"].append(mk32.reshape(B, n_mem, MX_HEADS, MX_DH))
        outs["mvp"].append(mv32.reshape(B, n_mem, MX_HEADS, MX_DH))
        outs["sgs"].append(zs[:, sec["s"] * D:(sec["s"] + 1) * D].reshape(N, n_tok, D))

    return (xp.reshape(B, S, D), xs.reshape(N, n_tok, D), jnp.stack(outs["dkp"]), jnp.stack(outs["dvp"]),
            jnp.stack(outs["dks"]), jnp.stack(outs["dvs"]), jnp.stack(outs["mkp"]), jnp.stack(outs["mvp"]),
            jnp.stack(outs["sgs"]))
```

```python
import functools
import math

import jax
import jax.numpy as jnp
from jax import lax
from jax.experimental import pallas as pl
from jax.experimental.pallas import tpu as pltpu

F32 = jnp.float32
BF16 = jnp.bfloat16

DA_HEADS = 8
DA_DK = 64
DA_DV = 2 * DA_DK
SG_GROUPS = 8
CHUNK = 128
MX_HEADS = 4
MX_DH = 256
N_BUCKETS = 32
MAX_DISTANCE = 128
N_BRANCH = 3
RMS_EPS = 1e-6
NEG_INF = -1e30
LOG2E = math.log2(math.e)

LANES = 128
BF16_SUBLANES = 16
VMEM_LIMIT_BYTES = 56 * 1024 * 1024


def _cparams(*sem):
    return pltpu.CompilerParams(dimension_semantics=sem, vmem_limit_bytes=VMEM_LIMIT_BYTES)


def _rms(x, g):
    ms = jnp.mean(x * x, axis=-1, keepdims=True)
    return x * lax.rsqrt(ms + RMS_EPS) * g


def _div_pow2(x, n):
    assert n > 0 and n & (n - 1) == 0, n
    return x >> (n.bit_length() - 1)


def _nt_dot(a, b):
    return lax.dot_general(a, b, (((1,), (1,)), ((), ())), preferred_element_type=F32)


def _lane_tile(x, n):
    return x if n == 1 else jnp.concatenate([x] * n, axis=1)


def _norm_matmul_kernel(x_ref, g_ref, w_ref, sgn_ref, z_ref, *copy_refs, sections, tn):
    hn = _rms(x_ref[...], g_ref[...]).astype(BF16)
    for sec, (act, copy_idx) in enumerate(sections):
        cols = slice(sec * tn, (sec + 1) * tn)
        acc = jnp.dot(hn, w_ref[:, cols], preferred_element_type=F32)
        if act[0] == "scale":
            val = acc * act[1]
        elif act[0] == "gelu":
            val = jax.nn.gelu(acc)
        elif act[0] == "gelu_rms":
            val = _rms(jax.nn.gelu(acc), sgn_ref[...])
        elif act[0] == "sigmoid":
            val = jax.nn.sigmoid(acc)
        else:
            val = acc
        z_ref[:, cols] = val.astype(z_ref.dtype)
        if copy_idx is not None:
            cref = copy_refs[copy_idx]
            if len(cref.shape) == 2:
                cref[...] = val
            else:
                hd = cref.shape[2]
                for hh in range(cref.shape[1]):
                    cref[:, hh, :] = val[:, hh * hd:(hh + 1) * hd]


def _norm_matmul(x, g, w, sgn, sections, z_dtype, tm, name, copy_heads=1):
    T, D = x.shape
    n_sec = len(sections)
    tn = w.shape[1] // n_sec
    n_copy = sum(1 for _, c in sections if c is not None)
    out_shape = [jax.ShapeDtypeStruct((T, n_sec * tn), z_dtype)]
    out_specs = [pl.BlockSpec((tm, n_sec * tn), lambda i: (i, 0))]
    for _ in range(n_copy):
        if copy_heads == 1:
            out_shape.append(jax.ShapeDtypeStruct((T, tn), F32))
            out_specs.append(pl.BlockSpec((tm, tn), lambda i: (i, 0)))
        else:
            out_shape.append(jax.ShapeDtypeStruct((T, copy_heads, tn // copy_heads), F32))
            out_specs.append(pl.BlockSpec((tm, copy_heads, tn // copy_heads), lambda i: (i, 0, 0)))

    def resident(shape):
        return pl.BlockSpec(shape, lambda i: (0, 0), pipeline_mode=pl.Buffered(1))

    return pl.pallas_call(
        functools.partial(_norm_matmul_kernel, sections=tuple(sections), tn=tn),
        grid=(T // tm,),
        in_specs=[pl.BlockSpec((tm, D), lambda i: (i, 0)), resident((1, D)), resident(w.shape), resident((1, tn))],
        out_specs=out_specs,
        out_shape=out_shape,
        compiler_params=_cparams("parallel"),
        name=name,
    )(x, g, w, sgn)


def _t5_bias(dist, rb_ref, h):
    n = jnp.maximum(dist, 0)
    max_exact = N_BUCKETS // 2
    nf = jnp.maximum(n, 1).astype(F32)
    large = max_exact + (jnp.log(nf / max_exact) / math.log(MAX_DISTANCE / max_exact)
                         * (N_BUCKETS - max_exact)).astype(jnp.int32)
    large = jnp.minimum(large, N_BUCKETS - 1)
    bucket = jnp.where(n < max_exact, n, large)
    out = jnp.zeros(dist.shape, F32)
    for b in range(N_BUCKETS):
        out = jnp.where(bucket == b, rb_ref[b, h], out)
    return jnp.where(dist >= 0, out * LOG2E, NEG_INF)


def _far_bias(rb_ref, h):
    return rb_ref[N_BUCKETS - 1, h] * LOG2E


def _prompt_bias_kernel(rb_ref, diag_ref, sub_ref, *, t):
    h = pl.program_id(0)
    d = lax.broadcasted_iota(jnp.int32, (CHUNK, CHUNK), 0) - lax.broadcasted_iota(jnp.int32, (CHUNK, CHUNK), 1)
    near = (_t5_bias(d, rb_ref, h), _t5_bias(d + CHUNK, rb_ref, h))
    far = jnp.full((CHUNK, CHUNK), _far_bias(rb_ref, h), F32)
    masked = jnp.full((CHUNK, CHUNK), NEG_INF, F32)

    def block(bd):
        return masked if bd < 0 else near[bd] if bd < 2 else far

    nb = t // CHUNK
    for bi in range(nb):
        for bj in range(nb):
            sl = (slice(bi * CHUNK, (bi + 1) * CHUNK), slice(bj * CHUNK, (bj + 1) * CHUNK))
            diag_ref[sl] = block(bi - bj)
            sub_ref[sl] = block(nb + bi - bj)


def _prompt_bias(rel_bias, t):
    assert MAX_DISTANCE <= CHUNK + 1 and t % CHUNK == 0
    H = rel_bias.shape[1]
    return pl.pallas_call(
        functools.partial(_prompt_bias_kernel, t=t),
        grid=(H,),
        in_specs=[pl.BlockSpec(memory_space=pltpu.SMEM)],
        out_specs=[pl.BlockSpec((None, t, t), lambda h: (h, 0, 0))] * 2,
        out_shape=[jax.ShapeDtypeStruct((H, t, t), F32)] * 2,
        compiler_params=_cparams("arbitrary"),
        name="prompt_bias",
    )(rel_bias)


def _sample_bias_kernel(rb_ref, lam_p_ref, far_ref, last_ref, new_ref, lam_ref, *, n_tok, page, lam_init):
    hr = 2 * n_tok

    def grid(cols):
        tok = lax.broadcasted_iota(jnp.int32, (hr, cols), 0) & (n_tok - 1)
        col = lax.broadcasted_iota(jnp.int32, (hr, cols), 1)
        return tok, col, _div_pow2(col, DA_HEADS), col & (DA_HEADS - 1)

    tok, _, key, kh = grid(page * DA_HEADS)
    tok_n, col_n, key_n, kh_n = grid(page)
    for h in range(DA_HEADS):
        sl = slice(h * hr, (h + 1) * hr)
        far_ref[sl, :] = jnp.where(kh == h, _far_bias(rb_ref, h), NEG_INF)
        last_ref[sl, :] = jnp.where(kh == h, _t5_bias(page + tok - key, rb_ref, h), NEG_INF)
        new_ref[sl, :] = jnp.where((kh_n == h) & (col_n < n_tok * DA_HEADS), _t5_bias(tok_n - key_n, rb_ref, h), NEG_INF)
    lp = lam_p_ref[...]
    s1 = jnp.sum(lp[0:1, :] * lp[1:2, :], axis=-1, keepdims=True)
    s2 = jnp.sum(lp[2:3, :] * lp[3:4, :], axis=-1, keepdims=True)
    lam = jnp.exp(s1) - jnp.exp(s2) + lam_init
    lam_ref[...] = jnp.broadcast_to(lam, lam_ref.shape)


def _sample_bias(rel_bias, da_lam_l, n_tok, page, lam_init):
    assert n_tok & (n_tok - 1) == 0 and n_tok * DA_HEADS <= page
    rows = DA_HEADS * 2 * n_tok
    wide = jax.ShapeDtypeStruct((rows, page * DA_HEADS), F32)
    return pl.pallas_call(
        functools.partial(_sample_bias_kernel, n_tok=n_tok, page=page, lam_init=lam_init),
        in_specs=[pl.BlockSpec(memory_space=pltpu.SMEM), pl.BlockSpec(memory_space=pltpu.VMEM)],
        out_shape=[wide, wide, jax.ShapeDtypeStruct((rows, page), F32), jax.ShapeDtypeStruct((8, LANES), F32)],
        name="sample_bias",
    )(rel_bias, da_lam_l)


def _split_components(q):
    lane = lax.broadcasted_iota(jnp.int32, q.shape, 1)
    zero = jnp.zeros_like(q)
    return jnp.concatenate([jnp.where(lane < DA_DK, q, zero), jnp.where(lane >= DA_DK, q, zero)], axis=0)


def _diff_out(o1, o2, lam_row, subln, lam_init):
    o = o1 - lam_row * o2
    return _rms(o, subln) * (1.0 - lam_init)


def _softmax_update(m, acc, s_list, v_list):
    m_new = m
    for s in s_list:
        m_new = jnp.maximum(m_new, jnp.max(s, axis=-1, keepdims=True))
    acc = acc * _lane_tile(jnp.exp2(m - m_new), 2)
    for s, v in zip(s_list, v_list):
        e = jnp.exp2(s - _lane_tile(m_new, s.shape[1] // LANES)).astype(BF16)
        v1 = jnp.concatenate([v, jnp.ones(v.shape, BF16)], axis=1)
        acc = acc + jnp.dot(e, v1, preferred_element_type=F32)
    return m_new, acc


def _attn_prompt_kernel(rb_ref, lam_ref, q_ref, k_ref, v_ref, bd_ref, bs_ref, sub_ref, o_ref, *, t, nq, lam_init):
    far = _far_bias(rb_ref, pl.program_id(1))
    for qi in range(nq):
        rows = slice(qi * t, (qi + 1) * t)
        qq = _split_components(q_ref[rows, :])
        m = jnp.full((2 * t, LANES), -jnp.inf, F32)
        acc = jnp.zeros((2 * t, 2 * LANES), F32)
        for kt in range(qi + 1):
            keys = slice(kt * t, (kt + 1) * t)
            s = _nt_dot(qq, k_ref[keys, :])
            if kt < qi - 1:
                s = s + far
            else:
                b = bd_ref[...] if kt == qi else bs_ref[...]
                s = jnp.concatenate([s[:t] + b, s[t:] + b], axis=0)
            m, acc = _softmax_update(m, acc, [s], [v_ref[keys, :]])
        o = acc[:, :DA_DV] / acc[:, DA_DV:]
        out = _diff_out(o[:t], o[t:], lam_ref[0:1, :], sub_ref[...], lam_init)
        o_ref[rows, :] = out.astype(o_ref.dtype)


def _attn_prompt(z, rel_bias, lam, bias_diag, bias_sub, subln, B, S, t, lam_init):
    assert t >= MAX_DISTANCE and S % t == 0
    H = DA_HEADS
    return pl.pallas_call(
        functools.partial(_attn_prompt_kernel, t=t, nq=S // t, lam_init=lam_init),
        grid=(B, H),
        in_specs=[
            pl.BlockSpec(memory_space=pltpu.SMEM),
            pl.BlockSpec((8, LANES), lambda b, h: (0, 0)),
            pl.BlockSpec((S, DA_DV), lambda b, h: (b, h)),
            pl.BlockSpec((S, DA_DV), lambda b, h: (b, H + h)),
            pl.BlockSpec((S, DA_DV), lambda b, h: (b, 2 * H + h)),
            pl.BlockSpec((None, t, t), lambda b, h: (h, 0, 0)),
            pl.BlockSpec((None, t, t), lambda b, h: (h, 0, 0)),
            pl.BlockSpec((1, DA_DV), lambda b, h: (0, 0)),
        ],
        out_specs=pl.BlockSpec((S, DA_DV), lambda b, h: (b, h)),
        out_shape=jax.ShapeDtypeStruct((B * S, H * DA_DV), BF16),
        compiler_params=_cparams("parallel", "arbitrary"),
        name="attn_prompt",
    )(rel_bias, lam, z, z, z, bias_diag, bias_sub, subln)


def _attn_sample_kernel(pt_ref, lam_ref, q_ref, kn_ref, vn_ref, far_ref, last_ref, new_ref, sub_ref, *rest,
                        spb, pps, n_tok, page, lam_init):
    n_in = spb * pps
    kp_refs, vp_refs = rest[:n_in], rest[n_in:2 * n_in]
    o_ref, qq_ref, m_ref, acc_ref = rest[2 * n_in:]
    p = pl.program_id(1)
    last = pl.num_programs(1) - 1
    hr = 2 * n_tok
    flat = page * DA_HEADS

    @pl.when(p == 0)
    def _():
        for j in range(spb):
            q = q_ref[j * n_tok:(j + 1) * n_tok, :]
            for h in range(DA_HEADS):
                qq_ref[j, h * hr:(h + 1) * hr, :] = _split_components(q[:, h * DA_DV:(h + 1) * DA_DV]).astype(BF16)
        m_ref[...] = jnp.full(m_ref.shape, -jnp.inf, F32)
        acc_ref[...] = jnp.zeros(acc_ref.shape, F32)

    far = far_ref[...]
    tail = jnp.where(p == last, last_ref[...], far)
    for j in range(spb):
        qq = qq_ref[j]
        s_list, v_list = [], []
        for r in range(pps):
            s = _nt_dot(qq, kp_refs[j * pps + r][...].reshape(flat, DA_DV).astype(BF16))
            s_list.append(s + (tail if r == pps - 1 else far))
            v_list.append(vp_refs[j * pps + r][...].reshape(flat, DA_DV).astype(BF16))
        m, acc = _softmax_update(m_ref[j], acc_ref[j], s_list, v_list)
        m_ref[j] = m
        acc_ref[j] = acc

    @pl.when(p == last)
    def _():
        pad = jnp.zeros((page - n_tok * DA_HEADS, DA_DV), F32)
        for j in range(spb):
            toks = slice(j * n_tok, (j + 1) * n_tok)

            def new_rows(ref):
                return jnp.concatenate([ref[toks].reshape(n_tok * DA_HEADS, DA_DV), pad], axis=0).astype(BF16)

            s_new = _nt_dot(qq_ref[j], new_rows(kn_ref)) + new_ref[...]
            _, acc2 = _softmax_update(m_ref[j], acc_ref[j], [s_new], [new_rows(vn_ref)])
            o = acc2[:, :DA_DV] / acc2[:, DA_DV:]
            for h in range(DA_HEADS):
                oh = o[h * hr:(h + 1) * hr]
                out = _diff_out(oh[:n_tok], oh[n_tok:], lam_ref[0:1, :], sub_ref[...], lam_init)
                o_ref[toks, h * DA_DV:(h + 1) * DA_DV] = out.astype(o_ref.dtype)


def _attn_sample(z, k_new, v_new, page_idx, cache_k, cache_v, lam, far, last, new, subln, N, n_tok, lam_init, spb, pps):
    n_pages = page_idx.shape[1]
    page = cache_k.shape[1]
    width = DA_HEADS * DA_DV
    rows = DA_HEADS * 2 * n_tok
    assert (2 * n_tok) % BF16_SUBLANES == 0 and n_pages % pps == 0 and N % spb == 0

    def const(shape):
        return pl.BlockSpec(shape, lambda n, p, pt: (0,) * len(shape))

    def page_spec(j, r):
        return pl.BlockSpec((None, page, DA_HEADS, DA_DV), lambda n, p, pt: (pt[n * spb + j, p * pps + r], 0, 0, 0))

    page_specs = [page_spec(j, r) for j in range(spb) for r in range(pps)]
    new_spec = pl.BlockSpec((spb * n_tok, DA_HEADS, DA_DV), lambda n, p, pt: (n, 0, 0))
    grid_spec = pltpu.PrefetchScalarGridSpec(
        num_scalar_prefetch=1,
        grid=(N // spb, n_pages // pps),
        in_specs=[
            const((8, LANES)),
            pl.BlockSpec((spb * n_tok, width), lambda n, p, pt: (n, 0)),
            new_spec, new_spec,
            const((rows, page * DA_HEADS)), const((rows, page * DA_HEADS)), const((rows, page)),
            const((1, DA_DV)),
        ] + page_specs + page_specs,
        out_specs=pl.BlockSpec((spb * n_tok, width), lambda n, p, pt: (n, 0)),
        scratch_shapes=[pltpu.VMEM((spb, rows, DA_DV), BF16), pltpu.VMEM((spb, rows, LANES), F32),
                        pltpu.VMEM((spb, rows, 2 * LANES), F32)],
    )
    n_in = spb * pps
    return pl.pallas_call(
        functools.partial(_attn_sample_kernel, spb=spb, pps=pps, n_tok=n_tok, page=page, lam_init=lam_init),
        grid_spec=grid_spec,
        out_shape=jax.ShapeDtypeStruct((N * n_tok, width), F32),
        compiler_params=_cparams("parallel", "arbitrary"),
        name="attn_sample",
    )(page_idx, lam, z, k_new, v_new, far, last, new, subln, *([cache_k] * n_in), *([cache_v] * n_in))


def _sgate_kernel(u_ref, vn_ref, w_ref, b_ref, o_ref, *, seg, n_chunk):
    i = lax.broadcasted_iota(jnp.int32, (CHUNK, CHUNK), 0)
    j = lax.broadcasted_iota(jnp.int32, (CHUNK, CHUNK), 1)
    keep = (_div_pow2(i, seg) == _div_pow2(j, seg)) & (j <= i)
    gd = u_ref.shape[1] // SG_GROUPS
    for g in range(SG_GROUPS):
        w = jnp.where(keep, w_ref[g], 0.0).astype(BF16)
        cols = slice(g * gd, (g + 1) * gd)
        for c in range(n_chunk):
            rws = slice(c * CHUNK, (c + 1) * CHUNK)
            mix = jnp.dot(w, vn_ref[rws, cols].astype(BF16), preferred_element_type=F32) + b_ref[:, cols]
            o_ref[rws, cols] = (u_ref[rws, cols].astype(F32) * mix).astype(o_ref.dtype)


def _sgate(z, w, b, seg, u_sec, vn_sec, n_chunk):
    T = z.shape[0]
    width = b.shape[1]
    rb = n_chunk * CHUNK
    return pl.pallas_call(
        functools.partial(_sgate_kernel, seg=seg, n_chunk=n_chunk),
        grid=(T // rb,),
        in_specs=[
            pl.BlockSpec((rb, width), lambda i: (i, u_sec)),
            pl.BlockSpec((rb, width), lambda i: (i, vn_sec)),
            pl.BlockSpec((SG_GROUPS, CHUNK, CHUNK), lambda i: (0, 0, 0)),
            pl.BlockSpec((CHUNK, width), lambda i: (0, 0)),
        ],
        out_specs=pl.BlockSpec((rb, width), lambda i: (i, 0)),
        out_shape=jax.ShapeDtypeStruct((T, width), BF16),
        compiler_params=_cparams("parallel"),
        name="sgate",
    )(z, z, w, b)


def _xattn_prompt_kernel(q_ref, k_ref, v_ref, o_ref):
    for h in range(MX_HEADS):
        cols = slice(h * MX_DH, (h + 1) * MX_DH)
        s = _nt_dot(q_ref[:, cols], k_ref[:, cols])
        e = jnp.exp(s - jnp.max(s, axis=-1, keepdims=True))
        o = jnp.dot(e.astype(BF16), v_ref[:, cols], preferred_element_type=F32)
        o_ref[:, cols] = (o / jnp.sum(e, axis=-1, keepdims=True)).astype(o_ref.dtype)


def _xattn_prompt(z, mem_kv, B, S, tq, q_sec):
    nq = S // tq
    n_mem = mem_kv.shape[0] // B
    width = MX_HEADS * MX_DH
    return pl.pallas_call(
        _xattn_prompt_kernel,
        grid=(B, nq),
        in_specs=[
            pl.BlockSpec((tq, width), lambda b, i: (b * nq + i, q_sec)),
            pl.BlockSpec((n_mem, width), lambda b, i: (b, 0)),
            pl.BlockSpec((n_mem, width), lambda b, i: (b, 1)),
        ],
        out_specs=pl.BlockSpec((tq, width), lambda b, i: (b * nq + i, 0)),
        out_shape=jax.ShapeDtypeStruct((B * S, width), BF16),
        compiler_params=_cparams("parallel", "arbitrary"),
        name="xattn_prompt",
    )(z, mem_kv, mem_kv)


def _xattn_sample_kernel(q_ref, k_ref, v_ref, o_ref, *, spb, n_tok):
    n_mem = k_ref.shape[1]
    rows, flat = MX_HEADS * n_tok, n_mem * MX_HEADS
    r = lax.broadcasted_iota(jnp.int32, (rows, flat), 0)
    c = lax.broadcasted_iota(jnp.int32, (rows, flat), 1)
    own = (c & (MX_HEADS - 1)) == _div_pow2(r, n_tok)
    for j in range(spb):
        toks = slice(j * n_tok, (j + 1) * n_tok)
        q = q_ref[toks, :]
        qq = jnp.concatenate([q[:, h * MX_DH:(h + 1) * MX_DH] for h in range(MX_HEADS)], axis=0).astype(BF16)
        s = jnp.where(own, _nt_dot(qq, k_ref[j].reshape(flat, MX_DH).astype(BF16)), NEG_INF)
        e = jnp.exp(s - jnp.max(s, axis=-1, keepdims=True))
        o = jnp.dot(e.astype(BF16), v_ref[j].reshape(flat, MX_DH).astype(BF16), preferred_element_type=F32)
        o = o / jnp.sum(e, axis=-1, keepdims=True)
        for h in range(MX_HEADS):
            o_ref[toks, h * MX_DH:(h + 1) * MX_DH] = o[h * n_tok:(h + 1) * n_tok].astype(o_ref.dtype)


def _xattn_sample(z, mem_k, mem_v, N, n_tok, q_sec, seq_off, spb):
    n_mem = mem_k.shape[1]
    width = MX_HEADS * MX_DH
    assert (MX_HEADS * n_tok) % BF16_SUBLANES == 0 and N % spb == 0 and seq_off % spb == 0
    mem_spec = pl.BlockSpec((spb, n_mem, MX_HEADS, MX_DH), lambda n: (seq_off // spb + n, 0, 0, 0))
    return pl.pallas_call(
        functools.partial(_xattn_sample_kernel, spb=spb, n_tok=n_tok),
        grid=(N // spb,),
        in_specs=[pl.BlockSpec((spb * n_tok, width), lambda n: (n, q_sec)), mem_spec, mem_spec],
        out_specs=pl.BlockSpec((spb * n_tok, width), lambda n: (n, 0)),
        out_shape=jax.ShapeDtypeStruct((N * n_tok, width), F32),
        compiler_params=_cparams("parallel"),
        name="xattn_sample",
    )(z, mem_k, mem_v)


def _post_kernel(x_ref, da_ref, sg_ref, mx_ref, gate_ref, wb_ref, wo_ref, gf_ref, wu_ref, wd_ref, gfin_ref,
                 o_ref, y_ref, *, d, d_ff, fc, n_sub, final_norm):
    branches = (da_ref, sg_ref, mx_ref)
    rs = x_ref.shape[0] // n_sub
    subs = [slice(sb * rs, (sb + 1) * rs) for sb in range(n_sub)]
    merged = [None] * n_sub
    for k in range(N_BRANCH):
        for i, rws in enumerate(subs):
            br = jnp.dot(branches[k][rws, :].astype(BF16), wb_ref[k * d:(k + 1) * d, :], preferred_element_type=F32)
            term = gate_ref[rws, k * d:(k + 1) * d].astype(F32) * br
            merged[i] = term if merged[i] is None else merged[i] + term
    h2 = []
    for i, rws in enumerate(subs):
        x1 = x_ref[rws, :] + jnp.dot(merged[i].astype(BF16), wo_ref[...], preferred_element_type=F32)
        y_ref[rws, :] = x1
        h2.append(_rms(x1, gf_ref[...]).astype(BF16))
    for c in range(d_ff // fc):
        up = [(jnp.dot(h, wu_ref[:, c * fc:(c + 1) * fc], preferred_element_type=F32),
               jnp.dot(h, wu_ref[:, d_ff + c * fc:d_ff + (c + 1) * fc], preferred_element_type=F32)) for h in h2]
        for (a, b), rws in zip(up, subs):
            act = (jax.nn.silu(a) * b).astype(BF16)
            y_ref[rws, :] += jnp.dot(act, wd_ref[c * fc:(c + 1) * fc, :], preferred_element_type=F32)
    for rws in subs:
        y = y_ref[rws, :]
        o_ref[rws, :] = _rms(y, gfin_ref[...]) if final_norm else y


def _post(x, o_da, o_sg, o_mx, z, gate_sec, wb, wo, g_ffn, wu, wd, g_final, tm, final_norm):
    T, d = x.shape
    d_ff = wd.shape[0]
    fc = 256
    n_sub = 2 if tm >= 512 else 1
    row = lambda i: (i, 0)
    whole = lambda i: (0, 0)

    def resident(shape):
        return pl.BlockSpec(shape, whole, pipeline_mode=pl.Buffered(1))

    return pl.pallas_call(
        functools.partial(_post_kernel, d=d, d_ff=d_ff, fc=fc, n_sub=n_sub, final_norm=final_norm),
        grid=(T // tm,),
        in_specs=[
            pl.BlockSpec((tm, d), row), pl.BlockSpec((tm, d), row), pl.BlockSpec((tm, d), row), pl.BlockSpec((tm, d), row),
            pl.BlockSpec((tm, N_BRANCH * d), lambda i: (i, gate_sec)),
            resident(wb.shape), resident(wo.shape), resident((1, d)), resident(wu.shape), resident(wd.shape),
            resident((1, d)),
        ],
        out_specs=pl.BlockSpec((tm, d), row),
        out_shape=jax.ShapeDtypeStruct((T, d), F32),
        scratch_shapes=[pltpu.VMEM((tm, d), F32)],
        compiler_params=_cparams("parallel"),
        name="post",
    )(x, o_da, o_sg, o_mx, z, wb, wo, g_ffn, wu, wd, g_final)


_IN_SECTIONS = ("q", "k", "v", "u", "s", "m", "g0", "g1", "g2")
_IN_ACTS = {
    "q": ("scale", DA_DK ** -0.5 * LOG2E), "k": ("none",), "v": ("none",), "u": ("gelu",), "s": ("gelu_rms",),
    "m": ("scale", MX_DH ** -0.5), "g0": ("sigmoid",), "g1": ("sigmoid",), "g2": ("sigmoid",),
}
_IN_COPIES = {"k": 0, "v": 1}


def kernel(x_prompt, x_sample, mem_prompt, cache_da_k, cache_da_v, cache_mem_k, cache_mem_v, page_table, g_attn, w_in,
           da_lam, da_subln, rel_bias, sg_norm, sg_w, sg_b, g_mem, w_mem_kv, w_branch, w_out, g_ffn, w_up, w_down,
           g_final):
    B, S, D = x_prompt.shape
    N, n_tok, _ = x_sample.shape
    depth, n_pool, page = cache_da_k.shape[:3]
    n_mem = mem_prompt.shape[1]
    n_pages = page_table.shape[1]
    width = DA_HEADS * DA_DV
    assert D == width == SG_GROUPS * CHUNK == MX_HEADS * MX_DH and page == CHUNK and n_tok <= CHUNK

    t_attn = min(512, S)
    pps = 4 if n_pages % 4 == 0 else 1
    spb_da = 2 if N % 2 == 0 else 1
    spb_mx = 4 if N % 4 == 0 else 1
    tm_p = 256
    tm_s = min(256, N * n_tok)
    tm_post = 512
    sec = {n: i for i, n in enumerate(_IN_SECTIONS)}
    in_sections = [(_IN_ACTS[n], _IN_COPIES.get(n)) for n in _IN_SECTIONS]

    xp = x_prompt.reshape(B * S, D)
    xs = x_sample.reshape(N * n_tok, D)
    mem = mem_prompt.reshape(B * n_mem, D)
    ck = cache_da_k.reshape(depth * n_pool, page, DA_HEADS, DA_DV)
    cv = cache_da_v.reshape(depth * n_pool, page, DA_HEADS, DA_DV)
    cmk = cache_mem_k.reshape(depth * N, n_mem, MX_HEADS, MX_DH)
    cmv = cache_mem_v.reshape(depth * N, n_mem, MX_HEADS, MX_DH)
    row = lambda a: a.reshape(1, -1)

    bias_diag, bias_sub = _prompt_bias(rel_bias, t_attn)
    seg_s = n_tok
    outs = {k: [] for k in ("dkp", "dvp", "dks", "dvs", "mkp", "mvp", "sgs")}
    for l in range(depth):
        lam_init = 0.8 - 0.6 * math.exp(-0.3 * l)
        w_in_l = w_in[l].astype(BF16)
        wb_l, wo_l = w_branch[l].astype(BF16), w_out[l].astype(BF16)
        wu_l, wd_l = w_up[l].astype(BF16), w_down[l].astype(BF16)
        far, last, new, lam = _sample_bias(rel_bias, da_lam[l], n_tok, page, lam_init)
        subln = row(da_subln[l])
        sgw_p = sg_w[l]
        sgb_p = jnp.repeat(sg_b[l].T, CHUNK, axis=1)
        sgw_s = jnp.tile(sg_w[l][:, :seg_s, :seg_s], (1, CHUNK // seg_s, CHUNK // seg_s))
        sgb_s = jnp.repeat(jnp.tile(sg_b[l][:, :seg_s].T, (CHUNK // seg_s, 1)), CHUNK, axis=1)

        mkv16, mk32, mv32 = _norm_matmul(mem, row(g_mem[l]), w_mem_kv[l].astype(BF16), row(sg_norm[l]),
                                         [(("none",), 0), (("none",), 1)], BF16, min(512, B * n_mem), "mem_kv",
                                         copy_heads=MX_HEADS)
        zp, kp32, vp32 = _norm_matmul(xp, row(g_attn[l]), w_in_l, row(sg_norm[l]), in_sections, BF16, tm_p,
                                      "inproj_prompt")
        oda_p = _attn_prompt(zp, rel_bias, lam, bias_diag, bias_sub, subln, B, S, t_attn, lam_init)
        osg_p = _sgate(zp, sgw_p, sgb_p, CHUNK, sec["u"], sec["s"], 4)
        omx_p = _xattn_prompt(zp, mkv16, B, S, 512, sec["m"])
        xp = _post(xp, oda_p, osg_p, omx_p, zp, sec["g0"] // N_BRANCH, wb_l, wo_l, row(g_ffn[l]), wu_l, wd_l,
                   row(g_final), tm_post, l == depth - 1)

        zs, ks32, vs32 = _norm_matmul(xs, row(g_attn[l]), w_in_l, row(sg_norm[l]), in_sections, F32, tm_s,
                                      "inproj_sample", copy_heads=DA_HEADS)
        oda_s = _attn_sample(zs, ks32, vs32, page_table + l * n_pool, ck, cv, lam, far, last, new, subln, N, n_tok,
                             lam_init, spb_da, pps)
        osg_s = _sgate(zs, sgw_s, sgb_s, seg_s, sec["u"], sec["s"], 1)
        omx_s = _xattn_sample(zs, cmk, cmv, N, n_tok, sec["m"], l * N, spb_mx)
        xs = _post(xs, oda_s, osg_s, omx_s, zs, sec["g0"] // N_BRANCH, wb_l, wo_l, row(g_ffn[l]), wu_l, wd_l,
                   row(g_final), tm_s, l == depth - 1)

        outs["dkp"].append(kp32.reshape(B, S, DA_HEADS, 2 * DA_DK))
        outs["dvp"].append(vp32.reshape(B, S, DA_HEADS, DA_DV))
        outs["dks"].append(ks32.reshape(N, n_tok, DA_HEADS, 2 * DA_DK))
        outs["dvs"].append(vs32.reshape(N, n_tok, DA_HEADS, DA_DV))
        outs["mkp"].append(mk32.reshape(B, n_mem, MX_HEADS, MX_DH))
        outs["mvp"].append(mv32.reshape(B, n_mem, MX_HEADS, MX_DH))
        outs["sgs"].append(zs[:, sec["s"] * D:(sec["s"] + 1) * D].reshape(N, n_tok, D))

    return (xp.reshape(B, S, D), xs.reshape(N, n_tok, D), jnp.stack(outs["dkp"]), jnp.stack(outs["dvp"]),
            jnp.stack(outs["dks"]), jnp.stack(outs["dvs"]), jnp.stack(outs["mkp"]), jnp.stack(outs["mvp"]),
            jnp.stack(outs["sgs"]))
```

```python
import functools
import math

import jax
import jax.numpy as jnp
from jax import lax
from jax.experimental import pallas as pl
from jax.experimental.pallas import tpu as pltpu

F32 = jnp.float32
BF16 = jnp.bfloat16

DA_HEADS = 8
DA_DK = 64
DA_DV = 2 * DA_DK
SG_GROUPS = 8
CHUNK = 128
MX_HEADS = 4
MX_DH = 256
N_BUCKETS = 32
MAX_DISTANCE = 128
N_BRANCH = 3
RMS_EPS = 1e-6
NEG_INF = -1e30
LOG2E = math.log2(math.e)

LANES = 128
BF16_SUBLANES = 16
VMEM_LIMIT_BYTES = 56 * 1024 * 1024


def _cparams(*sem):
    return pltpu.CompilerParams(dimension_semantics=sem, vmem_limit_bytes=VMEM_LIMIT_BYTES)


def _rms(x, g):
    ms = jnp.mean(x * x, axis=-1, keepdims=True)
    return x * lax.rsqrt(ms + RMS_EPS) * g


def _div_pow2(x, n):
    assert n > 0 and n & (n - 1) == 0, n
    return x >> (n.bit_length() - 1)


def _nt_dot(a, b):
    return lax.dot_general(a, b, (((1,), (1,)), ((), ())), preferred_element_type=F32)


def _lane_tile(x, n):
    return x if n == 1 else jnp.concatenate([x] * n, axis=1)


def _norm_matmul_kernel(x_ref, g_ref, w_ref, sgn_ref, z_ref, *copy_refs, sections, tn):
    hn = _rms(x_ref[...], g_ref[...]).astype(BF16)
    for sec, (act, copy_idx) in enumerate(sections):
        cols = slice(sec * tn, (sec + 1) * tn)
        acc = jnp.dot(hn, w_ref[:, cols], preferred_element_type=F32)
        if act[0] == "scale":
            val = acc * act[1]
        elif act[0] == "gelu":
            val = jax.nn.gelu(acc)
        elif act[0] == "gelu_rms":
            val = _rms(jax.nn.gelu(acc), sgn_ref[...])
        elif act[0] == "sigmoid":
            val = jax.nn.sigmoid(acc)
        else:
            val = acc
        z_ref[:, cols] = val.astype(z_ref.dtype)
        if copy_idx is not None:
            cref = copy_refs[copy_idx]
            if len(cref.shape) == 2:
                cref[...] = val
            else:
                hd = cref.shape[2]
                for hh in range(cref.shape[1]):
                    cref[:, hh, :] = val[:, hh * hd:(hh + 1) * hd]


def _norm_matmul(x, g, w, sgn, sections, z_dtype, tm, name, copy_heads=1):
    T, D = x.shape
    n_sec = len(sections)
    tn = w.shape[1] // n_sec
    n_copy = sum(1 for _, c in sections if c is not None)
    out_shape = [jax.ShapeDtypeStruct((T, n_sec * tn), z_dtype)]
    out_specs = [pl.BlockSpec((tm, n_sec * tn), lambda i: (i, 0))]
    for _ in range(n_copy):
        if copy_heads == 1:
            out_shape.append(jax.ShapeDtypeStruct((T, tn), F32))
            out_specs.append(pl.BlockSpec((tm, tn), lambda i: (i, 0)))
        else:
            out_shape.append(jax.ShapeDtypeStruct((T, copy_heads, tn // copy_heads), F32))
            out_specs.append(pl.BlockSpec((tm, copy_heads, tn // copy_heads), lambda i: (i, 0, 0)))

    def resident(shape):
        return pl.BlockSpec(shape, lambda i: (0, 0), pipeline_mode=pl.Buffered(1))

    return pl.pallas_call(
        functools.partial(_norm_matmul_kernel, sections=tuple(sections), tn=tn),
        grid=(T // tm,),
        in_specs=[pl.BlockSpec((tm, D), lambda i: (i, 0)), resident((1, D)), resident(w.shape), resident((1, tn))],
        out_specs=out_specs,
        out_shape=out_shape,
        compiler_params=_cparams("parallel"),
        name=name,
    )(x, g, w, sgn)


def _t5_bias(dist, rb_ref, h):
    n = jnp.maximum(dist, 0)
    max_exact = N_BUCKETS // 2
    nf = jnp.maximum(n, 1).astype(F32)
    rel = jnp.log(nf / max_exact) / math.log(MAX_DISTANCE / max_exact) * (N_BUCKETS - max_exact)
    out = jnp.zeros(dist.shape, F32)
    for b in range(max_exact):
        out = jnp.where(n == b, rb_ref[b, h], out)
    out = jnp.where(n >= max_exact, rb_ref[max_exact, h], out)
    for b in range(max_exact + 1, N_BUCKETS):
        out = jnp.where(rel >= b - max_exact, rb_ref[b, h], out)
    return jnp.where(dist >= 0, out * LOG2E, NEG_INF)


def _far_bias(rb_ref, h):
    return rb_ref[N_BUCKETS - 1, h] * LOG2E


def _prompt_bias_kernel(rb_ref, near_ref):
    d = lax.broadcasted_iota(jnp.int32, (CHUNK, CHUNK), 0) - lax.broadcasted_iota(jnp.int32, (CHUNK, CHUNK), 1)
    for h in range(near_ref.shape[0]):
        near_ref[h, 0] = _t5_bias(d, rb_ref, h)
        near_ref[h, 1] = _t5_bias(d + CHUNK, rb_ref, h)


def _prompt_bias(rel_bias):
    assert MAX_DISTANCE <= CHUNK + 1
    H = rel_bias.shape[1]
    return pl.pallas_call(
        _prompt_bias_kernel,
        in_specs=[pl.BlockSpec(memory_space=pltpu.SMEM)],
        out_shape=jax.ShapeDtypeStruct((H, 2, CHUNK, CHUNK), F32),
        name="prompt_bias",
    )(rel_bias)


def _tile_bias(near_ref, far, t, base):
    fill = {-1: jnp.full((CHUNK, CHUNK), NEG_INF, F32), 2: jnp.full((CHUNK, CHUNK), far, F32)}
    nb = t // CHUNK

    def block(bd):
        return near_ref[bd] if 0 <= bd < 2 else fill[max(min(bd, 2), -1)]

    return jnp.concatenate(
        [jnp.concatenate([block(base + bi - bj) for bj in range(nb)], axis=1) for bi in range(nb)], axis=0)


def _sample_bias_kernel(rb_ref, lam_p_ref, far_ref, last_ref, new_ref, lam_ref, *, n_tok, page, lam_init):
    hr = 2 * n_tok

    def grid(cols):
        tok = lax.broadcasted_iota(jnp.int32, (hr, cols), 0) & (n_tok - 1)
        col = lax.broadcasted_iota(jnp.int32, (hr, cols), 1)
        return tok, col, _div_pow2(col, DA_HEADS), col & (DA_HEADS - 1)

    tok, _, key, kh = grid(page * DA_HEADS)
    tok_n, col_n, key_n, kh_n = grid(page)
    for h in range(DA_HEADS):
        sl = slice(h * hr, (h + 1) * hr)
        far_ref[sl, :] = jnp.where(kh == h, _far_bias(rb_ref, h), NEG_INF)
        last_ref[sl, :] = jnp.where(kh == h, _t5_bias(page + tok - key, rb_ref, h), NEG_INF)
        new_ref[sl, :] = jnp.where((kh_n == h) & (col_n < n_tok * DA_HEADS), _t5_bias(tok_n - key_n, rb_ref, h), NEG_INF)
    lp = lam_p_ref[...]
    s1 = jnp.sum(lp[0:1, :] * lp[1:2, :], axis=-1, keepdims=True)
    s2 = jnp.sum(lp[2:3, :] * lp[3:4, :], axis=-1, keepdims=True)
    lam = jnp.exp(s1) - jnp.exp(s2) + lam_init
    lam_ref[...] = jnp.broadcast_to(lam, lam_ref.shape)


def _sample_bias(rel_bias, da_lam_l, n_tok, page, lam_init):
    assert n_tok & (n_tok - 1) == 0 and n_tok * DA_HEADS <= page
    rows = DA_HEADS * 2 * n_tok
    wide = jax.ShapeDtypeStruct((rows, page * DA_HEADS), F32)
    return pl.pallas_call(
        functools.partial(_sample_bias_kernel, n_tok=n_tok, page=page, lam_init=lam_init),
        in_specs=[pl.BlockSpec(memory_space=pltpu.SMEM), pl.BlockSpec(memory_space=pltpu.VMEM)],
        out_shape=[wide, wide, jax.ShapeDtypeStruct((rows, page), F32), jax.ShapeDtypeStruct((8, LANES), F32)],
        name="sample_bias",
    )(rel_bias, da_lam_l)


def _split_components(q):
    lane = lax.broadcasted_iota(jnp.int32, q.shape, 1)
    zero = jnp.zeros_like(q)
    return jnp.concatenate([jnp.where(lane < DA_DK, q, zero), jnp.where(lane >= DA_DK, q, zero)], axis=0)


def _diff_out(o1, o2, lam_row, subln, lam_init):
    o = o1 - lam_row * o2
    return _rms(o, subln) * (1.0 - lam_init)


def _softmax_update(m, acc, s_list, v_list, mxu_sums):
    m_new = m
    for s in s_list:
        m_new = jnp.maximum(m_new, jnp.max(s, axis=-1, keepdims=True))
    acc = acc * _lane_tile(jnp.exp2(m - m_new), 2)
    for s, v in zip(s_list, v_list):
        e = jnp.exp2(s - _lane_tile(m_new, s.shape[1] // LANES))
        if mxu_sums:
            v1 = jnp.concatenate([v, jnp.ones(v.shape, BF16)], axis=1)
            acc = acc + jnp.dot(e.astype(BF16), v1, preferred_element_type=F32)
        else:
            pv = jnp.dot(e.astype(BF16), v, preferred_element_type=F32)
            l = jnp.broadcast_to(jnp.sum(e, axis=-1, keepdims=True), pv.shape)
            acc = acc + jnp.concatenate([pv, l], axis=1)
    return m_new, acc


def _attn_prompt_kernel(rb_ref, lam_ref, q_ref, k_ref, v_ref, near_ref, sub_ref, o_ref, *, t, nq, lam_init):
    far = _far_bias(rb_ref, pl.program_id(1))
    bias_diag = _tile_bias(near_ref, far, t, 0)
    bias_sub = _tile_bias(near_ref, far, t, t // CHUNK)
    for qi in range(nq):
        rows = slice(qi * t, (qi + 1) * t)
        qq = _split_components(q_ref[rows, :])
        m = jnp.full((2 * t, LANES), -jnp.inf, F32)
        acc = jnp.zeros((2 * t, 2 * LANES), F32)
        for kt in range(qi + 1):
            keys = slice(kt * t, (kt + 1) * t)
            s = _nt_dot(qq, k_ref[keys, :])
            if kt < qi - 1:
                s = s + far
            else:
                b = bias_diag if kt == qi else bias_sub
                s = jnp.concatenate([s[:t] + b, s[t:] + b], axis=0)
            m, acc = _softmax_update(m, acc, [s], [v_ref[keys, :]], True)
        o = acc[:, :DA_DV] / acc[:, DA_DV:]
        out = _diff_out(o[:t], o[t:], lam_ref[0:1, :], sub_ref[...], lam_init)
        o_ref[rows, :] = out.astype(o_ref.dtype)


def _attn_prompt(z, rel_bias, lam, near, subln, B, S, t, lam_init):
    assert t >= MAX_DISTANCE and S % t == 0 and t % CHUNK == 0
    H = DA_HEADS
    return pl.pallas_call(
        functools.partial(_attn_prompt_kernel, t=t, nq=S // t, lam_init=lam_init),
        grid=(B, H),
        in_specs=[
            pl.BlockSpec(memory_space=pltpu.SMEM),
            pl.BlockSpec((8, LANES), lambda b, h: (0, 0)),
            pl.BlockSpec((S, DA_DV), lambda b, h: (b, h)),
            pl.BlockSpec((S, DA_DV), lambda b, h: (b, H + h)),
            pl.BlockSpec((S, DA_DV), lambda b, h: (b, 2 * H + h)),
            pl.BlockSpec((None, 2, CHUNK, CHUNK), lambda b, h: (h, 0, 0, 0)),
            pl.BlockSpec((1, DA_DV), lambda b, h: (0, 0)),
        ],
        out_specs=pl.BlockSpec((S, DA_DV), lambda b, h: (b, h)),
        out_shape=jax.ShapeDtypeStruct((B * S, H * DA_DV), BF16),
        compiler_params=_cparams("parallel", "arbitrary"),
        name="attn_prompt",
    )(rel_bias, lam, z, z, z, near, subln)


def _attn_sample_kernel(pt_ref, lam_ref, q_ref, kn_ref, vn_ref, far_ref, last_ref, new_ref, sub_ref, *rest,
                        spb, pps, n_tok, page, lam_init):
    n_in = spb * pps
    kp_refs, vp_refs = rest[:n_in], rest[n_in:2 * n_in]
    o_ref, qq_ref, m_ref, acc_ref = rest[2 * n_in:]
    p = pl.program_id(1)
    last = pl.num_programs(1) - 1
    hr = 2 * n_tok
    flat = page * DA_HEADS

    @pl.when(p == 0)
    def _():
        for j in range(spb):
            q = q_ref[j * n_tok:(j + 1) * n_tok, :]
            for h in range(DA_HEADS):
                qq_ref[j, h * hr:(h + 1) * hr, :] = _split_components(q[:, h * DA_DV:(h + 1) * DA_DV]).astype(BF16)
        m_ref[...] = jnp.full(m_ref.shape, -jnp.inf, F32)
        acc_ref[...] = jnp.zeros(acc_ref.shape, F32)

    far = far_ref[...]
    tail = jnp.where(p == last, last_ref[...], far)
    for j in range(spb):
        qq = qq_ref[j]
        s_list, v_list = [], []
        for r in range(pps):
            s = _nt_dot(qq, kp_refs[j * pps + r][...].reshape(flat, DA_DV).astype(BF16))
            s_list.append(s + (tail if r == pps - 1 else far))
            v_list.append(vp_refs[j * pps + r][...].reshape(flat, DA_DV).astype(BF16))
        m, acc = _softmax_update(m_ref[j], acc_ref[j], s_list, v_list, False)
        m_ref[j] = m
        acc_ref[j] = acc

    @pl.when(p == last)
    def _():
        pad = jnp.zeros((page - n_tok * DA_HEADS, DA_DV), F32)
        for j in range(spb):
            toks = slice(j * n_tok, (j + 1) * n_tok)

            def new_rows(ref):
                return jnp.concatenate([ref[toks].reshape(n_tok * DA_HEADS, DA_DV), pad], axis=0).astype(BF16)

            s_new = _nt_dot(qq_ref[j], new_rows(kn_ref)) + new_ref[...]
            _, acc2 = _softmax_update(m_ref[j], acc_ref[j], [s_new], [new_rows(vn_ref)], False)
            o = acc2[:, :DA_DV] / acc2[:, DA_DV:]
            for h in range(DA_HEADS):
                oh = o[h * hr:(h + 1) * hr]
                out = _diff_out(oh[:n_tok], oh[n_tok:], lam_ref[0:1, :], sub_ref[...], lam_init)
                o_ref[toks, h * DA_DV:(h + 1) * DA_DV] = out.astype(o_ref.dtype)


def _attn_sample(z, k_new, v_new, page_idx, cache_k, cache_v, lam, far, last, new, subln, N, n_tok, lam_init, spb, pps):
    n_pages = page_idx.shape[1]
    page = cache_k.shape[1]
    width = DA_HEADS * DA_DV
    rows = DA_HEADS * 2 * n_tok
    assert (2 * n_tok) % BF16_SUBLANES == 0 and n_pages % pps == 0 and N % spb == 0

    def const(shape):
        return pl.BlockSpec(shape, lambda n, p, pt: (0,) * len(shape))

    def page_spec(j, r):
        return pl.BlockSpec((None, page, DA_HEADS, DA_DV), lambda n, p, pt: (pt[n * spb + j, p * pps + r], 0, 0, 0))

    page_specs = [page_spec(j, r) for j in range(spb) for r in range(pps)]
    new_spec = pl.BlockSpec((spb * n_tok, DA_HEADS, DA_DV), lambda n, p, pt: (n, 0, 0))
    grid_spec = pltpu.PrefetchScalarGridSpec(
        num_scalar_prefetch=1,
        grid=(N // spb, n_pages // pps),
        in_specs=[
            const((8, LANES)),
            pl.BlockSpec((spb * n_tok, width), lambda n, p, pt: (n, 0)),
            new_spec, new_spec,
            const((rows, page * DA_HEADS)), const((rows, page * DA_HEADS)), const((rows, page)),
            const((1, DA_DV)),
        ] + page_specs + page_specs,
        out_specs=pl.BlockSpec((spb * n_tok, width), lambda n, p, pt: (n, 0)),
        scratch_shapes=[pltpu.VMEM((spb, rows, DA_DV), BF16), pltpu.VMEM((spb, rows, LANES), F32),
                        pltpu.VMEM((spb, rows, 2 * LANES), F32)],
    )
    n_in = spb * pps
    return pl.pallas_call(
        functools.partial(_attn_sample_kernel, spb=spb, pps=pps, n_tok=n_tok, page=page, lam_init=lam_init),
        grid_spec=grid_spec,
        out_shape=jax.ShapeDtypeStruct((N * n_tok, width), F32),
        compiler_params=_cparams("parallel", "arbitrary"),
        name="attn_sample",
    )(page_idx, lam, z, k_new, v_new, far, last, new, subln, *([cache_k] * n_in), *([cache_v] * n_in))


def _sgate_kernel(u_ref, vn_ref, w_ref, b_ref, o_ref, *, seg, n_chunk):
    i = lax.broadcasted_iota(jnp.int32, (CHUNK, CHUNK), 0)
    j = lax.broadcasted_iota(jnp.int32, (CHUNK, CHUNK), 1)
    keep = (_div_pow2(i, seg) == _div_pow2(j, seg)) & (j <= i)
    gd = u_ref.shape[1] // SG_GROUPS
    for g in range(SG_GROUPS):
        w = jnp.where(keep, w_ref[g], 0.0).astype(BF16)
        cols = slice(g * gd, (g + 1) * gd)
        for c in range(n_chunk):
            rws = slice(c * CHUNK, (c + 1) * CHUNK)
            mix = jnp.dot(w, vn_ref[rws, cols].astype(BF16), preferred_element_type=F32) + b_ref[:, cols]
            o_ref[rws, cols] = (u_ref[rws, cols].astype(F32) * mix).astype(o_ref.dtype)


def _sgate(z, w, b, u_sec, vn_sec, n_chunk):
    T = z.shape[0]
    width = b.shape[1]
    rb = n_chunk * CHUNK
    return pl.pallas_call(
        functools.partial(_sgate_kernel, seg=CHUNK, n_chunk=n_chunk),
        grid=(T // rb,),
        in_specs=[
            pl.BlockSpec((rb, width), lambda i: (i, u_sec)),
            pl.BlockSpec((rb, width), lambda i: (i, vn_sec)),
            pl.BlockSpec((SG_GROUPS, CHUNK, CHUNK), lambda i: (0, 0, 0)),
            pl.BlockSpec((CHUNK, width), lambda i: (0, 0)),
        ],
        out_specs=pl.BlockSpec((rb, width), lambda i: (i, 0)),
        out_shape=jax.ShapeDtypeStruct((T, width), BF16),
        compiler_params=_cparams("parallel"),
        name="sgate",
    )(z, z, w, b)


def _sgate_seq_kernel(w_ref, b_ref, u_ref, vn_ref, o_ref, wbd_ref, bias_ref, *, seg):
    gd = u_ref.shape[1] // SG_GROUPS

    @pl.when(pl.program_id(0) == 0)
    def _():
        i = lax.broadcasted_iota(jnp.int32, (CHUNK, CHUNK), 0)
        j = lax.broadcasted_iota(jnp.int32, (CHUNK, CHUNK), 1)
        same = _div_pow2(i, seg) == _div_pow2(j, seg)
        ti, tj = i & (seg - 1), j & (seg - 1)
        for g in range(SG_GROUPS):
            w = jnp.zeros((CHUNK, CHUNK), F32)
            bias = jnp.zeros((CHUNK, gd), F32)
            for a in range(seg):
                bias = jnp.where(ti == a, b_ref[g, a], bias)
                for b in range(a + 1):
                    w = jnp.where((ti == a) & (tj == b), w_ref[g * seg + a, b], w)
            wbd_ref[g] = jnp.where(same, w, 0.0).astype(BF16)
            bias_ref[:, g * gd:(g + 1) * gd] = bias

    for g in range(SG_GROUPS):
        cols = slice(g * gd, (g + 1) * gd)
        mix = jnp.dot(wbd_ref[g], vn_ref[:, cols].astype(BF16), preferred_element_type=F32) + bias_ref[:, cols]
        o_ref[:, cols] = (u_ref[:, cols].astype(F32) * mix).astype(o_ref.dtype)


def _sgate_seq(z, w, b, seg, u_sec, vn_sec):
    T = z.shape[0]
    width = SG_GROUPS * CHUNK
    assert CHUNK % seg == 0 and seg & (seg - 1) == 0 and T % CHUNK == 0
    return pl.pallas_call(
        functools.partial(_sgate_seq_kernel, seg=seg),
        grid=(T // CHUNK,),
        in_specs=[
            pl.BlockSpec(memory_space=pltpu.SMEM), pl.BlockSpec(memory_space=pltpu.SMEM),
            pl.BlockSpec((CHUNK, width), lambda i: (i, u_sec)),
            pl.BlockSpec((CHUNK, width), lambda i: (i, vn_sec)),
        ],
        out_specs=pl.BlockSpec((CHUNK, width), lambda i: (i, 0)),
        out_shape=jax.ShapeDtypeStruct((T, width), BF16),
        scratch_shapes=[pltpu.VMEM((SG_GROUPS, CHUNK, CHUNK), BF16), pltpu.VMEM((CHUNK, width), F32)],
        compiler_params=_cparams("arbitrary"),
        name="sgate_seq",
    )(w[:, :seg, :seg].reshape(SG_GROUPS * seg, seg), b[:, :seg], z, z)


def _xattn_prompt_kernel(q_ref, k_ref, v_ref, o_ref):
    heads = [slice(h * MX_DH, (h + 1) * MX_DH) for h in range(MX_HEADS)]
    scores = [_nt_dot(q_ref[:, cols], k_ref[:, cols]) for cols in heads]
    exps = [jnp.exp(s - jnp.max(s, axis=-1, keepdims=True)) for s in scores]
    for cols, e in zip(heads, exps):
        o = jnp.dot(e.astype(BF16), v_ref[:, cols], preferred_element_type=F32)
        o_ref[:, cols] = (o / jnp.sum(e, axis=-1, keepdims=True)).astype(o_ref.dtype)


def _xattn_prompt(z, mem_kv, B, S, tq, q_sec):
    nq = S // tq
    n_mem = mem_kv.shape[0] // B
    width = MX_HEADS * MX_DH
    return pl.pallas_call(
        _xattn_prompt_kernel,
        grid=(B, nq),
        in_specs=[
            pl.BlockSpec((tq, width), lambda b, i: (b * nq + i, q_sec)),
            pl.BlockSpec((n_mem, width), lambda b, i: (b, 0)),
            pl.BlockSpec((n_mem, width), lambda b, i: (b, 1)),
        ],
        out_specs=pl.BlockSpec((tq, width), lambda b, i: (b * nq + i, 0)),
        out_shape=jax.ShapeDtypeStruct((B * S, width), BF16),
        compiler_params=_cparams("parallel", "arbitrary"),
        name="xattn_prompt",
    )(z, mem_kv, mem_kv)


def _xattn_sample_kernel(q_ref, k_ref, v_ref, o_ref, *, spb, n_tok):
    n_mem = k_ref.shape[1]
    rows, flat = MX_HEADS * n_tok, n_mem * MX_HEADS
    r = lax.broadcasted_iota(jnp.int32, (rows, flat), 0)
    c = lax.broadcasted_iota(jnp.int32, (rows, flat), 1)
    own = (c & (MX_HEADS - 1)) == _div_pow2(r, n_tok)
    for j in range(spb):
        toks = slice(j * n_tok, (j + 1) * n_tok)
        q = q_ref[toks, :]
        qq = jnp.concatenate([q[:, h * MX_DH:(h + 1) * MX_DH] for h in range(MX_HEADS)], axis=0).astype(BF16)
        s = jnp.where(own, _nt_dot(qq, k_ref[j].reshape(flat, MX_DH).astype(BF16)), NEG_INF)
        e = jnp.exp(s - jnp.max(s, axis=-1, keepdims=True))
        o = jnp.dot(e.astype(BF16), v_ref[j].reshape(flat, MX_DH).astype(BF16), preferred_element_type=F32)
        o = o / jnp.sum(e, axis=-1, keepdims=True)
        for h in range(MX_HEADS):
            o_ref[toks, h * MX_DH:(h + 1) * MX_DH] = o[h * n_tok:(h + 1) * n_tok].astype(o_ref.dtype)


def _xattn_sample(z, mem_k, mem_v, N, n_tok, q_sec, seq_off, spb):
    n_mem = mem_k.shape[1]
    width = MX_HEADS * MX_DH
    assert (MX_HEADS * n_tok) % BF16_SUBLANES == 0 and N % spb == 0 and seq_off % spb == 0
    mem_spec = pl.BlockSpec((spb, n_mem, MX_HEADS, MX_DH), lambda n: (seq_off // spb + n, 0, 0, 0))
    return pl.pallas_call(
        functools.partial(_xattn_sample_kernel, spb=spb, n_tok=n_tok),
        grid=(N // spb,),
        in_specs=[pl.BlockSpec((spb * n_tok, width), lambda n: (n, q_sec)), mem_spec, mem_spec],
        out_specs=pl.BlockSpec((spb * n_tok, width), lambda n: (n, 0)),
        out_shape=jax.ShapeDtypeStruct((N * n_tok, width), F32),
        compiler_params=_cparams("parallel"),
        name="xattn_sample",
    )(z, mem_k, mem_v)


def _post_kernel(x_ref, da_ref, sg_ref, mx_ref, gate_ref, wb_ref, wo_ref, gf_ref, wu_ref, wd_ref, gfin_ref,
                 o_ref, y_ref, *, d, d_ff, fc, n_sub, final_norm):
    branches = (da_ref, sg_ref, mx_ref)
    rs = x_ref.shape[0] // n_sub
    subs = [slice(sb * rs, (sb + 1) * rs) for sb in range(n_sub)]
    merged = [None] * n_sub
    for k in range(N_BRANCH):
        for i, rws in enumerate(subs):
            br = jnp.dot(branches[k][rws, :].astype(BF16), wb_ref[k * d:(k + 1) * d, :], preferred_element_type=F32)
            term = gate_ref[rws, k * d:(k + 1) * d].astype(F32) * br
            merged[i] = term if merged[i] is None else merged[i] + term
    h2 = []
    for i, rws in enumerate(subs):
        x1 = x_ref[rws, :] + jnp.dot(merged[i].astype(BF16), wo_ref[...], preferred_element_type=F32)
        y_ref[rws, :] = x1
        h2.append(_rms(x1, gf_ref[...]).astype(BF16))
    for c in range(d_ff // fc):
        up = [(jnp.dot(h, wu_ref[:, c * fc:(c + 1) * fc], preferred_element_type=F32),
               jnp.dot(h, wu_ref[:, d_ff + c * fc:d_ff + (c + 1) * fc], preferred_element_type=F32)) for h in h2]
        for (a, b), rws in zip(up, subs):
            act = (jax.nn.silu(a) * b).astype(BF16)
            y_ref[rws, :] += jnp.dot(act, wd_ref[c * fc:(c + 1) * fc, :], preferred_element_type=F32)
    for rws in subs:
        y = y_ref[rws, :]
        o_ref[rws, :] = _rms(y, gfin_ref[...]) if final_norm else y


def _post(x, o_da, o_sg, o_mx, z, gate_sec, wb, wo, g_ffn, wu, wd, g_final, tm, final_norm):
    T, d = x.shape
    d_ff = wd.shape[0]
    fc = 256
    n_sub = 2 if tm >= 512 else 1
    row = lambda i: (i, 0)
    whole = lambda i: (0, 0)

    def resident(shape):
        return pl.BlockSpec(shape, whole, pipeline_mode=pl.Buffered(1))

    return pl.pallas_call(
        functools.partial(_post_kernel, d=d, d_ff=d_ff, fc=fc, n_sub=n_sub, final_norm=final_norm),
        grid=(T // tm,),
        in_specs=[
            pl.BlockSpec((tm, d), row), pl.BlockSpec((tm, d), row), pl.BlockSpec((tm, d), row), pl.BlockSpec((tm, d), row),
            pl.BlockSpec((tm, N_BRANCH * d), lambda i: (i, gate_sec)),
            resident(wb.shape), resident(wo.shape), resident((1, d)), resident(wu.shape), resident(wd.shape),
            resident((1, d)),
        ],
        out_specs=pl.BlockSpec((tm, d), row),
        out_shape=jax.ShapeDtypeStruct((T, d), F32),
        scratch_shapes=[pltpu.VMEM((tm, d), F32)],
        compiler_params=_cparams("parallel"),
        name="post",
    )(x, o_da, o_sg, o_mx, z, wb, wo, g_ffn, wu, wd, g_final)


_IN_SECTIONS = ("q", "k", "v", "u", "s", "m", "g0", "g1", "g2")
_IN_ACTS = {
    "q": ("scale", DA_DK ** -0.5 * LOG2E), "k": ("none",), "v": ("none",), "u": ("gelu",), "s": ("gelu_rms",),
    "m": ("scale", MX_DH ** -0.5), "g0": ("sigmoid",), "g1": ("sigmoid",), "g2": ("sigmoid",),
}
_IN_COPIES = {"k": 0, "v": 1}


def kernel(x_prompt, x_sample, mem_prompt, cache_da_k, cache_da_v, cache_mem_k, cache_mem_v, page_table, g_attn, w_in,
           da_lam, da_subln, rel_bias, sg_norm, sg_w, sg_b, g_mem, w_mem_kv, w_branch, w_out, g_ffn, w_up, w_down,
           g_final):
    B, S, D = x_prompt.shape
    N, n_tok, _ = x_sample.shape
    depth, n_pool, page = cache_da_k.shape[:3]
    n_mem = mem_prompt.shape[1]
    n_pages = page_table.shape[1]
    width = DA_HEADS * DA_DV
    assert D == width == SG_GROUPS * CHUNK == MX_HEADS * MX_DH and page == CHUNK and n_tok <= CHUNK

    t_attn = min(512, S)
    pps = 4 if n_pages % 4 == 0 else 1
    spb_da = 2 if N % 2 == 0 else 1
    spb_mx = 4 if N % 4 == 0 else 1
    tm_p = 256
    tm_s = min(256, N * n_tok)
    tm_post = 512
    sec = {n: i for i, n in enumerate(_IN_SECTIONS)}
    in_sections = [(_IN_ACTS[n], _IN_COPIES.get(n)) for n in _IN_SECTIONS]

    xp = x_prompt.reshape(B * S, D)
    xs = x_sample.reshape(N * n_tok, D)
    mem = mem_prompt.reshape(B * n_mem, D)
    ck = cache_da_k.reshape(depth * n_pool, page, DA_HEADS, DA_DV)
    cv = cache_da_v.reshape(depth * n_pool, page, DA_HEADS, DA_DV)
    cmk = cache_mem_k.reshape(depth * N, n_mem, MX_HEADS, MX_DH)
    cmv = cache_mem_v.reshape(depth * N, n_mem, MX_HEADS, MX_DH)
    row = lambda a: a.reshape(1, -1)

    near_bias = _prompt_bias(rel_bias)
    seg_s = n_tok
    outs = {k: [] for k in ("dkp", "dvp", "dks", "dvs", "mkp", "mvp", "sgs")}
    for l in range(depth):
        lam_init = 0.8 - 0.6 * math.exp(-0.3 * l)
        w_in_l = w_in[l].astype(BF16)
        wb_l, wo_l = w_branch[l].astype(BF16), w_out[l].astype(BF16)
        wu_l, wd_l = w_up[l].astype(BF16), w_down[l].astype(BF16)
        far, last, new, lam = _sample_bias(rel_bias, da_lam[l], n_tok, page, lam_init)
        subln = row(da_subln[l])
        sgb_p = jnp.repeat(sg_b[l].T, CHUNK, axis=1)

        mkv16, mk32, mv32 = _norm_matmul(mem, row(g_mem[l]), w_mem_kv[l].astype(BF16), row(sg_norm[l]),
                                         [(("none",), 0), (("none",), 1)], BF16, min(512, B * n_mem), "mem_kv",
                                         copy_heads=MX_HEADS)
        zp, kp32, vp32 = _norm_matmul(xp, row(g_attn[l]), w_in_l, row(sg_norm[l]), in_sections, BF16, tm_p,
                                      "inproj_prompt")
        oda_p = _attn_prompt(zp, rel_bias, lam, near_bias, subln, B, S, t_attn, lam_init)
        osg_p = _sgate(zp, sg_w[l], sgb_p, sec["u"], sec["s"], 8)
        omx_p = _xattn_prompt(zp, mkv16, B, S, 512, sec["m"])
        xp = _post(xp, oda_p, osg_p, omx_p, zp, sec["g0"] // N_BRANCH, wb_l, wo_l, row(g_ffn[l]), wu_l, wd_l,
                   row(g_final), tm_post, l == depth - 1)

        zs, ks32, vs32 = _norm_matmul(xs, row(g_attn[l]), w_in_l, row(sg_norm[l]), in_sections, F32, tm_s,
                                      "inproj_sample", copy_heads=DA_HEADS)
        oda_s = _attn_sample(zs, ks32, vs32, page_table + l * n_pool, ck, cv, lam, far, last, new, subln, N, n_tok,
                             lam_init, spb_da, pps)
        osg_s = _sgate_seq(zs, sg_w[l], sg_b[l], seg_s, sec["u"], sec["s"])
        omx_s = _xattn_sample(zs, cmk, cmv, N, n_tok, sec["m"], l * N, spb_mx)
        xs = _post(xs, oda_s, osg_s, omx_s, zs, sec["g0"] // N_BRANCH, wb_l, wo_l, row(g_ffn[l]), wu_l, wd_l,
                   row(g_final), tm_s, l == depth - 1)

        outs["dkp"].append(kp32.reshape(B, S, DA_HEADS, 2 * DA_DK))
        outs["dvp"].append(vp32.reshape(B, S, DA_HEADS, DA_DV))
        outs["dks"].append(ks32.reshape(N, n_tok, DA_HEADS, 2 * DA_DK))
        outs["dvs"].append(vs32.reshape(N, n_tok, DA_HEADS, DA_DV))
        outs["mkp"].append(mk32.reshape(B, n_mem, MX_HEADS, MX_DH))
        outs["mvp"].append(mv32.reshape(B, n_mem, MX_HEADS, MX_DH))
        outs["sgs"].append(zs[:, sec["s"] * D:(sec["s"] + 1) * D].reshape(N, n_tok, D))

    return (xp.reshape(B, S, D), xs.reshape(N, n_tok, D), jnp.stack(outs["dkp"]), jnp.stack(outs["dvp"]),
            jnp.stack(outs["dks"]), jnp.stack(outs["dvs"]), jnp.stack(outs["mkp"]), jnp.stack(outs["mvp"]),
            jnp.stack(outs["sgs"]))
```

```python
import functools
import math

import jax
import jax.numpy as jnp
from jax import lax
from jax.experimental import pallas as pl
from jax.experimental.pallas import tpu as pltpu

F32 = jnp.float32
BF16 = jnp.bfloat16

DA_HEADS = 8
DA_DK = 64
DA_DV = 2 * DA_DK
SG_GROUPS = 8
CHUNK = 128
MX_HEADS = 4
MX_DH = 256
N_BUCKETS = 32
MAX_DISTANCE = 128
N_BRANCH = 3
RMS_EPS = 1e-6
NEG_INF = -1e30
LOG2E = math.log2(math.e)

LANES = 128
BF16_SUBLANES = 16
VMEM_LIMIT_BYTES = 56 * 1024 * 1024
MAX_PAGES_PER_STEP = 16


def _cparams(*sem):
    return pltpu.CompilerParams(dimension_semantics=sem, vmem_limit_bytes=VMEM_LIMIT_BYTES)


def _rms(x, g):
    ms = jnp.mean(x * x, axis=-1, keepdims=True)
    return x * lax.rsqrt(ms + RMS_EPS) * g


def _div_pow2(x, n):
    assert n > 0 and n & (n - 1) == 0, n
    return x >> (n.bit_length() - 1)


def _nt_dot(a, b):
    return lax.dot_general(a, b, (((1,), (1,)), ((), ())), preferred_element_type=F32)


def _lane_tile(x, n):
    return x if n == 1 else jnp.concatenate([x] * n, axis=1)


def _norm_matmul_kernel(x_ref, g_ref, w_ref, sgn_ref, z_ref, *copy_refs, sections, tn):
    hn = _rms(x_ref[...], g_ref[...]).astype(BF16)
    for sec, (act, copy_idx) in enumerate(sections):
        cols = slice(sec * tn, (sec + 1) * tn)
        acc = jnp.dot(hn, w_ref[:, cols], preferred_element_type=F32)
        if act[0] == "scale":
            val = acc * act[1]
        elif act[0] == "gelu":
            val = jax.nn.gelu(acc)
        elif act[0] == "gelu_rms":
            val = _rms(jax.nn.gelu(acc), sgn_ref[...])
        elif act[0] == "sigmoid":
            val = jax.nn.sigmoid(acc)
        else:
            val = acc
        z_ref[:, cols] = val.astype(z_ref.dtype)
        if copy_idx is not None:
            cref = copy_refs[copy_idx]
            if len(cref.shape) == 2:
                cref[...] = val
            else:
                hd = cref.shape[2]
                for hh in range(cref.shape[1]):
                    cref[:, hh, :] = val[:, hh * hd:(hh + 1) * hd]


def _norm_matmul(x, g, w, sgn, sections, z_dtype, tm, name, copy_heads=1):
    T, D = x.shape
    n_sec = len(sections)
    tn = w.shape[1] // n_sec
    n_copy = sum(1 for _, c in sections if c is not None)
    out_shape = [jax.ShapeDtypeStruct((T, n_sec * tn), z_dtype)]
    out_specs = [pl.BlockSpec((tm, n_sec * tn), lambda i: (i, 0))]
    for _ in range(n_copy):
        if copy_heads == 1:
            out_shape.append(jax.ShapeDtypeStruct((T, tn), F32))
            out_specs.append(pl.BlockSpec((tm, tn), lambda i: (i, 0)))
        else:
            out_shape.append(jax.ShapeDtypeStruct((T, copy_heads, tn // copy_heads), F32))
            out_specs.append(pl.BlockSpec((tm, copy_heads, tn // copy_heads), lambda i: (i, 0, 0)))

    def resident(shape):
        return pl.BlockSpec(shape, lambda i: (0, 0), pipeline_mode=pl.Buffered(1))

    return pl.pallas_call(
        functools.partial(_norm_matmul_kernel, sections=tuple(sections), tn=tn),
        grid=(T // tm,),
        in_specs=[pl.BlockSpec((tm, D), lambda i: (i, 0)), resident((1, D)), resident(w.shape), resident((1, tn))],
        out_specs=out_specs,
        out_shape=out_shape,
        compiler_params=_cparams("parallel"),
        name=name,
    )(x, g, w, sgn)


def _t5_bias(dist, rb_ref, h):
    n = jnp.maximum(dist, 0)
    max_exact = N_BUCKETS // 2
    nf = jnp.maximum(n, 1).astype(F32)
    rel = jnp.log(nf / max_exact) / math.log(MAX_DISTANCE / max_exact) * (N_BUCKETS - max_exact)
    out = jnp.zeros(dist.shape, F32)
    for b in range(max_exact):
        out = jnp.where(n == b, rb_ref[b, h], out)
    out = jnp.where(n >= max_exact, rb_ref[max_exact, h], out)
    for b in range(max_exact + 1, N_BUCKETS):
        out = jnp.where(rel >= b - max_exact, rb_ref[b, h], out)
    return jnp.where(dist >= 0, out * LOG2E, NEG_INF)


def _far_bias(rb_ref, h):
    return rb_ref[N_BUCKETS - 1, h] * LOG2E


def _prompt_bias_kernel(rb_ref, near_ref):
    d = lax.broadcasted_iota(jnp.int32, (CHUNK, CHUNK), 0) - lax.broadcasted_iota(jnp.int32, (CHUNK, CHUNK), 1)
    for h in range(near_ref.shape[0]):
        far = _far_bias(rb_ref, h)
        near_ref[h, 0] = _t5_bias(d, rb_ref, h) - far
        near_ref[h, 1] = _t5_bias(d + CHUNK, rb_ref, h) - far


def _prompt_bias(rel_bias):
    assert MAX_DISTANCE <= CHUNK + 1
    H = rel_bias.shape[1]
    return pl.pallas_call(
        _prompt_bias_kernel,
        in_specs=[pl.BlockSpec(memory_space=pltpu.SMEM)],
        out_shape=jax.ShapeDtypeStruct((H, 2, CHUNK, CHUNK), F32),
        name="prompt_bias",
    )(rel_bias)


def _tile_bias(near_ref, rows, cols):
    fill = {-1: jnp.full((CHUNK, CHUNK), NEG_INF, F32), 2: jnp.zeros((CHUNK, CHUNK), F32)}

    def block(bd):
        return near_ref[bd] if 0 <= bd < 2 else fill[max(min(bd, 2), -1)]

    return jnp.concatenate([jnp.concatenate([block(bi - bj) for bj in range(cols // CHUNK)], axis=1)
                            for bi in range(rows // CHUNK)], axis=0)


def _sample_bias_kernel(rb_ref, lam_p_ref, far_ref, last_ref, new_ref, lam_ref, *, n_tok, page, lam_init):
    hr = 2 * n_tok

    def grid(cols):
        tok = lax.broadcasted_iota(jnp.int32, (hr, cols), 0) & (n_tok - 1)
        col = lax.broadcasted_iota(jnp.int32, (hr, cols), 1)
        return tok, col, _div_pow2(col, DA_HEADS), col & (DA_HEADS - 1)

    tok, _, key, kh = grid(page * DA_HEADS)
    tok_n, col_n, key_n, kh_n = grid(page)
    for h in range(DA_HEADS):
        sl = slice(h * hr, (h + 1) * hr)
        far_ref[sl, :] = jnp.where(kh == h, _far_bias(rb_ref, h), NEG_INF)
        last_ref[sl, :] = jnp.where(kh == h, _t5_bias(page + tok - key, rb_ref, h), NEG_INF)
        new_ref[sl, :] = jnp.where((kh_n == h) & (col_n < n_tok * DA_HEADS), _t5_bias(tok_n - key_n, rb_ref, h), NEG_INF)
    lp = lam_p_ref[...]
    s1 = jnp.sum(lp[0:1, :] * lp[1:2, :], axis=-1, keepdims=True)
    s2 = jnp.sum(lp[2:3, :] * lp[3:4, :], axis=-1, keepdims=True)
    lam = jnp.exp(s1) - jnp.exp(s2) + lam_init
    lam_ref[...] = jnp.broadcast_to(lam, lam_ref.shape)


def _sample_bias(rel_bias, da_lam_l, n_tok, page, lam_init):
    assert n_tok & (n_tok - 1) == 0 and n_tok * DA_HEADS <= page
    rows = DA_HEADS * 2 * n_tok
    wide = jax.ShapeDtypeStruct((rows, page * DA_HEADS), F32)
    return pl.pallas_call(
        functools.partial(_sample_bias_kernel, n_tok=n_tok, page=page, lam_init=lam_init),
        in_specs=[pl.BlockSpec(memory_space=pltpu.SMEM), pl.BlockSpec(memory_space=pltpu.VMEM)],
        out_shape=[wide, wide, jax.ShapeDtypeStruct((rows, page), F32), jax.ShapeDtypeStruct((8, LANES), F32)],
        name="sample_bias",
    )(rel_bias, da_lam_l)


def _split_components(q):
    lane = lax.broadcasted_iota(jnp.int32, q.shape, 1)
    zero = jnp.zeros_like(q)
    return jnp.concatenate([jnp.where(lane < DA_DK, q, zero), jnp.where(lane >= DA_DK, q, zero)], axis=0)


def _diff_out(o1, o2, lam_row, subln, lam_init):
    o = o1 - lam_row * o2
    return _rms(o, subln) * (1.0 - lam_init)


def _softmax_update(m, acc, s_list, v_list, mxu_sums):
    m_new = m
    for s in s_list:
        m_new = jnp.maximum(m_new, jnp.max(s, axis=-1, keepdims=True))
    acc = acc * _lane_tile(jnp.exp2(m - m_new), 2)
    for s, v in zip(s_list, v_list):
        e = jnp.exp2(s - _lane_tile(m_new, s.shape[1] // LANES))
        if mxu_sums:
            v1 = jnp.concatenate([v, jnp.ones(v.shape, BF16)], axis=1)
            acc = acc + jnp.dot(e.astype(BF16), v1, preferred_element_type=F32)
        else:
            pv = jnp.dot(e.astype(BF16), v, preferred_element_type=F32)
            l = jnp.broadcast_to(jnp.sum(e, axis=-1, keepdims=True), pv.shape)
            acc = acc + jnp.concatenate([pv, l], axis=1)
    return m_new, acc


def _attn_prompt_kernel(lam_ref, q_ref, k_ref, v_ref, near_ref, sub_ref, o_ref, *, t, nq, lam_init):
    hb = t // 2
    near1 = near_ref[1]
    bias_a = _tile_bias(near_ref, t, hb)
    bias_b = _tile_bias(near_ref, hb, hb)

    def both(s, b):
        n = s.shape[0] // 2
        return jnp.concatenate([s[:n] + b, s[n:] + b], axis=0)

    def sub_fix(sr):
        return jnp.concatenate([sr[:, :t - CHUNK], sr[:, t - CHUNK:] + near1], axis=1)

    def second_half(x):
        return jnp.concatenate([x[hb:t], x[t + hb:]], axis=0)

    def put_second_half(x, xb):
        return jnp.concatenate([x[:hb], xb[:hb], x[t:t + hb], xb[hb:]], axis=0)

    for qi in range(nq):
        q0 = qi * t
        qq = _split_components(q_ref[q0:q0 + t, :])
        m = jnp.full((2 * t, LANES), -jnp.inf, F32)
        acc = jnp.zeros((2 * t, 2 * LANES), F32)
        for kt in range(qi):
            keys = slice(kt * t, (kt + 1) * t)
            s = _nt_dot(qq, k_ref[keys, :])
            if kt == qi - 1:
                s = jnp.concatenate([sub_fix(s[:CHUNK]), s[CHUNK:t], sub_fix(s[t:t + CHUNK]), s[t + CHUNK:]], axis=0)
            m, acc = _softmax_update(m, acc, [s], [v_ref[keys, :]], True)
        keys = slice(q0, q0 + hb)
        m, acc = _softmax_update(m, acc, [both(_nt_dot(qq, k_ref[keys, :]), bias_a)], [v_ref[keys, :]], True)
        keys = slice(q0 + hb, q0 + t)
        s = both(_nt_dot(second_half(qq), k_ref[keys, :]), bias_b)
        _, acc_b = _softmax_update(second_half(m), second_half(acc), [s], [v_ref[keys, :]], True)
        acc = put_second_half(acc, acc_b)
        o = acc[:, :DA_DV] / acc[:, DA_DV:]
        out = _diff_out(o[:t], o[t:], lam_ref[0:1, :], sub_ref[...], lam_init)
        o_ref[q0:q0 + t, :] = out.astype(o_ref.dtype)


def _attn_prompt(z, lam, near, subln, B, S, t, lam_init):
    assert S % t == 0 and t % (2 * CHUNK) == 0 and MAX_DISTANCE <= CHUNK + 1
    H = DA_HEADS
    return pl.pallas_call(
        functools.partial(_attn_prompt_kernel, t=t, nq=S // t, lam_init=lam_init),
        grid=(B, H),
        in_specs=[
            pl.BlockSpec((8, LANES), lambda b, h: (0, 0)),
            pl.BlockSpec((S, DA_DV), lambda b, h: (b, h)),
            pl.BlockSpec((S, DA_DV), lambda b, h: (b, H + h)),
            pl.BlockSpec((S, DA_DV), lambda b, h: (b, 2 * H + h)),
            pl.BlockSpec((None, 2, CHUNK, CHUNK), lambda b, h: (h, 0, 0, 0)),
            pl.BlockSpec((1, DA_DV), lambda b, h: (0, 0)),
        ],
        out_specs=pl.BlockSpec((S, DA_DV), lambda b, h: (b, h)),
        out_shape=jax.ShapeDtypeStruct((B * S, H * DA_DV), BF16),
        compiler_params=_cparams("parallel", "arbitrary"),
        name="attn_prompt",
    )(lam, z, z, z, near, subln)


def _attn_sample_kernel(pt_ref, lam_ref, q_ref, kn_ref, vn_ref, far_ref, last_ref, new_ref, sub_ref, *rest,
                        spb, pps, n_tok, page, lam_init):
    n_in = spb * pps
    kp_refs, vp_refs = rest[:n_in], rest[n_in:2 * n_in]
    o_ref, qq_ref, m_ref, acc_ref = rest[2 * n_in:]
    p = pl.program_id(1)
    last = pl.num_programs(1) - 1
    hr = 2 * n_tok
    flat = page * DA_HEADS

    @pl.when(p == 0)
    def _():
        for j in range(spb):
            q = q_ref[j * n_tok:(j + 1) * n_tok, :]
            for h in range(DA_HEADS):
                qq_ref[j, h * hr:(h + 1) * hr, :] = _split_components(q[:, h * DA_DV:(h + 1) * DA_DV]).astype(BF16)
        m_ref[...] = jnp.full(m_ref.shape, -jnp.inf, F32)
        acc_ref[...] = jnp.zeros(acc_ref.shape, F32)

    far = far_ref[...]
    tail = jnp.where(p == last, last_ref[...], far)
    for j in range(spb):
        qq = qq_ref[j]
        s_list, v_list = [], []
        for r in range(pps):
            s = _nt_dot(qq, kp_refs[j * pps + r][...].reshape(flat, DA_DV).astype(BF16))
            s_list.append(s + (tail if r == pps - 1 else far))
            v_list.append(vp_refs[j * pps + r][...].reshape(flat, DA_DV).astype(BF16))
        m, acc = _softmax_update(m_ref[j], acc_ref[j], s_list, v_list, False)
        m_ref[j] = m
        acc_ref[j] = acc

    @pl.when(p == last)
    def _():
        pad = jnp.zeros((page - n_tok * DA_HEADS, DA_DV), F32)
        for j in range(spb):
            toks = slice(j * n_tok, (j + 1) * n_tok)

            def new_rows(ref):
                return jnp.concatenate([ref[toks].reshape(n_tok * DA_HEADS, DA_DV), pad], axis=0).astype(BF16)

            s_new = _nt_dot(qq_ref[j], new_rows(kn_ref)) + new_ref[...]
            _, acc2 = _softmax_update(m_ref[j], acc_ref[j], [s_new], [new_rows(vn_ref)], False)
            o = acc2[:, :DA_DV] / acc2[:, DA_DV:]
            for h in range(DA_HEADS):
                oh = o[h * hr:(h + 1) * hr]
                out = _diff_out(oh[:n_tok], oh[n_tok:], lam_ref[0:1, :], sub_ref[...], lam_init)
                o_ref[toks, h * DA_DV:(h + 1) * DA_DV] = out.astype(o_ref.dtype)


def _attn_sample(z, k_new, v_new, page_idx, cache_k, cache_v, lam, far, last, new, subln, N, n_tok, lam_init, spb, pps):
    n_pages = page_idx.shape[1]
    page = cache_k.shape[1]
    width = DA_HEADS * DA_DV
    rows = DA_HEADS * 2 * n_tok
    assert (2 * n_tok) % BF16_SUBLANES == 0 and n_pages % pps == 0 and N % spb == 0

    def const(shape):
        return pl.BlockSpec(shape, lambda n, p, pt: (0,) * len(shape))

    def page_spec(j, r):
        return pl.BlockSpec((None, page, DA_HEADS, DA_DV), lambda n, p, pt: (pt[n * spb + j, p * pps + r], 0, 0, 0))

    page_specs = [page_spec(j, r) for j in range(spb) for r in range(pps)]
    new_spec = pl.BlockSpec((spb * n_tok, DA_HEADS, DA_DV), lambda n, p, pt: (n, 0, 0))
    grid_spec = pltpu.PrefetchScalarGridSpec(
        num_scalar_prefetch=1,
        grid=(N // spb, n_pages // pps),
        in_specs=[
            const((8, LANES)),
            pl.BlockSpec((spb * n_tok, width), lambda n, p, pt: (n, 0)),
            new_spec, new_spec,
            const((rows, page * DA_HEADS)), const((rows, page * DA_HEADS)), const((rows, page)),
            const((1, DA_DV)),
        ] + page_specs + page_specs,
        out_specs=pl.BlockSpec((spb * n_tok, width), lambda n, p, pt: (n, 0)),
        scratch_shapes=[pltpu.VMEM((spb, rows, DA_DV), BF16), pltpu.VMEM((spb, rows, LANES), F32),
                        pltpu.VMEM((spb, rows, 2 * LANES), F32)],
    )
    n_in = spb * pps
    return pl.pallas_call(
        functools.partial(_attn_sample_kernel, spb=spb, pps=pps, n_tok=n_tok, page=page, lam_init=lam_init),
        grid_spec=grid_spec,
        out_shape=jax.ShapeDtypeStruct((N * n_tok, width), F32),
        compiler_params=_cparams("parallel", "arbitrary"),
        name="attn_sample",
    )(page_idx, lam, z, k_new, v_new, far, last, new, subln, *([cache_k] * n_in), *([cache_v] * n_in))


def _sgate_kernel(u_ref, vn_ref, w_ref, b_ref, o_ref, *, seg, n_chunk):
    i = lax.broadcasted_iota(jnp.int32, (CHUNK, CHUNK), 0)
    j = lax.broadcasted_iota(jnp.int32, (CHUNK, CHUNK), 1)
    keep = (_div_pow2(i, seg) == _div_pow2(j, seg)) & (j <= i)
    gd = u_ref.shape[1] // SG_GROUPS
    for g in range(SG_GROUPS):
        w = jnp.where(keep, w_ref[g], 0.0).astype(BF16)
        cols = slice(g * gd, (g + 1) * gd)
        for c in range(n_chunk):
            rws = slice(c * CHUNK, (c + 1) * CHUNK)
            mix = jnp.dot(w, vn_ref[rws, cols].astype(BF16), preferred_element_type=F32) + b_ref[:, cols]
            o_ref[rws, cols] = (u_ref[rws, cols].astype(F32) * mix).astype(o_ref.dtype)


def _sgate(z, w, b, u_sec, vn_sec, n_chunk):
    T = z.shape[0]
    width = b.shape[1]
    rb = n_chunk * CHUNK
    return pl.pallas_call(
        functools.partial(_sgate_kernel, seg=CHUNK, n_chunk=n_chunk),
        grid=(T // rb,),
        in_specs=[
            pl.BlockSpec((rb, width), lambda i: (i, u_sec)),
            pl.BlockSpec((rb, width), lambda i: (i, vn_sec)),
            pl.BlockSpec((SG_GROUPS, CHUNK, CHUNK), lambda i: (0, 0, 0)),
            pl.BlockSpec((CHUNK, width), lambda i: (0, 0)),
        ],
        out_specs=pl.BlockSpec((rb, width), lambda i: (i, 0)),
        out_shape=jax.ShapeDtypeStruct((T, width), BF16),
        compiler_params=_cparams("parallel"),
        name="sgate",
    )(z, z, w, b)


def _sgate_seq_kernel(w_ref, b_ref, u_ref, vn_ref, o_ref, wbd_ref, bias_ref, *, seg):
    gd = u_ref.shape[1] // SG_GROUPS

    @pl.when(pl.program_id(0) == 0)
    def _():
        i = lax.broadcasted_iota(jnp.int32, (CHUNK, CHUNK), 0)
        j = lax.broadcasted_iota(jnp.int32, (CHUNK, CHUNK), 1)
        same = _div_pow2(i, seg) == _div_pow2(j, seg)
        ti, tj = i & (seg - 1), j & (seg - 1)
        for g in range(SG_GROUPS):
            w = jnp.zeros((CHUNK, CHUNK), F32)
            bias = jnp.zeros((CHUNK, gd), F32)
            for a in range(seg):
                bias = jnp.where(ti == a, b_ref[g, a], bias)
                for b in range(a + 1):
                    w = jnp.where((ti == a) & (tj == b), w_ref[g * seg + a, b], w)
            wbd_ref[g] = jnp.where(same, w, 0.0).astype(BF16)
            bias_ref[:, g * gd:(g + 1) * gd] = bias

    for g in range(SG_GROUPS):
        cols = slice(g * gd, (g + 1) * gd)
        mix = jnp.dot(wbd_ref[g], vn_ref[:, cols].astype(BF16), preferred_element_type=F32) + bias_ref[:, cols]
        o_ref[:, cols] = (u_ref[:, cols].astype(F32) * mix).astype(o_ref.dtype)


def _sgate_seq(z, w, b, seg, u_sec, vn_sec):
    T = z.shape[0]
    width = SG_GROUPS * CHUNK
    assert CHUNK % seg == 0 and seg & (seg - 1) == 0 and T % CHUNK == 0
    return pl.pallas_call(
        functools.partial(_sgate_seq_kernel, seg=seg),
        grid=(T // CHUNK,),
        in_specs=[
            pl.BlockSpec(memory_space=pltpu.SMEM), pl.BlockSpec(memory_space=pltpu.SMEM),
            pl.BlockSpec((CHUNK, width), lambda i: (i, u_sec)),
            pl.BlockSpec((CHUNK, width), lambda i: (i, vn_sec)),
        ],
        out_specs=pl.BlockSpec((CHUNK, width), lambda i: (i, 0)),
        out_shape=jax.ShapeDtypeStruct((T, width), BF16),
        scratch_shapes=[pltpu.VMEM((SG_GROUPS, CHUNK, CHUNK), BF16), pltpu.VMEM((CHUNK, width), F32)],
        compiler_params=_cparams("arbitrary"),
        name="sgate_seq",
    )(w[:, :seg, :seg].reshape(SG_GROUPS * seg, seg), b[:, :seg], z, z)


def _xattn_prompt_kernel(q_ref, k_ref, v_ref, o_ref):
    heads = [slice(h * MX_DH, (h + 1) * MX_DH) for h in range(MX_HEADS)]
    scores = [_nt_dot(q_ref[:, cols], k_ref[:, cols]) for cols in heads]
    exps = [jnp.exp(s - jnp.max(s, axis=-1, keepdims=True)) for s in scores]
    for cols, e in zip(heads, exps):
        o = jnp.dot(e.astype(BF16), v_ref[:, cols], preferred_element_type=F32)
        o_ref[:, cols] = (o / jnp.sum(e, axis=-1, keepdims=True)).astype(o_ref.dtype)


def _xattn_prompt(z, mem_kv, B, S, tq, q_sec):
    nq = S // tq
    n_mem = mem_kv.shape[0] // B
    width = MX_HEADS * MX_DH
    return pl.pallas_call(
        _xattn_prompt_kernel,
        grid=(B, nq),
        in_specs=[
            pl.BlockSpec((tq, width), lambda b, i: (b * nq + i, q_sec)),
            pl.BlockSpec((n_mem, width), lambda b, i: (b, 0)),
            pl.BlockSpec((n_mem, width), lambda b, i: (b, 1)),
        ],
        out_specs=pl.BlockSpec((tq, width), lambda b, i: (b * nq + i, 0)),
        out_shape=jax.ShapeDtypeStruct((B * S, width), BF16),
        compiler_params=_cparams("parallel", "arbitrary"),
        name="xattn_prompt",
    )(z, mem_kv, mem_kv)


def _xattn_sample_kernel(q_ref, k_ref, v_ref, o_ref, *, spb, n_tok):
    n_mem = k_ref.shape[1]
    rows, flat = MX_HEADS * n_tok, n_mem * MX_HEADS
    r = lax.broadcasted_iota(jnp.int32, (rows, flat), 0)
    c = lax.broadcasted_iota(jnp.int32, (rows, flat), 1)
    own = (c & (MX_HEADS - 1)) == _div_pow2(r, n_tok)
    for j in range(spb):
        toks = slice(j * n_tok, (j + 1) * n_tok)
        q = q_ref[toks, :]
        qq = jnp.concatenate([q[:, h * MX_DH:(h + 1) * MX_DH] for h in range(MX_HEADS)], axis=0).astype(BF16)
        s = jnp.where(own, _nt_dot(qq, k_ref[j].reshape(flat, MX_DH).astype(BF16)), NEG_INF)
        e = jnp.exp(s - jnp.max(s, axis=-1, keepdims=True))
        o = jnp.dot(e.astype(BF16), v_ref[j].reshape(flat, MX_DH).astype(BF16), preferred_element_type=F32)
        o = o / jnp.sum(e, axis=-1, keepdims=True)
        for h in range(MX_HEADS):
            o_ref[toks, h * MX_DH:(h + 1) * MX_DH] = o[h * n_tok:(h + 1) * n_tok].astype(o_ref.dtype)


def _xattn_sample(z, mem_k, mem_v, N, n_tok, q_sec, seq_off, spb):
    n_mem = mem_k.shape[1]
    width = MX_HEADS * MX_DH
    assert (MX_HEADS * n_tok) % BF16_SUBLANES == 0 and N % spb == 0 and seq_off % spb == 0
    mem_spec = pl.BlockSpec((spb, n_mem, MX_HEADS, MX_DH), lambda n: (seq_off // spb + n, 0, 0, 0))
    return pl.pallas_call(
        functools.partial(_xattn_sample_kernel, spb=spb, n_tok=n_tok),
        grid=(N // spb,),
        in_specs=[pl.BlockSpec((spb * n_tok, width), lambda n: (n, q_sec)), mem_spec, mem_spec],
        out_specs=pl.BlockSpec((spb * n_tok, width), lambda n: (n, 0)),
        out_shape=jax.ShapeDtypeStruct((N * n_tok, width), F32),
        compiler_params=_cparams("parallel"),
        name="xattn_sample",
    )(z, mem_k, mem_v)


def _post_kernel(x_ref, da_ref, sg_ref, mx_ref, gate_ref, wb_ref, wo_ref, gf_ref, wu_ref, wd_ref, gfin_ref,
                 o_ref, y_ref, *, d, d_ff, fc, n_sub, final_norm):
    branches = (da_ref, sg_ref, mx_ref)
    rs = x_ref.shape[0] // n_sub
    subs = [slice(sb * rs, (sb + 1) * rs) for sb in range(n_sub)]
    merged = [None] * n_sub
    for k in range(N_BRANCH):
        for i, rws in enumerate(subs):
            br = jnp.dot(branches[k][rws, :].astype(BF16), wb_ref[k * d:(k + 1) * d, :], preferred_element_type=F32)
            term = gate_ref[rws, k * d:(k + 1) * d].astype(F32) * br
            merged[i] = term if merged[i] is None else merged[i] + term
    h2 = []
    for i, rws in enumerate(subs):
        x1 = x_ref[rws, :] + jnp.dot(merged[i].astype(BF16), wo_ref[...], preferred_element_type=F32)
        y_ref[rws, :] = x1
        h2.append(_rms(x1, gf_ref[...]).astype(BF16))
    for c in range(d_ff // fc):
        up = [(jnp.dot(h, wu_ref[:, c * fc:(c + 1) * fc], preferred_element_type=F32),
               jnp.dot(h, wu_ref[:, d_ff + c * fc:d_ff + (c + 1) * fc], preferred_element_type=F32)) for h in h2]
        for (a, b), rws in zip(up, subs):
            act = (jax.nn.silu(a) * b).astype(BF16)
            y_ref[rws, :] += jnp.dot(act, wd_ref[c * fc:(c + 1) * fc, :], preferred_element_type=F32)
    for rws in subs:
        y = y_ref[rws, :]
        o_ref[rws, :] = _rms(y, gfin_ref[...]) if final_norm else y


def _post(x, o_da, o_sg, o_mx, z, gate_sec, wb, wo, g_ffn, wu, wd, g_final, tm, final_norm):
    T, d = x.shape
    d_ff = wd.shape[0]
    fc = 256
    n_sub = 2 if tm >= 512 else 1
    row = lambda i: (i, 0)
    whole = lambda i: (0, 0)

    def resident(shape):
        return pl.BlockSpec(shape, whole, pipeline_mode=pl.Buffered(1))

    return pl.pallas_call(
        functools.partial(_post_kernel, d=d, d_ff=d_ff, fc=fc, n_sub=n_sub, final_norm=final_norm),
        grid=(T // tm,),
        in_specs=[
            pl.BlockSpec((tm, d), row), pl.BlockSpec((tm, d), row), pl.BlockSpec((tm, d), row), pl.BlockSpec((tm, d), row),
            pl.BlockSpec((tm, N_BRANCH * d), lambda i: (i, gate_sec)),
            resident(wb.shape), resident(wo.shape), resident((1, d)), resident(wu.shape), resident(wd.shape),
            resident((1, d)),
        ],
        out_specs=pl.BlockSpec((tm, d), row),
        out_shape=jax.ShapeDtypeStruct((T, d), F32),
        scratch_shapes=[pltpu.VMEM((tm, d), F32)],
        compiler_params=_cparams("parallel"),
        name="post",
    )(x, o_da, o_sg, o_mx, z, wb, wo, g_ffn, wu, wd, g_final)


_IN_SECTIONS = ("q", "k", "v", "u", "s", "m", "g0", "g1", "g2")
_IN_ACTS = {
    "q": ("scale", DA_DK ** -0.5 * LOG2E), "k": ("none",), "v": ("none",), "u": ("gelu",), "s": ("gelu_rms",),
    "m": ("scale", MX_DH ** -0.5), "g0": ("sigmoid",), "g1": ("sigmoid",), "g2": ("sigmoid",),
}
_IN_COPIES = {"k": 0, "v": 1}


def kernel(x_prompt, x_sample, mem_prompt, cache_da_k, cache_da_v, cache_mem_k, cache_mem_v, page_table, g_attn, w_in,
           da_lam, da_subln, rel_bias, sg_norm, sg_w, sg_b, g_mem, w_mem_kv, w_branch, w_out, g_ffn, w_up, w_down,
           g_final):
    B, S, D = x_prompt.shape
    N, n_tok, _ = x_sample.shape
    depth, n_pool, page = cache_da_k.shape[:3]
    n_mem = mem_prompt.shape[1]
    n_pages = page_table.shape[1]
    width = DA_HEADS * DA_DV
    assert D == width == SG_GROUPS * CHUNK == MX_HEADS * MX_DH and page == CHUNK and n_tok <= CHUNK

    t_attn = min(512, S)
    pps = max(d for d in range(1, MAX_PAGES_PER_STEP + 1) if n_pages % d == 0)
    spb_da = 1
    spb_mx = 4 if N % 4 == 0 else 1
    tm_p = 256
    tm_s = min(256, N * n_tok)
    tm_post = 512
    sec = {n: i for i, n in enumerate(_IN_SECTIONS)}
    in_sections = [(_IN_ACTS[n], _IN_COPIES.get(n)) for n in _IN_SECTIONS]

    xp = x_prompt.reshape(B * S, D)
    xs = x_sample.reshape(N * n_tok, D)
    mem = mem_prompt.reshape(B * n_mem, D)
    ck = cache_da_k.reshape(depth * n_pool, page, DA_HEADS, DA_DV)
    cv = cache_da_v.reshape(depth * n_pool, page, DA_HEADS, DA_DV)
    cmk = cache_mem_k.reshape(depth * N, n_mem, MX_HEADS, MX_DH)
    cmv = cache_mem_v.reshape(depth * N, n_mem, MX_HEADS, MX_DH)
    row = lambda a: a.reshape(1, -1)

    near_bias = _prompt_bias(rel_bias)
    seg_s = n_tok
    outs = {k: [] for k in ("dkp", "dvp", "dks", "dvs", "mkp", "mvp", "sgs")}
    for l in range(depth):
        lam_init = 0.8 - 0.6 * math.exp(-0.3 * l)
        w_in_l = w_in[l].astype(BF16)
        wb_l, wo_l = w_branch[l].astype(BF16), w_out[l].astype(BF16)
        wu_l, wd_l = w_up[l].astype(BF16), w_down[l].astype(BF16)
        far, last, new, lam = _sample_bias(rel_bias, da_lam[l], n_tok, page, lam_init)
        subln = row(da_subln[l])
        sgb_p = jnp.repeat(sg_b[l].T, CHUNK, axis=1)

        mkv16, mk32, mv32 = _norm_matmul(mem, row(g_mem[l]), w_mem_kv[l].astype(BF16), row(sg_norm[l]),
                                         [(("none",), 0), (("none",), 1)], BF16, min(512, B * n_mem), "mem_kv",
                                         copy_heads=MX_HEADS)
        zp, kp32, vp32 = _norm_matmul(xp, row(g_attn[l]), w_in_l, row(sg_norm[l]), in_sections, BF16, tm_p,
                                      "inproj_prompt")
        oda_p = _attn_prompt(zp, lam, near_bias, subln, B, S, t_attn, lam_init)
        osg_p = _sgate(zp, sg_w[l], sgb_p, sec["u"], sec["s"], 8)
        omx_p = _xattn_prompt(zp, mkv16, B, S, 512, sec["m"])
        xp = _post(xp, oda_p, osg_p, omx_p, zp, sec["g0"] // N_BRANCH, wb_l, wo_l, row(g_ffn[l]), wu_l, wd_l,
                   row(g_final), tm_post, l == depth - 1)

        zs, ks32, vs32 = _norm_matmul(xs, row(g_attn[l]), w_in_l, row(sg_norm[l]), in_sections, F32, tm_s,
                                      "inproj_sample", copy_heads=DA_HEADS)
        oda_s = _attn_sample(zs, ks32, vs32, page_table + l * n_pool, ck, cv, lam, far, last, new, subln, N, n_tok,
                             lam_init, spb_da, pps)
        osg_s = _sgate_seq(zs, sg_w[l], sg_b[l], seg_s, sec["u"], sec["s"])
        omx_s = _xattn_sample(zs, cmk, cmv, N, n_tok, sec["m"], l * N, spb_mx)
        xs = _post(xs, oda_s, osg_s, omx_s, zs, sec["g0"] // N_BRANCH, wb_l, wo_l, row(g_ffn[l]), wu_l, wd_l,
                   row(g_final), tm_s, l == depth - 1)

        outs["dkp"].append(kp32.reshape(B, S, DA_HEADS, 2 * DA_DK))
        outs["dvp"].append(vp32.reshape(B, S, DA_HEADS, DA_DV))
        outs["dks"].append(ks32.reshape(N, n_tok, DA_HEADS, 2 * DA_DK))
        outs["dvs"].append(vs32.reshape(N, n_tok, DA_HEADS, DA_DV))
        outs["mkp"].append(mk32.reshape(B, n_mem, MX_HEADS, MX_DH))
        outs["mvp"].append(mv32.reshape(B, n_mem, MX_HEADS, MX_DH))
        outs["sgs"].append(zs[:, sec["s"] * D:(sec["s"] + 1) * D].reshape(N, n_tok, D))

    return (xp.reshape(B, S, D), xs.reshape(N, n_tok, D), jnp.stack(outs["dkp"]), jnp.stack(outs["dvp"]),
            jnp.stack(outs["dks"]), jnp.stack(outs["dvs"]), jnp.stack(outs["mkp"]), jnp.stack(outs["mvp"]),
            jnp.stack(outs["sgs"]))
```

```python
import functools
import math

import jax
import jax.numpy as jnp
from jax import lax
from jax.experimental import pallas as pl
from jax.experimental.pallas import tpu as pltpu

F32 = jnp.float32
BF16 = jnp.bfloat16

DA_HEADS = 8
DA_DK = 64
DA_DV = 2 * DA_DK
SG_GROUPS = 8
CHUNK = 128
MX_HEADS = 4
MX_DH = 256
N_BUCKETS = 32
MAX_DISTANCE = 128
N_BRANCH = 3
RMS_EPS = 1e-6
NEG_INF = -1e30
LOG2E = math.log2(math.e)

LANES = 128
BF16_SUBLANES = 16
VMEM_LIMIT_BYTES = 56 * 1024 * 1024
MAX_PAGES_PER_STEP = 16


def _cparams(*sem):
    return pltpu.CompilerParams(dimension_semantics=sem, vmem_limit_bytes=VMEM_LIMIT_BYTES)


def _rms(x, g):
    ms = jnp.mean(x * x, axis=-1, keepdims=True)
    return x * lax.rsqrt(ms + RMS_EPS) * g


def _div_pow2(x, n):
    assert n > 0 and n & (n - 1) == 0, n
    return x >> (n.bit_length() - 1)


def _nt_dot(a, b):
    return lax.dot_general(a, b, (((1,), (1,)), ((), ())), preferred_element_type=F32)


def _lane_tile(x, n):
    return x if n == 1 else jnp.concatenate([x] * n, axis=1)


def _norm_matmul_kernel(x_ref, g_ref, w_ref, sgn_ref, z_ref, *copy_refs, sections, tn):
    hn = _rms(x_ref[...], g_ref[...]).astype(BF16)
    for sec, (act, copy_idx) in enumerate(sections):
        cols = slice(sec * tn, (sec + 1) * tn)
        acc = jnp.dot(hn, w_ref[:, cols], preferred_element_type=F32)
        if act[0] == "scale":
            val = acc * act[1]
        elif act[0] == "gelu":
            val = jax.nn.gelu(acc)
        elif act[0] == "gelu_rms":
            val = _rms(jax.nn.gelu(acc), sgn_ref[...])
        elif act[0] == "sigmoid":
            val = jax.nn.sigmoid(acc)
        else:
            val = acc
        z_ref[:, cols] = val.astype(z_ref.dtype)
        if copy_idx is not None:
            cref = copy_refs[copy_idx]
            if len(cref.shape) == 2:
                cref[...] = val
            else:
                hd = cref.shape[2]
                for hh in range(cref.shape[1]):
                    cref[:, hh, :] = val[:, hh * hd:(hh + 1) * hd]


def _norm_matmul(x, g, w, sgn, sections, z_dtype, tm, name, copy_heads=1):
    T, D = x.shape
    n_sec = len(sections)
    tn = w.shape[1] // n_sec
    n_copy = sum(1 for _, c in sections if c is not None)
    out_shape = [jax.ShapeDtypeStruct((T, n_sec * tn), z_dtype)]
    out_specs = [pl.BlockSpec((tm, n_sec * tn), lambda i: (i, 0))]
    for _ in range(n_copy):
        if copy_heads == 1:
            out_shape.append(jax.ShapeDtypeStruct((T, tn), F32))
            out_specs.append(pl.BlockSpec((tm, tn), lambda i: (i, 0)))
        else:
            out_shape.append(jax.ShapeDtypeStruct((T, copy_heads, tn // copy_heads), F32))
            out_specs.append(pl.BlockSpec((tm, copy_heads, tn // copy_heads), lambda i: (i, 0, 0)))

    def resident(shape):
        return pl.BlockSpec(shape, lambda i: (0, 0), pipeline_mode=pl.Buffered(1))

    return pl.pallas_call(
        functools.partial(_norm_matmul_kernel, sections=tuple(sections), tn=tn),
        grid=(T // tm,),
        in_specs=[pl.BlockSpec((tm, D), lambda i: (i, 0)), resident((1, D)), resident(w.shape), resident((1, tn))],
        out_specs=out_specs,
        out_shape=out_shape,
        compiler_params=_cparams("parallel"),
        name=name,
    )(x, g, w, sgn)


def _t5_bias(dist, rb_ref, h):
    n = jnp.maximum(dist, 0)
    max_exact = N_BUCKETS // 2
    nf = jnp.maximum(n, 1).astype(F32)
    rel = jnp.log(nf / max_exact) / math.log(MAX_DISTANCE / max_exact) * (N_BUCKETS - max_exact)
    out = jnp.zeros(dist.shape, F32)
    for b in range(max_exact):
        out = jnp.where(n == b, rb_ref[b, h], out)
    out = jnp.where(n >= max_exact, rb_ref[max_exact, h], out)
    for b in range(max_exact + 1, N_BUCKETS):
        out = jnp.where(rel >= b - max_exact, rb_ref[b, h], out)
    return jnp.where(dist >= 0, out * LOG2E, NEG_INF)


def _far_bias(rb_ref, h):
    return rb_ref[N_BUCKETS - 1, h] * LOG2E


def _prompt_bias_kernel(rb_ref, near_ref):
    d = lax.broadcasted_iota(jnp.int32, (CHUNK, CHUNK), 0) - lax.broadcasted_iota(jnp.int32, (CHUNK, CHUNK), 1)
    for h in range(near_ref.shape[0]):
        far = _far_bias(rb_ref, h)
        near_ref[h, 0] = _t5_bias(d, rb_ref, h) - far
        near_ref[h, 1] = _t5_bias(d + CHUNK, rb_ref, h) - far


def _prompt_bias(rel_bias):
    assert MAX_DISTANCE <= CHUNK + 1
    H = rel_bias.shape[1]
    return pl.pallas_call(
        _prompt_bias_kernel,
        in_specs=[pl.BlockSpec(memory_space=pltpu.SMEM)],
        out_shape=jax.ShapeDtypeStruct((H, 2, CHUNK, CHUNK), F32),
        name="prompt_bias",
    )(rel_bias)


def _tile_bias(near_ref, rows, cols):
    fill = {-1: jnp.full((CHUNK, CHUNK), NEG_INF, F32), 2: jnp.zeros((CHUNK, CHUNK), F32)}

    def block(bd):
        return near_ref[bd] if 0 <= bd < 2 else fill[max(min(bd, 2), -1)]

    return jnp.concatenate([jnp.concatenate([block(bi - bj) for bj in range(cols // CHUNK)], axis=1)
                            for bi in range(rows // CHUNK)], axis=0)


def _sample_bias_kernel(rb_ref, lam_p_ref, far_ref, last_ref, new_ref, lam_ref, *, n_tok, page, lam_init):
    hr = 2 * n_tok

    def grid(cols):
        tok = lax.broadcasted_iota(jnp.int32, (hr, cols), 0) & (n_tok - 1)
        col = lax.broadcasted_iota(jnp.int32, (hr, cols), 1)
        return tok, col, _div_pow2(col, DA_HEADS), col & (DA_HEADS - 1)

    tok, _, key, kh = grid(page * DA_HEADS)
    tok_n, col_n, key_n, kh_n = grid(page)
    for h in range(DA_HEADS):
        sl = slice(h * hr, (h + 1) * hr)
        far_ref[sl, :] = jnp.where(kh == h, _far_bias(rb_ref, h), NEG_INF)
        last_ref[sl, :] = jnp.where(kh == h, _t5_bias(page + tok - key, rb_ref, h), NEG_INF)
        new_ref[sl, :] = jnp.where((kh_n == h) & (col_n < n_tok * DA_HEADS), _t5_bias(tok_n - key_n, rb_ref, h), NEG_INF)
    lp = lam_p_ref[...]
    s1 = jnp.sum(lp[0:1, :] * lp[1:2, :], axis=-1, keepdims=True)
    s2 = jnp.sum(lp[2:3, :] * lp[3:4, :], axis=-1, keepdims=True)
    lam = jnp.exp(s1) - jnp.exp(s2) + lam_init
    lam_ref[...] = jnp.broadcast_to(lam, lam_ref.shape)


def _sample_bias(rel_bias, da_lam_l, n_tok, page, lam_init):
    assert n_tok & (n_tok - 1) == 0 and n_tok * DA_HEADS <= page
    rows = DA_HEADS * 2 * n_tok
    wide = jax.ShapeDtypeStruct((rows, page * DA_HEADS), F32)
    return pl.pallas_call(
        functools.partial(_sample_bias_kernel, n_tok=n_tok, page=page, lam_init=lam_init),
        in_specs=[pl.BlockSpec(memory_space=pltpu.SMEM), pl.BlockSpec(memory_space=pltpu.VMEM)],
        out_shape=[wide, wide, jax.ShapeDtypeStruct((rows, page), F32), jax.ShapeDtypeStruct((8, LANES), F32)],
        name="sample_bias",
    )(rel_bias, da_lam_l)


def _split_components(q):
    lane = lax.broadcasted_iota(jnp.int32, q.shape, 1)
    zero = jnp.zeros_like(q)
    return jnp.concatenate([jnp.where(lane < DA_DK, q, zero), jnp.where(lane >= DA_DK, q, zero)], axis=0)


def _diff_out(o1, o2, lam_row, subln, lam_init):
    o = o1 - lam_row * o2
    return _rms(o, subln) * (1.0 - lam_init)


def _softmax_update(m, acc, s_list, v_list, mxu_sums):
    m_new = m
    for s in s_list:
        m_new = jnp.maximum(m_new, jnp.max(s, axis=-1, keepdims=True))
    acc = acc * _lane_tile(jnp.exp2(m - m_new), 2)
    for s, v in zip(s_list, v_list):
        e = jnp.exp2(s - _lane_tile(m_new, s.shape[1] // LANES))
        if mxu_sums:
            v1 = jnp.concatenate([v, jnp.ones(v.shape, BF16)], axis=1)
            acc = acc + jnp.dot(e.astype(BF16), v1, preferred_element_type=F32)
        else:
            pv = jnp.dot(e.astype(BF16), v, preferred_element_type=F32)
            l = jnp.broadcast_to(jnp.sum(e, axis=-1, keepdims=True), pv.shape)
            acc = acc + jnp.concatenate([pv, l], axis=1)
    return m_new, acc


def _attn_prompt_kernel(lam_ref, q_ref, k_ref, v_ref, near_ref, sub_ref, o_ref, *, t, nq, lam_init):
    hb = t // 2
    near1 = near_ref[1]
    bias_a = _tile_bias(near_ref, t, hb)
    bias_b = _tile_bias(near_ref, hb, hb)

    def both(s, b):
        n = s.shape[0] // 2
        return jnp.concatenate([s[:n] + b, s[n:] + b], axis=0)

    def sub_fix(sr):
        return jnp.concatenate([sr[:, :t - CHUNK], sr[:, t - CHUNK:] + near1], axis=1)

    def second_half(x):
        return jnp.concatenate([x[hb:t], x[t + hb:]], axis=0)

    def put_second_half(x, xb):
        return jnp.concatenate([x[:hb], xb[:hb], x[t:t + hb], xb[hb:]], axis=0)

    for qi in range(nq):
        q0 = qi * t
        qq = _split_components(q_ref[q0:q0 + t, :])
        m = jnp.full((2 * t, LANES), -jnp.inf, F32)
        acc = jnp.zeros((2 * t, 2 * LANES), F32)
        for kt in range(qi):
            keys = slice(kt * t, (kt + 1) * t)
            s = _nt_dot(qq, k_ref[keys, :])
            if kt == qi - 1:
                s = jnp.concatenate([sub_fix(s[:CHUNK]), s[CHUNK:t], sub_fix(s[t:t + CHUNK]), s[t + CHUNK:]], axis=0)
            m, acc = _softmax_update(m, acc, [s], [v_ref[keys, :]], True)
        keys = slice(q0, q0 + hb)
        m, acc = _softmax_update(m, acc, [both(_nt_dot(qq, k_ref[keys, :]), bias_a)], [v_ref[keys, :]], True)
        keys = slice(q0 + hb, q0 + t)
        s = both(_nt_dot(second_half(qq), k_ref[keys, :]), bias_b)
        _, acc_b = _softmax_update(second_half(m), second_half(acc), [s], [v_ref[keys, :]], True)
        acc = put_second_half(acc, acc_b)
        o = acc[:, :DA_DV] / acc[:, DA_DV:]
        out = _diff_out(o[:t], o[t:], lam_ref[0:1, :], sub_ref[...], lam_init)
        o_ref[q0:q0 + t, :] = out.astype(o_ref.dtype)


def _attn_prompt(z, lam, near, subln, B, S, t, lam_init):
    assert S % t == 0 and t % (2 * CHUNK) == 0 and MAX_DISTANCE <= CHUNK + 1
    H = DA_HEADS
    return pl.pallas_call(
        functools.partial(_attn_prompt_kernel, t=t, nq=S // t, lam_init=lam_init),
        grid=(B, H),
        in_specs=[
            pl.BlockSpec((8, LANES), lambda b, h: (0, 0)),
            pl.BlockSpec((S, DA_DV), lambda b, h: (b, h)),
            pl.BlockSpec((S, DA_DV), lambda b, h: (b, H + h)),
            pl.BlockSpec((S, DA_DV), lambda b, h: (b, 2 * H + h)),
            pl.BlockSpec((None, 2, CHUNK, CHUNK), lambda b, h: (h, 0, 0, 0)),
            pl.BlockSpec((1, DA_DV), lambda b, h: (0, 0)),
        ],
        out_specs=pl.BlockSpec((S, DA_DV), lambda b, h: (b, h)),
        out_shape=jax.ShapeDtypeStruct((B * S, H * DA_DV), BF16),
        compiler_params=_cparams("parallel", "arbitrary"),
        name="attn_prompt",
    )(lam, z, z, z, near, subln)


def _attn_sample_kernel(pt_ref, lam_ref, q_ref, kn_ref, vn_ref, far_ref, last_ref, new_ref, sub_ref, *rest,
                        spb, pps, n_tok, page, lam_init):
    n_in = spb * pps
    kp_refs, vp_refs = rest[:n_in], rest[n_in:2 * n_in]
    o_ref, qq_ref, m_ref, acc_ref = rest[2 * n_in:]
    p = pl.program_id(1)
    last = pl.num_programs(1) - 1
    hr = 2 * n_tok
    flat = page * DA_HEADS

    @pl.when(p == 0)
    def _():
        for j in range(spb):
            q = q_ref[j * n_tok:(j + 1) * n_tok, :]
            for h in range(DA_HEADS):
                qq_ref[j, h * hr:(h + 1) * hr, :] = _split_components(q[:, h * DA_DV:(h + 1) * DA_DV]).astype(BF16)
        m_ref[...] = jnp.full(m_ref.shape, -jnp.inf, F32)
        acc_ref[...] = jnp.zeros(acc_ref.shape, F32)

    far = far_ref[...]
    tail = jnp.where(p == last, last_ref[...], far)
    for j in range(spb):
        qq = qq_ref[j]
        s_list, v_list = [], []
        for r in range(pps):
            s = _nt_dot(qq, kp_refs[j * pps + r][...].reshape(flat, DA_DV).astype(BF16))
            s_list.append(s + (tail if r == pps - 1 else far))
            v_list.append(vp_refs[j * pps + r][...].reshape(flat, DA_DV).astype(BF16))
        m, acc = _softmax_update(m_ref[j], acc_ref[j], s_list, v_list, False)
        m_ref[j] = m
        acc_ref[j] = acc

    @pl.when(p == last)
    def _():
        pad = jnp.zeros((page - n_tok * DA_HEADS, DA_DV), F32)
        for j in range(spb):
            toks = slice(j * n_tok, (j + 1) * n_tok)

            def new_rows(ref):
                return jnp.concatenate([ref[toks].reshape(n_tok * DA_HEADS, DA_DV), pad], axis=0).astype(BF16)

            s_new = _nt_dot(qq_ref[j], new_rows(kn_ref)) + new_ref[...]
            _, acc2 = _softmax_update(m_ref[j], acc_ref[j], [s_new], [new_rows(vn_ref)], False)
            o = acc2[:, :DA_DV] / acc2[:, DA_DV:]
            for h in range(DA_HEADS):
                oh = o[h * hr:(h + 1) * hr]
                out = _diff_out(oh[:n_tok], oh[n_tok:], lam_ref[0:1, :], sub_ref[...], lam_init)
                o_ref[toks, h * DA_DV:(h + 1) * DA_DV] = out.astype(o_ref.dtype)


def _attn_sample(z, k_new, v_new, page_idx, cache_k, cache_v, lam, far, last, new, subln, N, n_tok, lam_init, spb, pps):
    n_pages = page_idx.shape[1]
    page = cache_k.shape[1]
    width = DA_HEADS * DA_DV
    rows = DA_HEADS * 2 * n_tok
    assert (2 * n_tok) % BF16_SUBLANES == 0 and n_pages % pps == 0 and N % spb == 0

    def const(shape):
        return pl.BlockSpec(shape, lambda n, p, pt: (0,) * len(shape))

    def page_spec(j, r):
        return pl.BlockSpec((None, page, DA_HEADS, DA_DV), lambda n, p, pt: (pt[n * spb + j, p * pps + r], 0, 0, 0))

    page_specs = [page_spec(j, r) for j in range(spb) for r in range(pps)]
    new_spec = pl.BlockSpec((spb * n_tok, DA_HEADS, DA_DV), lambda n, p, pt: (n, 0, 0))
    grid_spec = pltpu.PrefetchScalarGridSpec(
        num_scalar_prefetch=1,
        grid=(N // spb, n_pages // pps),
        in_specs=[
            const((8, LANES)),
            pl.BlockSpec((spb * n_tok, width), lambda n, p, pt: (n, 0)),
            new_spec, new_spec,
            const((rows, page * DA_HEADS)), const((rows, page * DA_HEADS)), const((rows, page)),
            const((1, DA_DV)),
        ] + page_specs + page_specs,
        out_specs=pl.BlockSpec((spb * n_tok, width), lambda n, p, pt: (n, 0)),
        scratch_shapes=[pltpu.VMEM((spb, rows, DA_DV), BF16), pltpu.VMEM((spb, rows, LANES), F32),
                        pltpu.VMEM((spb, rows, 2 * LANES), F32)],
    )
    n_in = spb * pps
    return pl.pallas_call(
        functools.partial(_attn_sample_kernel, spb=spb, pps=pps, n_tok=n_tok, page=page, lam_init=lam_init),
        grid_spec=grid_spec,
        out_shape=jax.ShapeDtypeStruct((N * n_tok, width), F32),
        compiler_params=_cparams("parallel", "arbitrary"),
        name="attn_sample",
    )(page_idx, lam, z, k_new, v_new, far, last, new, subln, *([cache_k] * n_in), *([cache_v] * n_in))


def _sgate_kernel(u_ref, vn_ref, w_ref, b_ref, o_ref, *, seg, n_chunk):
    i = lax.broadcasted_iota(jnp.int32, (CHUNK, CHUNK), 0)
    j = lax.broadcasted_iota(jnp.int32, (CHUNK, CHUNK), 1)
    keep = (_div_pow2(i, seg) == _div_pow2(j, seg)) & (j <= i)
    gd = u_ref.shape[1] // SG_GROUPS
    for g in range(SG_GROUPS):
        w = jnp.where(keep, w_ref[g], 0.0).astype(BF16)
        cols = slice(g * gd, (g + 1) * gd)
        for c in range(n_chunk):
            rws = slice(c * CHUNK, (c + 1) * CHUNK)
            mix = jnp.dot(w, vn_ref[rws, cols].astype(BF16), preferred_element_type=F32) + b_ref[:, cols]
            o_ref[rws, cols] = (u_ref[rws, cols].astype(F32) * mix).astype(o_ref.dtype)


def _sgate(z, w, b, u_sec, vn_sec, n_chunk):
    T = z.shape[0]
    width = b.shape[1]
    rb = n_chunk * CHUNK
    return pl.pallas_call(
        functools.partial(_sgate_kernel, seg=CHUNK, n_chunk=n_chunk),
        grid=(T // rb,),
        in_specs=[
            pl.BlockSpec((rb, width), lambda i: (i, u_sec)),
            pl.BlockSpec((rb, width), lambda i: (i, vn_sec)),
            pl.BlockSpec((SG_GROUPS, CHUNK, CHUNK), lambda i: (0, 0, 0)),
            pl.BlockSpec((CHUNK, width), lambda i: (0, 0)),
        ],
        out_specs=pl.BlockSpec((rb, width), lambda i: (i, 0)),
        out_shape=jax.ShapeDtypeStruct((T, width), BF16),
        compiler_params=_cparams("parallel"),
        name="sgate",
    )(z, z, w, b)


def _sgate_seq_kernel(w_ref, b_ref, u_ref, vn_ref, o_ref, wbd_ref, bias_ref, *, seg):
    gd = u_ref.shape[1] // SG_GROUPS

    @pl.when(pl.program_id(0) == 0)
    def _():
        i = lax.broadcasted_iota(jnp.int32, (CHUNK, CHUNK), 0)
        j = lax.broadcasted_iota(jnp.int32, (CHUNK, CHUNK), 1)
        same = _div_pow2(i, seg) == _div_pow2(j, seg)
        ti, tj = i & (seg - 1), j & (seg - 1)
        for g in range(SG_GROUPS):
            w = jnp.zeros((CHUNK, CHUNK), F32)
            bias = jnp.zeros((CHUNK, gd), F32)
            for a in range(seg):
                bias = jnp.where(ti == a, b_ref[g, a], bias)
                for b in range(a + 1):
                    w = jnp.where((ti == a) & (tj == b), w_ref[g * seg + a, b], w)
            wbd_ref[g] = jnp.where(same, w, 0.0).astype(BF16)
            bias_ref[:, g * gd:(g + 1) * gd] = bias

    for g in range(SG_GROUPS):
        cols = slice(g * gd, (g + 1) * gd)
        mix = jnp.dot(wbd_ref[g], vn_ref[:, cols].astype(BF16), preferred_element_type=F32) + bias_ref[:, cols]
        o_ref[:, cols] = (u_ref[:, cols].astype(F32) * mix).astype(o_ref.dtype)


def _sgate_seq(z, w, b, seg, u_sec, vn_sec):
    T = z.shape[0]
    width = SG_GROUPS * CHUNK
    assert CHUNK % seg == 0 and seg & (seg - 1) == 0 and T % CHUNK == 0
    return pl.pallas_call(
        functools.partial(_sgate_seq_kernel, seg=seg),
        grid=(T // CHUNK,),
        in_specs=[
            pl.BlockSpec(memory_space=pltpu.SMEM), pl.BlockSpec(memory_space=pltpu.SMEM),
            pl.BlockSpec((CHUNK, width), lambda i: (i, u_sec)),
            pl.BlockSpec((CHUNK, width), lambda i: (i, vn_sec)),
        ],
        out_specs=pl.BlockSpec((CHUNK, width), lambda i: (i, 0)),
        out_shape=jax.ShapeDtypeStruct((T, width), BF16),
        scratch_shapes=[pltpu.VMEM((SG_GROUPS, CHUNK, CHUNK), BF16), pltpu.VMEM((CHUNK, width), F32)],
        compiler_params=_cparams("arbitrary"),
        name="sgate_seq",
    )(w[:, :seg, :seg].reshape(SG_GROUPS * seg, seg), b[:, :seg], z, z)


def _xattn_prompt_kernel(q_ref, k_ref, v_ref, o_ref):
    heads = [slice(h * MX_DH, (h + 1) * MX_DH) for h in range(MX_HEADS)]
    scores = [_nt_dot(q_ref[:, cols], k_ref[:, cols]) for cols in heads]
    exps = [jnp.exp(s - jnp.max(s, axis=-1, keepdims=True)) for s in scores]
    for cols, e in zip(heads, exps):
        o = jnp.dot(e.astype(BF16), v_ref[:, cols], preferred_element_type=F32)
        o_ref[:, cols] = (o / jnp.sum(e, axis=-1, keepdims=True)).astype(o_ref.dtype)


def _xattn_prompt(z, mem_kv, B, S, tq, q_sec):
    nq = S // tq
    n_mem = mem_kv.shape[0] // B
    width = MX_HEADS * MX_DH
    return pl.pallas_call(
        _xattn_prompt_kernel,
        grid=(B, nq),
        in_specs=[
            pl.BlockSpec((tq, width), lambda b, i: (b * nq + i, q_sec)),
            pl.BlockSpec((n_mem, width), lambda b, i: (b, 0)),
            pl.BlockSpec((n_mem, width), lambda b, i: (b, 1)),
        ],
        out_specs=pl.BlockSpec((tq, width), lambda b, i: (b * nq + i, 0)),
        out_shape=jax.ShapeDtypeStruct((B * S, width), BF16),
        compiler_params=_cparams("parallel", "arbitrary"),
        name="xattn_prompt",
    )(z, mem_kv, mem_kv)


def _xattn_sample_kernel(q_ref, k_ref, v_ref, o_ref, *, spb, n_tok):
    n_mem = k_ref.shape[1]
    rows, flat = MX_HEADS * n_tok, n_mem * MX_HEADS
    r = lax.broadcasted_iota(jnp.int32, (rows, flat), 0)
    c = lax.broadcasted_iota(jnp.int32, (rows, flat), 1)
    own = (c & (MX_HEADS - 1)) == _div_pow2(r, n_tok)
    for j in range(spb):
        toks = slice(j * n_tok, (j + 1) * n_tok)
        q = q_ref[toks, :]
        qq = jnp.concatenate([q[:, h * MX_DH:(h + 1) * MX_DH] for h in range(MX_HEADS)], axis=0).astype(BF16)
        s = jnp.where(own, _nt_dot(qq, k_ref[j].reshape(flat, MX_DH).astype(BF16)), NEG_INF)
        e = jnp.exp(s - jnp.max(s, axis=-1, keepdims=True))
        o = jnp.dot(e.astype(BF16), v_ref[j].reshape(flat, MX_DH).astype(BF16), preferred_element_type=F32)
        o = o / jnp.sum(e, axis=-1, keepdims=True)
        for h in range(MX_HEADS):
            o_ref[toks, h * MX_DH:(h + 1) * MX_DH] = o[h * n_tok:(h + 1) * n_tok].astype(o_ref.dtype)


def _xattn_sample(z, mem_k, mem_v, N, n_tok, q_sec, seq_off, spb):
    n_mem = mem_k.shape[1]
    width = MX_HEADS * MX_DH
    assert (MX_HEADS * n_tok) % BF16_SUBLANES == 0 and N % spb == 0 and seq_off % spb == 0
    mem_spec = pl.BlockSpec((spb, n_mem, MX_HEADS, MX_DH), lambda n: (seq_off // spb + n, 0, 0, 0))
    return pl.pallas_call(
        functools.partial(_xattn_sample_kernel, spb=spb, n_tok=n_tok),
        grid=(N // spb,),
        in_specs=[pl.BlockSpec((spb * n_tok, width), lambda n: (n, q_sec)), mem_spec, mem_spec],
        out_specs=pl.BlockSpec((spb * n_tok, width), lambda n: (n, 0)),
        out_shape=jax.ShapeDtypeStruct((N * n_tok, width), F32),
        compiler_params=_cparams("parallel"),
        name="xattn_sample",
    )(z, mem_k, mem_v)


def _post_kernel(x_ref, da_ref, sg_ref, mx_ref, gate_ref, wb_ref, wo_ref, gf_ref, wu_ref, wd_ref, gfin_ref,
                 o_ref, y_ref, *, d, d_ff, fc, n_sub, final_norm):
    branches = (da_ref, sg_ref, mx_ref)
    rs = x_ref.shape[0] // n_sub
    subs = [slice(sb * rs, (sb + 1) * rs) for sb in range(n_sub)]
    merged = [None] * n_sub
    for k in range(N_BRANCH):
        for i, rws in enumerate(subs):
            br = jnp.dot(branches[k][rws, :].astype(BF16), wb_ref[k * d:(k + 1) * d, :], preferred_element_type=F32)
            term = gate_ref[rws, k * d:(k + 1) * d].astype(F32) * br
            merged[i] = term if merged[i] is None else merged[i] + term
    h2 = []
    for i, rws in enumerate(subs):
        x1 = x_ref[rws, :] + jnp.dot(merged[i].astype(BF16), wo_ref[...], preferred_element_type=F32)
        y_ref[rws, :] = x1
        h2.append(_rms(x1, gf_ref[...]).astype(BF16))
    for c in range(d_ff // fc):
        up = [(jnp.dot(h, wu_ref[:, c * fc:(c + 1) * fc], preferred_element_type=F32),
               jnp.dot(h, wu_ref[:, d_ff + c * fc:d_ff + (c + 1) * fc], preferred_element_type=F32)) for h in h2]
        for (a, b), rws in zip(up, subs):
            act = (jax.nn.silu(a) * b).astype(BF16)
            y_ref[rws, :] += jnp.dot(act, wd_ref[c * fc:(c + 1) * fc, :], preferred_element_type=F32)
    for rws in subs:
        y = y_ref[rws, :]
        o_ref[rws, :] = _rms(y, gfin_ref[...]) if final_norm else y


def _post(x, o_da, o_sg, o_mx, z, gate_sec, wb, wo, g_ffn, wu, wd, g_final, tm, final_norm):
    T, d = x.shape
    d_ff = wd.shape[0]
    fc = 256
    n_sub = 2 if tm >= 512 else 1
    row = lambda i: (i, 0)
    whole = lambda i: (0, 0)

    def resident(shape):
        return pl.BlockSpec(shape, whole, pipeline_mode=pl.Buffered(1))

    return pl.pallas_call(
        functools.partial(_post_kernel, d=d, d_ff=d_ff, fc=fc, n_sub=n_sub, final_norm=final_norm),
        grid=(T // tm,),
        in_specs=[
            pl.BlockSpec((tm, d), row), pl.BlockSpec((tm, d), row), pl.BlockSpec((tm, d), row), pl.BlockSpec((tm, d), row),
            pl.BlockSpec((tm, N_BRANCH * d), lambda i: (i, gate_sec)),
            resident(wb.shape), resident(wo.shape), resident((1, d)), resident(wu.shape), resident(wd.shape),
            resident((1, d)),
        ],
        out_specs=pl.BlockSpec((tm, d), row),
        out_shape=jax.ShapeDtypeStruct((T, d), F32),
        scratch_shapes=[pltpu.VMEM((tm, d), F32)],
        compiler_params=_cparams("parallel"),
        name="post",
    )(x, o_da, o_sg, o_mx, z, wb, wo, g_ffn, wu, wd, g_final)


_IN_SECTIONS = ("q", "k", "v", "u", "s", "m", "g0", "g1", "g2")
_IN_ACTS = {
    "q": ("scale", DA_DK ** -0.5 * LOG2E), "k": ("none",), "v": ("none",), "u": ("gelu",), "s": ("gelu_rms",),
    "m": ("scale", MX_DH ** -0.5), "g0": ("sigmoid",), "g1": ("sigmoid",), "g2": ("sigmoid",),
}
_IN_COPIES = {"k": 0, "v": 1}


def kernel(x_prompt, x_sample, mem_prompt, cache_da_k, cache_da_v, cache_mem_k, cache_mem_v, page_table, g_attn, w_in,
           da_lam, da_subln, rel_bias, sg_norm, sg_w, sg_b, g_mem, w_mem_kv, w_branch, w_out, g_ffn, w_up, w_down,
           g_final):
    B, S, D = x_prompt.shape
    N, n_tok, _ = x_sample.shape
    depth, n_pool, page = cache_da_k.shape[:3]
    n_mem = mem_prompt.shape[1]
    n_pages = page_table.shape[1]
    width = DA_HEADS * DA_DV
    assert D == width == SG_GROUPS * CHUNK == MX_HEADS * MX_DH and page == CHUNK and n_tok <= CHUNK

    t_attn = min(512, S)
    pps = max(d for d in range(1, MAX_PAGES_PER_STEP + 1) if n_pages % d == 0)
    spb_da = 1
    spb_mx = 4 if N % 4 == 0 else 1
    tm_p = 256
    tm_s = min(256, N * n_tok)
    tm_post = 512
    sec = {n: i for i, n in enumerate(_IN_SECTIONS)}
    in_sections = [(_IN_ACTS[n], _IN_COPIES.get(n)) for n in _IN_SECTIONS]

    xp = x_prompt.reshape(B * S, D)
    xs = x_sample.reshape(N * n_tok, D)
    mem = mem_prompt.reshape(B * n_mem, D)
    ck = cache_da_k.reshape(depth * n_pool, page, DA_HEADS, DA_DV)
    cv = cache_da_v.reshape(depth * n_pool, page, DA_HEADS, DA_DV)
    cmk = cache_mem_k.reshape(depth * N, n_mem, MX_HEADS, MX_DH)
    cmv = cache_mem_v.reshape(depth * N, n_mem, MX_HEADS, MX_DH)
    row = lambda a: a.reshape(1, -1)

    near_bias = _prompt_bias(rel_bias)
    seg_s = n_tok
    outs = {k: [] for k in ("dkp", "dvp", "dks", "dvs", "mkp", "mvp", "sgs")}
    for l in range(depth):
        lam_init = 0.8 - 0.6 * math.exp(-0.3 * l)
        w_in_l = w_in[l].astype(BF16)
        wb_l, wo_l = w_branch[l].astype(BF16), w_out[l].astype(BF16)
        wu_l, wd_l = w_up[l].astype(BF16), w_down[l].astype(BF16)
        far, last, new, lam = _sample_bias(rel_bias, da_lam[l], n_tok, page, lam_init)
        subln = row(da_subln[l])
        sgb_p = jnp.repeat(sg_b[l].T, CHUNK, axis=1)

        zp, kp32, vp32 = _norm_matmul(xp, row(g_attn[l]), w_in_l, row(sg_norm[l]), in_sections, BF16, tm_p,
                                      "inproj_prompt")
        oda_p = _attn_prompt(zp, lam, near_bias, subln, B, S, t_attn, lam_init)
        oda_p, zp, mem_l, xs = lax.optimization_barrier((oda_p, zp, mem, xs))
        mkv16, mk32, mv32 = _norm_matmul(mem_l, row(g_mem[l]), w_mem_kv[l].astype(BF16), row(sg_norm[l]),
                                         [(("none",), 0), (("none",), 1)], BF16, min(512, B * n_mem), "mem_kv",
                                         copy_heads=MX_HEADS)
        osg_p = _sgate(zp, sg_w[l], sgb_p, sec["u"], sec["s"], 8)
        omx_p = _xattn_prompt(zp, mkv16, B, S, 1024 if S % 1024 == 0 else 512, sec["m"])
        xp = _post(xp, oda_p, osg_p, omx_p, zp, sec["g0"] // N_BRANCH, wb_l, wo_l, row(g_ffn[l]), wu_l, wd_l,
                   row(g_final), tm_post, l == depth - 1)

        zs, ks32, vs32 = _norm_matmul(xs, row(g_attn[l]), w_in_l, row(sg_norm[l]), in_sections, F32, tm_s,
                                      "inproj_sample", copy_heads=DA_HEADS)
        oda_s = _attn_sample(zs, ks32, vs32, page_table + l * n_pool, ck, cv, lam, far, last, new, subln, N, n_tok,
                             lam_init, spb_da, pps)
        osg_s = _sgate_seq(zs, sg_w[l], sg_b[l], seg_s, sec["u"], sec["s"])
        omx_s = _xattn_sample(zs, cmk, cmv, N, n_tok, sec["m"], l * N, spb_mx)
        xs = _post(xs, oda_s, osg_s, omx_s, zs, sec["g0"] // N_BRANCH, wb_l, wo_l, row(g_ffn[l]), wu_l, wd_l,
                   row(g_final), tm_s, l == depth - 1)

        outs["dkp"].append(kp32.reshape(B, S, DA_HEADS, 2 * DA_DK))
        outs["dvp"].append(vp32.reshape(B, S, DA_HEADS, DA_DV))
        outs["dks"].append(ks32.reshape(N, n_tok, DA_HEADS, 2 * DA_DK))
        outs["dvs"].append(vs32.reshape(N, n_tok, DA_HEADS, DA_DV))
        outs["mkp"].append(mk32.reshape(B, n_mem, MX_HEADS, MX_DH))
        outs["mvp"].append(mv32.reshape(B, n_mem, MX_HEADS, MX_DH))
        outs["sgs"].append(zs[:, sec["s"] * D:(sec["s"] + 1) * D].reshape(N, n_tok, D))

    return (xp.reshape(B, S, D), xs.reshape(N, n_tok, D), jnp.stack(outs["dkp"]), jnp.stack(outs["dvp"]),
            jnp.stack(outs["dks"]), jnp.stack(outs["dvs"]), jnp.stack(outs["mkp"]), jnp.stack(outs["mvp"]),
            jnp.stack(outs["sgs"]))
```

```python
import functools
import math

import jax
import jax.numpy as jnp
from jax import lax
from jax.experimental import pallas as pl
from jax.experimental.pallas import tpu as pltpu

F32 = jnp.float32
BF16 = jnp.bfloat16

DA_HEADS = 8
DA_DK = 64
DA_DV = 2 * DA_DK
SG_GROUPS = 8
CHUNK = 128
MX_HEADS = 4
MX_DH = 256
N_BUCKETS = 32
MAX_DISTANCE = 128
N_BRANCH = 3
RMS_EPS = 1e-6
NEG_INF = -1e30
LOG2E = math.log2(math.e)

LANES = 128
BF16_SUBLANES = 16
VMEM_LIMIT_BYTES = 56 * 1024 * 1024
MAX_PAGES_PER_STEP = 16


def _cparams(*sem):
    return pltpu.CompilerParams(dimension_semantics=sem, vmem_limit_bytes=VMEM_LIMIT_BYTES)


def _rms(x, g):
    ms = jnp.mean(x * x, axis=-1, keepdims=True)
    return x * lax.rsqrt(ms + RMS_EPS) * g


def _div_pow2(x, n):
    assert n > 0 and n & (n - 1) == 0, n
    return x >> (n.bit_length() - 1)


def _nt_dot(a, b):
    return lax.dot_general(a, b, (((1,), (1,)), ((), ())), preferred_element_type=F32)


def _lane_tile(x, n):
    return x if n == 1 else jnp.concatenate([x] * n, axis=1)


def _norm_matmul_kernel(x_ref, g_ref, w_ref, sgn_ref, z_ref, *copy_refs, sections, tn):
    hn = _rms(x_ref[...], g_ref[...]).astype(BF16)
    for sec, (act, copy_idx) in enumerate(sections):
        cols = slice(sec * tn, (sec + 1) * tn)
        acc = jnp.dot(hn, w_ref[:, cols], preferred_element_type=F32)
        if act[0] == "scale":
            val = acc * act[1]
        elif act[0] == "gelu":
            val = jax.nn.gelu(acc)
        elif act[0] == "gelu_rms":
            val = _rms(jax.nn.gelu(acc), sgn_ref[...])
        elif act[0] == "sigmoid":
            val = jax.nn.sigmoid(acc)
        else:
            val = acc
        z_ref[:, cols] = val.astype(z_ref.dtype)
        if copy_idx is not None:
            cref = copy_refs[copy_idx]
            if len(cref.shape) == 2:
                cref[...] = val
            else:
                hd = cref.shape[2]
                for hh in range(cref.shape[1]):
                    cref[:, hh, :] = val[:, hh * hd:(hh + 1) * hd]


def _norm_matmul(x, g, w, sgn, sections, z_dtype, tm, name, copy_heads=1):
    T, D = x.shape
    n_sec = len(sections)
    tn = w.shape[1] // n_sec
    n_copy = sum(1 for _, c in sections if c is not None)
    out_shape = [jax.ShapeDtypeStruct((T, n_sec * tn), z_dtype)]
    out_specs = [pl.BlockSpec((tm, n_sec * tn), lambda i: (i, 0))]
    for _ in range(n_copy):
        if copy_heads == 1:
            out_shape.append(jax.ShapeDtypeStruct((T, tn), F32))
            out_specs.append(pl.BlockSpec((tm, tn), lambda i: (i, 0)))
        else:
            out_shape.append(jax.ShapeDtypeStruct((T, copy_heads, tn // copy_heads), F32))
            out_specs.append(pl.BlockSpec((tm, copy_heads, tn // copy_heads), lambda i: (i, 0, 0)))

    def resident(shape):
        return pl.BlockSpec(shape, lambda i: (0, 0), pipeline_mode=pl.Buffered(1))

    return pl.pallas_call(
        functools.partial(_norm_matmul_kernel, sections=tuple(sections), tn=tn),
        grid=(T // tm,),
        in_specs=[pl.BlockSpec((tm, D), lambda i: (i, 0)), resident((1, D)), resident(w.shape), resident((1, tn))],
        out_specs=out_specs,
        out_shape=out_shape,
        compiler_params=_cparams("parallel"),
        name=name,
    )(x, g, w, sgn)


def _t5_bias(dist, rb_ref, h):
    n = jnp.maximum(dist, 0)
    max_exact = N_BUCKETS // 2
    nf = jnp.maximum(n, 1).astype(F32)
    rel = jnp.log(nf / max_exact) / math.log(MAX_DISTANCE / max_exact) * (N_BUCKETS - max_exact)
    out = jnp.zeros(dist.shape, F32)
    for b in range(max_exact):
        out = jnp.where(n == b, rb_ref[b, h], out)
    out = jnp.where(n >= max_exact, rb_ref[max_exact, h], out)
    for b in range(max_exact + 1, N_BUCKETS):
        out = jnp.where(rel >= b - max_exact, rb_ref[b, h], out)
    return jnp.where(dist >= 0, out * LOG2E, NEG_INF)


def _far_bias(rb_ref, h):
    return rb_ref[N_BUCKETS - 1, h] * LOG2E


def _prompt_bias_kernel(rb_ref, near_ref):
    d = lax.broadcasted_iota(jnp.int32, (CHUNK, CHUNK), 0) - lax.broadcasted_iota(jnp.int32, (CHUNK, CHUNK), 1)
    for h in range(near_ref.shape[0]):
        far = _far_bias(rb_ref, h)
        near_ref[h, 0] = _t5_bias(d, rb_ref, h) - far
        near_ref[h, 1] = _t5_bias(d + CHUNK, rb_ref, h) - far


def _prompt_bias(rel_bias):
    assert MAX_DISTANCE <= CHUNK + 1
    H = rel_bias.shape[1]
    return pl.pallas_call(
        _prompt_bias_kernel,
        in_specs=[pl.BlockSpec(memory_space=pltpu.SMEM)],
        out_shape=jax.ShapeDtypeStruct((H, 2, CHUNK, CHUNK), F32),
        name="prompt_bias",
    )(rel_bias)


def _tile_bias(near_ref, rows, cols):
    fill = {-1: jnp.full((CHUNK, CHUNK), NEG_INF, F32), 2: jnp.zeros((CHUNK, CHUNK), F32)}

    def block(bd):
        return near_ref[bd] if 0 <= bd < 2 else fill[max(min(bd, 2), -1)]

    return jnp.concatenate([jnp.concatenate([block(bi - bj) for bj in range(cols // CHUNK)], axis=1)
                            for bi in range(rows // CHUNK)], axis=0)


def _sample_bias_kernel(rb_ref, lam_p_ref, far_ref, last_ref, new_ref, lam_ref, *, n_tok, page, lam_init):
    hr = 2 * n_tok

    def grid(cols):
        tok = lax.broadcasted_iota(jnp.int32, (hr, cols), 0) & (n_tok - 1)
        col = lax.broadcasted_iota(jnp.int32, (hr, cols), 1)
        return tok, col, _div_pow2(col, DA_HEADS), col & (DA_HEADS - 1)

    tok, _, key, kh = grid(page * DA_HEADS)
    tok_n, col_n, key_n, kh_n = grid(page)
    for h in range(DA_HEADS):
        sl = slice(h * hr, (h + 1) * hr)
        far_ref[sl, :] = jnp.where(kh == h, _far_bias(rb_ref, h), NEG_INF)
        last_ref[sl, :] = jnp.where(kh == h, _t5_bias(page + tok - key, rb_ref, h), NEG_INF)
        new_ref[sl, :] = jnp.where((kh_n == h) & (col_n < n_tok * DA_HEADS), _t5_bias(tok_n - key_n, rb_ref, h), NEG_INF)
    lp = lam_p_ref[...]
    s1 = jnp.sum(lp[0:1, :] * lp[1:2, :], axis=-1, keepdims=True)
    s2 = jnp.sum(lp[2:3, :] * lp[3:4, :], axis=-1, keepdims=True)
    lam = jnp.exp(s1) - jnp.exp(s2) + lam_init
    lam_ref[...] = jnp.broadcast_to(lam, lam_ref.shape)


def _sample_bias(rel_bias, da_lam_l, n_tok, page, lam_init):
    assert n_tok & (n_tok - 1) == 0 and n_tok * DA_HEADS <= page
    rows = DA_HEADS * 2 * n_tok
    wide = jax.ShapeDtypeStruct((rows, page * DA_HEADS), F32)
    return pl.pallas_call(
        functools.partial(_sample_bias_kernel, n_tok=n_tok, page=page, lam_init=lam_init),
        in_specs=[pl.BlockSpec(memory_space=pltpu.SMEM), pl.BlockSpec(memory_space=pltpu.VMEM)],
        out_shape=[wide, wide, jax.ShapeDtypeStruct((rows, page), F32), jax.ShapeDtypeStruct((8, LANES), F32)],
        name="sample_bias",
    )(rel_bias, da_lam_l)


def _split_components(q):
    lane = lax.broadcasted_iota(jnp.int32, q.shape, 1)
    zero = jnp.zeros_like(q)
    return jnp.concatenate([jnp.where(lane < DA_DK, q, zero), jnp.where(lane >= DA_DK, q, zero)], axis=0)


def _diff_out(o1, o2, lam_row, subln, lam_init):
    o = o1 - lam_row * o2
    return _rms(o, subln) * (1.0 - lam_init)


def _softmax_update(m, acc, s_list, v_list, mxu_sums):
    m_new = m
    for s in s_list:
        m_new = jnp.maximum(m_new, jnp.max(s, axis=-1, keepdims=True))
    acc = acc * _lane_tile(jnp.exp2(m - m_new), 2)
    for s, v in zip(s_list, v_list):
        e = jnp.exp2(s - _lane_tile(m_new, s.shape[1] // LANES))
        if mxu_sums:
            v1 = jnp.concatenate([v, jnp.ones(v.shape, BF16)], axis=1)
            acc = acc + jnp.dot(e.astype(BF16), v1, preferred_element_type=F32)
        else:
            pv = jnp.dot(e.astype(BF16), v, preferred_element_type=F32)
            l = jnp.broadcast_to(jnp.sum(e, axis=-1, keepdims=True), pv.shape)
            acc = acc + jnp.concatenate([pv, l], axis=1)
    return m_new, acc


def _attn_prompt_kernel(lam_ref, q_ref, k_ref, v_ref, near_ref, sub_ref, o_ref, *, t, nq, lam_init):
    _prompt_tiles(lam_ref, q_ref, k_ref, v_ref, near_ref, sub_ref, o_ref, t, range(nq), lam_init)


def _prompt_tiles(lam_ref, q_ref, k_ref, v_ref, near_ref, sub_ref, o_ref, t, tiles, lam_init):
    hb = t // 2
    near1 = near_ref[1]
    bias_a = _tile_bias(near_ref, t, hb)
    bias_b = _tile_bias(near_ref, hb, hb)

    def both(s, b):
        n = s.shape[0] // 2
        return jnp.concatenate([s[:n] + b, s[n:] + b], axis=0)

    def sub_fix(sr):
        return jnp.concatenate([sr[:, :t - CHUNK], sr[:, t - CHUNK:] + near1], axis=1)

    def second_half(x):
        return jnp.concatenate([x[hb:t], x[t + hb:]], axis=0)

    def put_second_half(x, xb):
        return jnp.concatenate([x[:hb], xb[:hb], x[t:t + hb], xb[hb:]], axis=0)

    for qi in tiles:
        q0 = qi * t
        r0 = (qi - tiles[0]) * t
        qq = _split_components(q_ref[r0:r0 + t, :])
        m = jnp.full((2 * t, LANES), -jnp.inf, F32)
        acc = jnp.zeros((2 * t, 2 * LANES), F32)
        for kt in range(qi):
            keys = slice(kt * t, (kt + 1) * t)
            s = _nt_dot(qq, k_ref[keys, :])
            if kt == qi - 1:
                s = jnp.concatenate([sub_fix(s[:CHUNK]), s[CHUNK:t], sub_fix(s[t:t + CHUNK]), s[t + CHUNK:]], axis=0)
            m, acc = _softmax_update(m, acc, [s], [v_ref[keys, :]], True)
        keys = slice(q0, q0 + hb)
        m, acc = _softmax_update(m, acc, [both(_nt_dot(qq, k_ref[keys, :]), bias_a)], [v_ref[keys, :]], True)
        keys = slice(q0 + hb, q0 + t)
        s = both(_nt_dot(second_half(qq), k_ref[keys, :]), bias_b)
        _, acc_b = _softmax_update(second_half(m), second_half(acc), [s], [v_ref[keys, :]], True)
        acc = put_second_half(acc, acc_b)
        o = acc[:, :DA_DV] / acc[:, DA_DV:]
        out = _diff_out(o[:t], o[t:], lam_ref[0:1, :], sub_ref[...], lam_init)
        o_ref[r0:r0 + t, :] = out.astype(o_ref.dtype)


def _attn_prompt(z, lam, near, subln, B, S, t, lam_init):
    assert S % t == 0 and t % (2 * CHUNK) == 0 and MAX_DISTANCE <= CHUNK + 1
    H = DA_HEADS
    return pl.pallas_call(
        functools.partial(_attn_prompt_kernel, t=t, nq=S // t, lam_init=lam_init),
        grid=(B, H),
        in_specs=[
            pl.BlockSpec((8, LANES), lambda b, h: (0, 0)),
            pl.BlockSpec((S, DA_DV), lambda b, h: (b, h)),
            pl.BlockSpec((S, DA_DV), lambda b, h: (b, H + h)),
            pl.BlockSpec((S, DA_DV), lambda b, h: (b, 2 * H + h)),
            pl.BlockSpec((None, 2, CHUNK, CHUNK), lambda b, h: (h, 0, 0, 0)),
            pl.BlockSpec((1, DA_DV), lambda b, h: (0, 0)),
        ],
        out_specs=pl.BlockSpec((S, DA_DV), lambda b, h: (b, h)),
        out_shape=jax.ShapeDtypeStruct((B * S, H * DA_DV), BF16),
        compiler_params=_cparams("parallel", "arbitrary"),
        name="attn_prompt",
    )(lam, z, z, z, near, subln)


def _attn_sample_kernel(pt_ref, lam_ref, q_ref, kn_ref, vn_ref, far_ref, last_ref, new_ref, sub_ref, *rest,
                        spb, pps, n_tok, page, lam_init):
    n_in = spb * pps
    kp_refs, vp_refs = rest[:n_in], rest[n_in:2 * n_in]
    o_ref, qq_ref, m_ref, acc_ref = rest[2 * n_in:]
    p = pl.program_id(1)
    last = pl.num_programs(1) - 1
    hr = 2 * n_tok
    flat = page * DA_HEADS

    @pl.when(p == 0)
    def _():
        for j in range(spb):
            q = q_ref[j * n_tok:(j + 1) * n_tok, :]
            for h in range(DA_HEADS):
                qq_ref[j, h * hr:(h + 1) * hr, :] = _split_components(q[:, h * DA_DV:(h + 1) * DA_DV]).astype(BF16)
        m_ref[...] = jnp.full(m_ref.shape, -jnp.inf, F32)
        acc_ref[...] = jnp.zeros(acc_ref.shape, F32)

    far = far_ref[...]
    tail = jnp.where(p == last, last_ref[...], far)
    for j in range(spb):
        qq = qq_ref[j]
        s_list, v_list = [], []
        for r in range(pps):
            s = _nt_dot(qq, kp_refs[j * pps + r][...].reshape(flat, DA_DV).astype(BF16))
            s_list.append(s + (tail if r == pps - 1 else far))
            v_list.append(vp_refs[j * pps + r][...].reshape(flat, DA_DV).astype(BF16))
        m, acc = _softmax_update(m_ref[j], acc_ref[j], s_list, v_list, False)
        m_ref[j] = m
        acc_ref[j] = acc

    @pl.when(p == last)
    def _():
        pad = jnp.zeros((page - n_tok * DA_HEADS, DA_DV), F32)
        for j in range(spb):
            toks = slice(j * n_tok, (j + 1) * n_tok)

            def new_rows(ref):
                return jnp.concatenate([ref[toks].reshape(n_tok * DA_HEADS, DA_DV), pad], axis=0).astype(BF16)

            s_new = _nt_dot(qq_ref[j], new_rows(kn_ref)) + new_ref[...]
            _, acc2 = _softmax_update(m_ref[j], acc_ref[j], [s_new], [new_rows(vn_ref)], False)
            o = acc2[:, :DA_DV] / acc2[:, DA_DV:]
            for h in range(DA_HEADS):
                oh = o[h * hr:(h + 1) * hr]
                out = _diff_out(oh[:n_tok], oh[n_tok:], lam_ref[0:1, :], sub_ref[...], lam_init)
                o_ref[toks, h * DA_DV:(h + 1) * DA_DV] = out.astype(o_ref.dtype)


def _attn_sample(z, k_new, v_new, page_idx, cache_k, cache_v, lam, far, last, new, subln, N, n_tok, lam_init, spb, pps):
    n_pages = page_idx.shape[1]
    page = cache_k.shape[1]
    width = DA_HEADS * DA_DV
    rows = DA_HEADS * 2 * n_tok
    assert (2 * n_tok) % BF16_SUBLANES == 0 and n_pages % pps == 0 and N % spb == 0

    def const(shape):
        return pl.BlockSpec(shape, lambda n, p, pt: (0,) * len(shape))

    def page_spec(j, r):
        return pl.BlockSpec((None, page, DA_HEADS, DA_DV), lambda n, p, pt: (pt[n * spb + j, p * pps + r], 0, 0, 0))

    page_specs = [page_spec(j, r) for j in range(spb) for r in range(pps)]
    new_spec = pl.BlockSpec((spb * n_tok, DA_HEADS, DA_DV), lambda n, p, pt: (n, 0, 0))
    grid_spec = pltpu.PrefetchScalarGridSpec(
        num_scalar_prefetch=1,
        grid=(N // spb, n_pages // pps),
        in_specs=[
            const((8, LANES)),
            pl.BlockSpec((spb * n_tok, width), lambda n, p, pt: (n, 0)),
            new_spec, new_spec,
            const((rows, page * DA_HEADS)), const((rows, page * DA_HEADS)), const((rows, page)),
            const((1, DA_DV)),
        ] + page_specs + page_specs,
        out_specs=pl.BlockSpec((spb * n_tok, width), lambda n, p, pt: (n, 0)),
        scratch_shapes=[pltpu.VMEM((spb, rows, DA_DV), BF16), pltpu.VMEM((spb, rows, LANES), F32),
                        pltpu.VMEM((spb, rows, 2 * LANES), F32)],
    )
    n_in = spb * pps
    return pl.pallas_call(
        functools.partial(_attn_sample_kernel, spb=spb, pps=pps, n_tok=n_tok, page=page, lam_init=lam_init),
        grid_spec=grid_spec,
        out_shape=jax.ShapeDtypeStruct((N * n_tok, width), F32),
        compiler_params=_cparams("parallel", "arbitrary"),
        name="attn_sample",
    )(page_idx, lam, z, k_new, v_new, far, last, new, subln, *([cache_k] * n_in), *([cache_v] * n_in))


def _sample_seq(lam_ref, q_ref, kn_ref, vn_ref, far_ref, last_ref, new_ref, sub_ref, kp_refs, vp_refs, o_ref,
                n_tok, page, lam_init):
    hr = 2 * n_tok
    flat = page * DA_HEADS
    q = q_ref[...]
    qq = jnp.concatenate([_split_components(q[:, h * DA_DV:(h + 1) * DA_DV]) for h in range(DA_HEADS)],
                         axis=0).astype(BF16)
    far = far_ref[...]
    s_list, v_list = [], []
    for r, (kp_ref, vp_ref) in enumerate(zip(kp_refs, vp_refs)):
        s = _nt_dot(qq, kp_ref[...].reshape(flat, DA_DV).astype(BF16))
        s_list.append(s + (last_ref[...] if r == len(kp_refs) - 1 else far))
        v_list.append(vp_ref[...].reshape(flat, DA_DV).astype(BF16))
    pad = jnp.zeros((page - n_tok * DA_HEADS, DA_DV), F32)

    def new_rows(ref):
        return jnp.concatenate([ref[...].reshape(n_tok * DA_HEADS, DA_DV), pad], axis=0).astype(BF16)

    s_list.append(_nt_dot(qq, new_rows(kn_ref)) + new_ref[...])
    v_list.append(new_rows(vn_ref))
    rows = DA_HEADS * hr
    _, acc = _softmax_update(jnp.full((rows, LANES), -jnp.inf, F32), jnp.zeros((rows, 2 * LANES), F32),
                             s_list, v_list, False)
    o = acc[:, :DA_DV] / acc[:, DA_DV:]
    for h in range(DA_HEADS):
        oh = o[h * hr:(h + 1) * hr]
        out = _diff_out(oh[:n_tok], oh[n_tok:], lam_ref[0:1, :], sub_ref[...], lam_init)
        o_ref[:, h * DA_DV:(h + 1) * DA_DV] = out.astype(o_ref.dtype)


def _attn_fused_kernel(pt_ref, lam_ref, sub_ref, q_ref, k_ref, v_ref, near_ref, qs_ref, kn_ref, vn_ref, far_ref,
                       last_ref, new_ref, *rest, parts, tpp, t, n_tok, page, lam_init):
    n_pages = (len(rest) - 2) // 2
    kp_refs, vp_refs = rest[:n_pages], rest[n_pages:2 * n_pages]
    op_ref, os_ref = rest[2 * n_pages:]

    def body(part):
        _prompt_tiles(lam_ref, q_ref, k_ref, v_ref, near_ref, sub_ref, op_ref, t,
                      range(part * tpp, (part + 1) * tpp), lam_init)
        _sample_seq(lam_ref, qs_ref, kn_ref, vn_ref, far_ref, last_ref, new_ref, sub_ref, kp_refs, vp_refs, os_ref,
                    n_tok, page, lam_init)

    if parts == 1:
        body(0)
    else:
        part = lax.rem(pl.program_id(0), parts)
        for p in range(parts):
            pl.when(part == p)(functools.partial(body, p))


def _attn_fused(zp, zs, k_new, v_new, page_idx, cache_k, cache_v, lam, near, far, last, new, subln, B, S, t, N,
                n_tok, lam_init):
    H = DA_HEADS
    n_pages = page_idx.shape[1]
    page = cache_k.shape[1]
    width = H * DA_DV
    rows = H * 2 * n_tok
    parts = N // (B * H)
    tpp = (S // t) // parts
    assert N == parts * B * H and S == parts * tpp * t and n_pages <= MAX_PAGES_PER_STEP
    assert t % (2 * CHUNK) == 0 and MAX_DISTANCE <= CHUNK + 1 and (2 * n_tok) % 8 == 0
    rows_p = tpp * t

    def const(shape):
        return pl.BlockSpec(shape, lambda i, pt: (0,) * len(shape))

    def b_of(i):
        return i // (parts * H)

    def h_of(i):
        return (i // parts) % H

    def q_block(i, pt):
        return (b_of(i) * parts + i % parts, h_of(i))

    def page_spec(r):
        return pl.BlockSpec((None, page, H, DA_DV), lambda i, pt: (pt[i, r], 0, 0, 0))

    page_specs = [page_spec(r) for r in range(n_pages)]
    new_spec = pl.BlockSpec((n_tok, H, DA_DV), lambda i, pt: (i, 0, 0))
    grid_spec = pltpu.PrefetchScalarGridSpec(
        num_scalar_prefetch=1,
        grid=(N,),
        in_specs=[
            const((8, LANES)), const((1, DA_DV)),
            pl.BlockSpec((rows_p, DA_DV), q_block),
            pl.BlockSpec((S, DA_DV), lambda i, pt: (b_of(i), H + h_of(i))),
            pl.BlockSpec((S, DA_DV), lambda i, pt: (b_of(i), 2 * H + h_of(i))),
            pl.BlockSpec((None, 2, CHUNK, CHUNK), lambda i, pt: (h_of(i), 0, 0, 0)),
            pl.BlockSpec((n_tok, width), lambda i, pt: (i, 0)),
            new_spec, new_spec,
            const((rows, page * H)), const((rows, page * H)), const((rows, page)),
        ] + page_specs + page_specs,
        out_specs=[pl.BlockSpec((rows_p, DA_DV), q_block), pl.BlockSpec((n_tok, width), lambda i, pt: (i, 0))],
    )
    return pl.pallas_call(
        functools.partial(_attn_fused_kernel, parts=parts, tpp=tpp, t=t, n_tok=n_tok, page=page, lam_init=lam_init),
        grid_spec=grid_spec,
        out_shape=[jax.ShapeDtypeStruct((B * S, width), BF16), jax.ShapeDtypeStruct((N * n_tok, width), F32)],
        compiler_params=_cparams("arbitrary"),
        name="attn_fused",
    )(page_idx, lam, subln, zp, zp, zp, near, zs, k_new, v_new, far, last, new,
      *([cache_k] * n_pages), *([cache_v] * n_pages))


def _sgate_kernel(u_ref, vn_ref, w_ref, b_ref, o_ref, *, seg, n_chunk):
    i = lax.broadcasted_iota(jnp.int32, (CHUNK, CHUNK), 0)
    j = lax.broadcasted_iota(jnp.int32, (CHUNK, CHUNK), 1)
    keep = (_div_pow2(i, seg) == _div_pow2(j, seg)) & (j <= i)
    gd = u_ref.shape[1] // SG_GROUPS
    for g in range(SG_GROUPS):
        w = jnp.where(keep, w_ref[g], 0.0).astype(BF16)
        cols = slice(g * gd, (g + 1) * gd)
        for c in range(n_chunk):
            rws = slice(c * CHUNK, (c + 1) * CHUNK)
            mix = jnp.dot(w, vn_ref[rws, cols].astype(BF16), preferred_element_type=F32) + b_ref[:, cols]
            o_ref[rws, cols] = (u_ref[rws, cols].astype(F32) * mix).astype(o_ref.dtype)


def _sgate(z, w, b, u_sec, vn_sec, n_chunk):
    T = z.shape[0]
    width = b.shape[1]
    rb = n_chunk * CHUNK
    return pl.pallas_call(
        functools.partial(_sgate_kernel, seg=CHUNK, n_chunk=n_chunk),
        grid=(T // rb,),
        in_specs=[
            pl.BlockSpec((rb, width), lambda i: (i, u_sec)),
            pl.BlockSpec((rb, width), lambda i: (i, vn_sec)),
            pl.BlockSpec((SG_GROUPS, CHUNK, CHUNK), lambda i: (0, 0, 0)),
            pl.BlockSpec((CHUNK, width), lambda i: (0, 0)),
        ],
        out_specs=pl.BlockSpec((rb, width), lambda i: (i, 0)),
        out_shape=jax.ShapeDtypeStruct((T, width), BF16),
        compiler_params=_cparams("parallel"),
        name="sgate",
    )(z, z, w, b)


def _sgate_seq_kernel(w_ref, b_ref, u_ref, vn_ref, o_ref, wbd_ref, bias_ref, *, seg):
    gd = u_ref.shape[1] // SG_GROUPS

    @pl.when(pl.program_id(0) == 0)
    def _():
        i = lax.broadcasted_iota(jnp.int32, (CHUNK, CHUNK), 0)
        j = lax.broadcasted_iota(jnp.int32, (CHUNK, CHUNK), 1)
        same = _div_pow2(i, seg) == _div_pow2(j, seg)
        ti, tj = i & (seg - 1), j & (seg - 1)
        for g in range(SG_GROUPS):
            w = jnp.zeros((CHUNK, CHUNK), F32)
            bias = jnp.zeros((CHUNK, gd), F32)
            for a in range(seg):
                bias = jnp.where(ti == a, b_ref[g, a], bias)
                for b in range(a + 1):
                    w = jnp.where((ti == a) & (tj == b), w_ref[g * seg + a, b], w)
            wbd_ref[g] = jnp.where(same, w, 0.0).astype(BF16)
            bias_ref[:, g * gd:(g + 1) * gd] = bias

    for g in range(SG_GROUPS):
        cols = slice(g * gd, (g + 1) * gd)
        mix = jnp.dot(wbd_ref[g], vn_ref[:, cols].astype(BF16), preferred_element_type=F32) + bias_ref[:, cols]
        o_ref[:, cols] = (u_ref[:, cols].astype(F32) * mix).astype(o_ref.dtype)


def _sgate_seq(z, w, b, seg, u_sec, vn_sec):
    T = z.shape[0]
    width = SG_GROUPS * CHUNK
    assert CHUNK % seg == 0 and seg & (seg - 1) == 0 and T % CHUNK == 0
    return pl.pallas_call(
        functools.partial(_sgate_seq_kernel, seg=seg),
        grid=(T // CHUNK,),
        in_specs=[
            pl.BlockSpec(memory_space=pltpu.SMEM), pl.BlockSpec(memory_space=pltpu.SMEM),
            pl.BlockSpec((CHUNK, width), lambda i: (i, u_sec)),
            pl.BlockSpec((CHUNK, width), lambda i: (i, vn_sec)),
        ],
        out_specs=pl.BlockSpec((CHUNK, width), lambda i: (i, 0)),
        out_shape=jax.ShapeDtypeStruct((T, width), BF16),
        scratch_shapes=[pltpu.VMEM((SG_GROUPS, CHUNK, CHUNK), BF16), pltpu.VMEM((CHUNK, width), F32)],
        compiler_params=_cparams("arbitrary"),
        name="sgate_seq",
    )(w[:, :seg, :seg].reshape(SG_GROUPS * seg, seg), b[:, :seg], z, z)


def _xattn_prompt_kernel(q_ref, k_ref, v_ref, o_ref):
    heads = [slice(h * MX_DH, (h + 1) * MX_DH) for h in range(MX_HEADS)]
    scores = [_nt_dot(q_ref[:, cols], k_ref[:, cols]) for cols in heads]
    exps = [jnp.exp(s - jnp.max(s, axis=-1, keepdims=True)) for s in scores]
    for cols, e in zip(heads, exps):
        o = jnp.dot(e.astype(BF16), v_ref[:, cols], preferred_element_type=F32)
        o_ref[:, cols] = (o / jnp.sum(e, axis=-1, keepdims=True)).astype(o_ref.dtype)


def _xattn_prompt(z, mem_kv, B, S, tq, q_sec):
    nq = S // tq
    n_mem = mem_kv.shape[0] // B
    width = MX_HEADS * MX_DH
    return pl.pallas_call(
        _xattn_prompt_kernel,
        grid=(B, nq),
        in_specs=[
            pl.BlockSpec((tq, width), lambda b, i: (b * nq + i, q_sec)),
            pl.BlockSpec((n_mem, width), lambda b, i: (b, 0)),
            pl.BlockSpec((n_mem, width), lambda b, i: (b, 1)),
        ],
        out_specs=pl.BlockSpec((tq, width), lambda b, i: (b * nq + i, 0)),
        out_shape=jax.ShapeDtypeStruct((B * S, width), BF16),
        compiler_params=_cparams("parallel", "arbitrary"),
        name="xattn_prompt",
    )(z, mem_kv, mem_kv)


def _xattn_sample_kernel(q_ref, k_ref, v_ref, o_ref, *, spb, n_tok):
    n_mem = k_ref.shape[1]
    rows, flat = MX_HEADS * n_tok, n_mem * MX_HEADS
    r = lax.broadcasted_iota(jnp.int32, (rows, flat), 0)
    c = lax.broadcasted_iota(jnp.int32, (rows, flat), 1)
    own = (c & (MX_HEADS - 1)) == _div_pow2(r, n_tok)
    for j in range(spb):
        toks = slice(j * n_tok, (j + 1) * n_tok)
        q = q_ref[toks, :]
        qq = jnp.concatenate([q[:, h * MX_DH:(h + 1) * MX_DH] for h in range(MX_HEADS)], axis=0).astype(BF16)
        s = jnp.where(own, _nt_dot(qq, k_ref[j].reshape(flat, MX_DH).astype(BF16)), NEG_INF)
        e = jnp.exp(s - jnp.max(s, axis=-1, keepdims=True))
        o = jnp.dot(e.astype(BF16), v_ref[j].reshape(flat, MX_DH).astype(BF16), preferred_element_type=F32)
        o = o / jnp.sum(e, axis=-1, keepdims=True)
        for h in range(MX_HEADS):
            o_ref[toks, h * MX_DH:(h + 1) * MX_DH] = o[h * n_tok:(h + 1) * n_tok].astype(o_ref.dtype)


def _xattn_sample(z, mem_k, mem_v, N, n_tok, q_sec, seq_off, spb):
    n_mem = mem_k.shape[1]
    width = MX_HEADS * MX_DH
    assert (MX_HEADS * n_tok) % BF16_SUBLANES == 0 and N % spb == 0 and seq_off % spb == 0
    mem_spec = pl.BlockSpec((spb, n_mem, MX_HEADS, MX_DH), lambda n: (seq_off // spb + n, 0, 0, 0))
    return pl.pallas_call(
        functools.partial(_xattn_sample_kernel, spb=spb, n_tok=n_tok),
        grid=(N // spb,),
        in_specs=[pl.BlockSpec((spb * n_tok, width), lambda n: (n, q_sec)), mem_spec, mem_spec],
        out_specs=pl.BlockSpec((spb * n_tok, width), lambda n: (n, 0)),
        out_shape=jax.ShapeDtypeStruct((N * n_tok, width), F32),
        compiler_params=_cparams("parallel"),
        name="xattn_sample",
    )(z, mem_k, mem_v)


def _post_kernel(x_ref, da_ref, sg_ref, mx_ref, gate_ref, wb_ref, wo_ref, gf_ref, wu_ref, wd_ref, gfin_ref,
                 o_ref, y_ref, *, d, d_ff, fc, n_sub, final_norm):
    branches = (da_ref, sg_ref, mx_ref)
    rs = x_ref.shape[0] // n_sub
    subs = [slice(sb * rs, (sb + 1) * rs) for sb in range(n_sub)]
    merged = [None] * n_sub
    for k in range(N_BRANCH):
        for i, rws in enumerate(subs):
            br = jnp.dot(branches[k][rws, :].astype(BF16), wb_ref[k * d:(k + 1) * d, :], preferred_element_type=F32)
            term = gate_ref[rws, k * d:(k + 1) * d].astype(F32) * br
            merged[i] = term if merged[i] is None else merged[i] + term
    h2 = []
    for i, rws in enumerate(subs):
        x1 = x_ref[rws, :] + jnp.dot(merged[i].astype(BF16), wo_ref[...], preferred_element_type=F32)
        y_ref[rws, :] = x1
        h2.append(_rms(x1, gf_ref[...]).astype(BF16))
    for c in range(d_ff // fc):
        up = [(jnp.dot(h, wu_ref[:, c * fc:(c + 1) * fc], preferred_element_type=F32),
               jnp.dot(h, wu_ref[:, d_ff + c * fc:d_ff + (c + 1) * fc], preferred_element_type=F32)) for h in h2]
        for (a, b), rws in zip(up, subs):
            act = (jax.nn.silu(a) * b).astype(BF16)
            y_ref[rws, :] += jnp.dot(act, wd_ref[c * fc:(c + 1) * fc, :], preferred_element_type=F32)
    for rws in subs:
        y = y_ref[rws, :]
        o_ref[rws, :] = _rms(y, gfin_ref[...]) if final_norm else y


def _post(x, o_da, o_sg, o_mx, z, gate_sec, wb, wo, g_ffn, wu, wd, g_final, tm, final_norm):
    T, d = x.shape
    d_ff = wd.shape[0]
    fc = 256
    n_sub = 2 if tm >= 512 else 1
    row = lambda i: (i, 0)
    whole = lambda i: (0, 0)

    def resident(shape):
        return pl.BlockSpec(shape, whole, pipeline_mode=pl.Buffered(1))

    return pl.pallas_call(
        functools.partial(_post_kernel, d=d, d_ff=d_ff, fc=fc, n_sub=n_sub, final_norm=final_norm),
        grid=(T // tm,),
        in_specs=[
            pl.BlockSpec((tm, d), row), pl.BlockSpec((tm, d), row), pl.BlockSpec((tm, d), row), pl.BlockSpec((tm, d), row),
            pl.BlockSpec((tm, N_BRANCH * d), lambda i: (i, gate_sec)),
            resident(wb.shape), resident(wo.shape), resident((1, d)), resident(wu.shape), resident(wd.shape),
            resident((1, d)),
        ],
        out_specs=pl.BlockSpec((tm, d), row),
        out_shape=jax.ShapeDtypeStruct((T, d), F32),
        scratch_shapes=[pltpu.VMEM((tm, d), F32)],
        compiler_params=_cparams("parallel"),
        name="post",
    )(x, o_da, o_sg, o_mx, z, wb, wo, g_ffn, wu, wd, g_final)


_IN_SECTIONS = ("q", "k", "v", "u", "s", "m", "g0", "g1", "g2")
_IN_ACTS = {
    "q": ("scale", DA_DK ** -0.5 * LOG2E), "k": ("none",), "v": ("none",), "u": ("gelu",), "s": ("gelu_rms",),
    "m": ("scale", MX_DH ** -0.5), "g0": ("sigmoid",), "g1": ("sigmoid",), "g2": ("sigmoid",),
}
_IN_COPIES = {"k": 0, "v": 1}


def kernel(x_prompt, x_sample, mem_prompt, cache_da_k, cache_da_v, cache_mem_k, cache_mem_v, page_table, g_attn, w_in,
           da_lam, da_subln, rel_bias, sg_norm, sg_w, sg_b, g_mem, w_mem_kv, w_branch, w_out, g_ffn, w_up, w_down,
           g_final):
    B, S, D = x_prompt.shape
    N, n_tok, _ = x_sample.shape
    depth, n_pool, page = cache_da_k.shape[:3]
    n_mem = mem_prompt.shape[1]
    n_pages = page_table.shape[1]
    width = DA_HEADS * DA_DV
    assert D == width == SG_GROUPS * CHUNK == MX_HEADS * MX_DH and page == CHUNK and n_tok <= CHUNK

    t_attn = min(512, S)
    pps = max(d for d in range(1, MAX_PAGES_PER_STEP + 1) if n_pages % d == 0)
    spb_da = 1
    units = B * DA_HEADS
    fuse_attn = (N % units == 0 and (S // t_attn) % (N // units) == 0 and n_pages <= MAX_PAGES_PER_STEP
                 and S % t_attn == 0)
    spb_mx = 4 if N % 4 == 0 else 1
    tm_p = 256
    tm_s = min(256, N * n_tok)
    tm_post = 512
    sec = {n: i for i, n in enumerate(_IN_SECTIONS)}
    in_sections = [(_IN_ACTS[n], _IN_COPIES.get(n)) for n in _IN_SECTIONS]

    xp = x_prompt.reshape(B * S, D)
    xs = x_sample.reshape(N * n_tok, D)
    mem = mem_prompt.reshape(B * n_mem, D)
    ck = cache_da_k.reshape(depth * n_pool, page, DA_HEADS, DA_DV)
    cv = cache_da_v.reshape(depth * n_pool, page, DA_HEADS, DA_DV)
    cmk = cache_mem_k.reshape(depth * N, n_mem, MX_HEADS, MX_DH)
    cmv = cache_mem_v.reshape(depth * N, n_mem, MX_HEADS, MX_DH)
    row = lambda a: a.reshape(1, -1)

    near_bias = _prompt_bias(rel_bias)
    seg_s = n_tok
    outs = {k: [] for k in ("dkp", "dvp", "dks", "dvs", "mkp", "mvp", "sgs")}
    for l in range(depth):
        lam_init = 0.8 - 0.6 * math.exp(-0.3 * l)
        w_in_l = w_in[l].astype(BF16)
        wb_l, wo_l = w_branch[l].astype(BF16), w_out[l].astype(BF16)
        wu_l, wd_l = w_up[l].astype(BF16), w_down[l].astype(BF16)
        far, last, new, lam = _sample_bias(rel_bias, da_lam[l], n_tok, page, lam_init)
        subln = row(da_subln[l])
        sgb_p = jnp.repeat(sg_b[l].T, CHUNK, axis=1)

        zs, ks32, vs32 = _norm_matmul(xs, row(g_attn[l]), w_in_l, row(sg_norm[l]), in_sections, F32, tm_s,
                                      "inproj_sample", copy_heads=DA_HEADS)
        zp, kp32, vp32 = _norm_matmul(xp, row(g_attn[l]), w_in_l, row(sg_norm[l]), in_sections, BF16, tm_p,
                                      "inproj_prompt")
        page_idx = page_table + l * n_pool
        if fuse_attn:
            oda_p, oda_s = _attn_fused(zp, zs, ks32, vs32, page_idx, ck, cv, lam, near_bias, far, last, new, subln,
                                       B, S, t_attn, N, n_tok, lam_init)
        else:
            oda_p = _attn_prompt(zp, lam, near_bias, subln, B, S, t_attn, lam_init)
            oda_s = _attn_sample(zs, ks32, vs32, page_idx, ck, cv, lam, far, last, new, subln, N, n_tok, lam_init,
                                 spb_da, pps)
        oda_p, oda_s, zp, zs, mem_l = lax.optimization_barrier((oda_p, oda_s, zp, zs, mem))

        mkv16, mk32, mv32 = _norm_matmul(mem_l, row(g_mem[l]), w_mem_kv[l].astype(BF16), row(sg_norm[l]),
                                         [(("none",), 0), (("none",), 1)], BF16, min(512, B * n_mem), "mem_kv",
                                         copy_heads=MX_HEADS)
        osg_p = _sgate(zp, sg_w[l], sgb_p, sec["u"], sec["s"], 8)
        omx_p = _xattn_prompt(zp, mkv16, B, S, 1024 if S % 1024 == 0 else 512, sec["m"])
        osg_s = _sgate_seq(zs, sg_w[l], sg_b[l], seg_s, sec["u"], sec["s"])
        omx_s = _xattn_sample(zs, cmk, cmv, N, n_tok, sec["m"], l * N, spb_mx)
        osg_p, omx_p, osg_s, omx_s, kp32, vp32 = lax.optimization_barrier((osg_p, omx_p, osg_s, omx_s, kp32, vp32))

        xs = _post(xs, oda_s, osg_s, omx_s, zs, sec["g0"] // N_BRANCH, wb_l, wo_l, row(g_ffn[l]), wu_l, wd_l,
                   row(g_final), tm_s, l == depth - 1)
        xp = _post(xp, oda_p, osg_p, omx_p, zp, sec["g0"] // N_BRANCH, wb_l, wo_l, row(g_ffn[l]), wu_l, wd_l,
                   row(g_final), tm_post, l == depth - 1)

        outs["dkp"].append(kp32.reshape(B, S, DA_HEADS, 2 * DA_DK))
        outs["dvp"].append(vp32.reshape(B, S, DA_HEADS, DA_DV))
        outs["dks"].append(ks32.reshape(N, n_tok, DA_HEADS, 2 * DA_DK))
        outs["dvs"].append(vs32.reshape(N, n_tok, DA_HEADS, DA_DV))
        outs["mkp"].append(mk32.reshape(B, n_mem, MX_HEADS, MX_DH))
        outs["mvp"].append(mv32.reshape(B, n_mem, MX_HEADS, MX_DH))
        outs["sgs"].append(zs[:, sec["s"] * D:(sec["s"] + 1) * D].reshape(N, n_tok, D))

    return (xp.reshape(B, S, D), xs.reshape(N, n_tok, D), jnp.stack(outs["dkp"]), jnp.stack(outs["dvp"]),
            jnp.stack(outs["dks"]), jnp.stack(outs["dvs"]), jnp.stack(outs["mkp"]), jnp.stack(outs["mvp"]),
            jnp.stack(outs["sgs"]))
```

```python
import functools
import math

import jax
import jax.numpy as jnp
from jax import lax
from jax.experimental import pallas as pl
from jax.experimental.pallas import tpu as pltpu

F32 = jnp.float32
BF16 = jnp.bfloat16

DA_HEADS = 8
DA_DK = 64
DA_DV = 2 * DA_DK
SG_GROUPS = 8
CHUNK = 128
MX_HEADS = 4
MX_DH = 256
N_BUCKETS = 32
MAX_DISTANCE = 128
N_BRANCH = 3
RMS_EPS = 1e-6
NEG_INF = -1e30
LOG2E = math.log2(math.e)

LANES = 128
BF16_SUBLANES = 16
VMEM_LIMIT_BYTES = 56 * 1024 * 1024
MAX_PAGES_PER_STEP = 16


def _cparams(*sem):
    return pltpu.CompilerParams(dimension_semantics=sem, vmem_limit_bytes=VMEM_LIMIT_BYTES)


def _rms(x, g):
    ms = jnp.mean(x * x, axis=-1, keepdims=True)
    return x * lax.rsqrt(ms + RMS_EPS) * g


def _div_pow2(x, n):
    assert n > 0 and n & (n - 1) == 0, n
    return x >> (n.bit_length() - 1)


def _nt_dot(a, b):
    return lax.dot_general(a, b, (((1,), (1,)), ((), ())), preferred_element_type=F32)


def _lane_tile(x, n):
    return x if n == 1 else jnp.concatenate([x] * n, axis=1)


def _norm_matmul_kernel(x_ref, g_ref, w_ref, sgn_ref, z_ref, *copy_refs, sections, tn):
    hn = _rms(x_ref[...], g_ref[...]).astype(BF16)
    for sec, (act, copy_idx) in enumerate(sections):
        cols = slice(sec * tn, (sec + 1) * tn)
        acc = jnp.dot(hn, w_ref[:, cols], preferred_element_type=F32)
        if act[0] == "scale":
            val = acc * act[1]
        elif act[0] == "gelu":
            val = jax.nn.gelu(acc)
        elif act[0] == "gelu_rms":
            val = _rms(jax.nn.gelu(acc), sgn_ref[...])
        elif act[0] == "sigmoid":
            val = jax.nn.sigmoid(acc)
        else:
            val = acc
        z_ref[:, cols] = val.astype(z_ref.dtype)
        if copy_idx is not None:
            cref = copy_refs[copy_idx]
            if len(cref.shape) == 2:
                cref[...] = val
            else:
                hd = cref.shape[2]
                for hh in range(cref.shape[1]):
                    cref[:, hh, :] = val[:, hh * hd:(hh + 1) * hd]


def _norm_matmul(x, g, w, sgn, sections, z_dtype, tm, name, copy_heads=1):
    T, D = x.shape
    n_sec = len(sections)
    tn = w.shape[1] // n_sec
    n_copy = sum(1 for _, c in sections if c is not None)
    out_shape = [jax.ShapeDtypeStruct((T, n_sec * tn), z_dtype)]
    out_specs = [pl.BlockSpec((tm, n_sec * tn), lambda i: (i, 0))]
    for _ in range(n_copy):
        if copy_heads == 1:
            out_shape.append(jax.ShapeDtypeStruct((T, tn), F32))
            out_specs.append(pl.BlockSpec((tm, tn), lambda i: (i, 0)))
        else:
            out_shape.append(jax.ShapeDtypeStruct((T, copy_heads, tn // copy_heads), F32))
            out_specs.append(pl.BlockSpec((tm, copy_heads, tn // copy_heads), lambda i: (i, 0, 0)))

    def resident(shape):
        return pl.BlockSpec(shape, lambda i: (0, 0), pipeline_mode=pl.Buffered(1))

    return pl.pallas_call(
        functools.partial(_norm_matmul_kernel, sections=tuple(sections), tn=tn),
        grid=(T // tm,),
        in_specs=[pl.BlockSpec((tm, D), lambda i: (i, 0)), resident((1, D)), resident(w.shape), resident((1, tn))],
        out_specs=out_specs,
        out_shape=out_shape,
        compiler_params=_cparams("parallel"),
        name=name,
    )(x, g, w, sgn)


def _t5_bias(dist, rb_ref, h):
    n = jnp.maximum(dist, 0)
    max_exact = N_BUCKETS // 2
    nf = jnp.maximum(n, 1).astype(F32)
    rel = jnp.log(nf / max_exact) / math.log(MAX_DISTANCE / max_exact) * (N_BUCKETS - max_exact)
    out = jnp.zeros(dist.shape, F32)
    for b in range(max_exact):
        out = jnp.where(n == b, rb_ref[b, h], out)
    out = jnp.where(n >= max_exact, rb_ref[max_exact, h], out)
    for b in range(max_exact + 1, N_BUCKETS):
        out = jnp.where(rel >= b - max_exact, rb_ref[b, h], out)
    return jnp.where(dist >= 0, out * LOG2E, NEG_INF)


def _far_bias(rb_ref, h):
    return rb_ref[N_BUCKETS - 1, h] * LOG2E


def _prompt_bias_kernel(rb_ref, near_ref):
    d = lax.broadcasted_iota(jnp.int32, (CHUNK, CHUNK), 0) - lax.broadcasted_iota(jnp.int32, (CHUNK, CHUNK), 1)
    for h in range(near_ref.shape[0]):
        far = _far_bias(rb_ref, h)
        near_ref[h, 0] = _t5_bias(d, rb_ref, h) - far
        near_ref[h, 1] = _t5_bias(d + CHUNK, rb_ref, h) - far


def _prompt_bias(rel_bias):
    assert MAX_DISTANCE <= CHUNK + 1
    H = rel_bias.shape[1]
    return pl.pallas_call(
        _prompt_bias_kernel,
        in_specs=[pl.BlockSpec(memory_space=pltpu.SMEM)],
        out_shape=jax.ShapeDtypeStruct((H, 2, CHUNK, CHUNK), F32),
        name="prompt_bias",
    )(rel_bias)


def _tile_bias(near_ref, rows, cols):
    fill = {-1: jnp.full((CHUNK, CHUNK), NEG_INF, F32), 2: jnp.zeros((CHUNK, CHUNK), F32)}

    def block(bd):
        return near_ref[bd] if 0 <= bd < 2 else fill[max(min(bd, 2), -1)]

    return jnp.concatenate([jnp.concatenate([block(bi - bj) for bj in range(cols // CHUNK)], axis=1)
                            for bi in range(rows // CHUNK)], axis=0)


def _sample_bias_kernel(rb_ref, lam_p_ref, far_ref, last_ref, new_ref, lam_ref, *, n_tok, page, lam_init):
    hr = 2 * n_tok

    def grid(cols):
        tok = lax.broadcasted_iota(jnp.int32, (hr, cols), 0) & (n_tok - 1)
        col = lax.broadcasted_iota(jnp.int32, (hr, cols), 1)
        return tok, col, _div_pow2(col, DA_HEADS), col & (DA_HEADS - 1)

    tok, _, key, kh = grid(page * DA_HEADS)
    tok_n, col_n, key_n, kh_n = grid(page)
    for h in range(DA_HEADS):
        sl = slice(h * hr, (h + 1) * hr)
        far_ref[sl, :] = jnp.where(kh == h, _far_bias(rb_ref, h), NEG_INF)
        last_ref[sl, :] = jnp.where(kh == h, _t5_bias(page + tok - key, rb_ref, h), NEG_INF)
        new_ref[sl, :] = jnp.where((kh_n == h) & (col_n < n_tok * DA_HEADS), _t5_bias(tok_n - key_n, rb_ref, h), NEG_INF)
    lp = lam_p_ref[...]
    s1 = jnp.sum(lp[0:1, :] * lp[1:2, :], axis=-1, keepdims=True)
    s2 = jnp.sum(lp[2:3, :] * lp[3:4, :], axis=-1, keepdims=True)
    lam = jnp.exp(s1) - jnp.exp(s2) + lam_init
    lam_ref[...] = jnp.broadcast_to(lam, lam_ref.shape)


def _sample_bias(rel_bias, da_lam_l, n_tok, page, lam_init):
    assert n_tok & (n_tok - 1) == 0 and n_tok * DA_HEADS <= page
    rows = DA_HEADS * 2 * n_tok
    wide = jax.ShapeDtypeStruct((rows, page * DA_HEADS), F32)
    return pl.pallas_call(
        functools.partial(_sample_bias_kernel, n_tok=n_tok, page=page, lam_init=lam_init),
        in_specs=[pl.BlockSpec(memory_space=pltpu.SMEM), pl.BlockSpec(memory_space=pltpu.VMEM)],
        out_shape=[wide, wide, jax.ShapeDtypeStruct((rows, page), F32), jax.ShapeDtypeStruct((8, LANES), F32)],
        name="sample_bias",
    )(rel_bias, da_lam_l)


def _split_components(q):
    lane = lax.broadcasted_iota(jnp.int32, q.shape, 1)
    zero = jnp.zeros_like(q)
    return jnp.concatenate([jnp.where(lane < DA_DK, q, zero), jnp.where(lane >= DA_DK, q, zero)], axis=0)


def _diff_out(o1, o2, lam_row, subln, lam_init):
    o = o1 - lam_row * o2
    return _rms(o, subln) * (1.0 - lam_init)


def _softmax_update(m, acc, s_list, v_list, mxu_sums):
    m_new = m
    for s in s_list:
        m_new = jnp.maximum(m_new, jnp.max(s, axis=-1, keepdims=True))
    acc = acc * _lane_tile(jnp.exp2(m - m_new), 2)
    for s, v in zip(s_list, v_list):
        e = jnp.exp2(s - _lane_tile(m_new, s.shape[1] // LANES))
        if mxu_sums:
            v1 = jnp.concatenate([v, jnp.ones(v.shape, BF16)], axis=1)
            acc = acc + jnp.dot(e.astype(BF16), v1, preferred_element_type=F32)
        else:
            pv = jnp.dot(e.astype(BF16), v, preferred_element_type=F32)
            l = jnp.broadcast_to(jnp.sum(e, axis=-1, keepdims=True), pv.shape)
            acc = acc + jnp.concatenate([pv, l], axis=1)
    return m_new, acc


def _attn_prompt_kernel(lam_ref, q_ref, k_ref, v_ref, near_ref, sub_ref, *rest, t, nq, lam_init, ride_spb, n_tok):
    if ride_spb:
        qx_ref, kx_ref, vx_ref, o_ref, ox_ref = rest
        _xattn_sample_kernel(qx_ref, kx_ref, vx_ref, ox_ref, spb=ride_spb, n_tok=n_tok)
    else:
        o_ref, = rest
    _prompt_tiles(lam_ref, q_ref, k_ref, v_ref, near_ref, sub_ref, o_ref, t, range(nq), lam_init)


def _prompt_tiles(lam_ref, q_ref, k_ref, v_ref, near_ref, sub_ref, o_ref, t, tiles, lam_init):
    hb = t // 2
    near1 = near_ref[1]
    bias_a = _tile_bias(near_ref, t, hb)
    bias_b = _tile_bias(near_ref, hb, hb)

    def both(s, b):
        n = s.shape[0] // 2
        return jnp.concatenate([s[:n] + b, s[n:] + b], axis=0)

    def sub_fix(sr):
        return jnp.concatenate([sr[:, :t - CHUNK], sr[:, t - CHUNK:] + near1], axis=1)

    def second_half(x):
        return jnp.concatenate([x[hb:t], x[t + hb:]], axis=0)

    def put_second_half(x, xb):
        return jnp.concatenate([x[:hb], xb[:hb], x[t:t + hb], xb[hb:]], axis=0)

    for qi in tiles:
        q0 = qi * t
        r0 = (qi - tiles[0]) * t
        qq = _split_components(q_ref[r0:r0 + t, :])
        m = jnp.full((2 * t, LANES), -jnp.inf, F32)
        acc = jnp.zeros((2 * t, 2 * LANES), F32)
        for kt in range(qi):
            keys = slice(kt * t, (kt + 1) * t)
            s = _nt_dot(qq, k_ref[keys, :])
            if kt == qi - 1:
                s = jnp.concatenate([sub_fix(s[:CHUNK]), s[CHUNK:t], sub_fix(s[t:t + CHUNK]), s[t + CHUNK:]], axis=0)
            m, acc = _softmax_update(m, acc, [s], [v_ref[keys, :]], True)
        keys = slice(q0, q0 + hb)
        m, acc = _softmax_update(m, acc, [both(_nt_dot(qq, k_ref[keys, :]), bias_a)], [v_ref[keys, :]], True)
        keys = slice(q0 + hb, q0 + t)
        s = both(_nt_dot(second_half(qq), k_ref[keys, :]), bias_b)
        _, acc_b = _softmax_update(second_half(m), second_half(acc), [s], [v_ref[keys, :]], True)
        acc = put_second_half(acc, acc_b)
        o = acc[:, :DA_DV] / acc[:, DA_DV:]
        out = _diff_out(o[:t], o[t:], lam_ref[0:1, :], sub_ref[...], lam_init)
        o_ref[r0:r0 + t, :] = out.astype(o_ref.dtype)


def _attn_prompt(z, lam, near, subln, B, S, t, lam_init, rider=None):
    assert S % t == 0 and t % (2 * CHUNK) == 0 and MAX_DISTANCE <= CHUNK + 1
    H = DA_HEADS
    in_specs = [
        pl.BlockSpec((8, LANES), lambda b, h: (0, 0)),
        pl.BlockSpec((S, DA_DV), lambda b, h: (b, h)),
        pl.BlockSpec((S, DA_DV), lambda b, h: (b, H + h)),
        pl.BlockSpec((S, DA_DV), lambda b, h: (b, 2 * H + h)),
        pl.BlockSpec((None, 2, CHUNK, CHUNK), lambda b, h: (h, 0, 0, 0)),
        pl.BlockSpec((1, DA_DV), lambda b, h: (0, 0)),
    ]
    operands = [lam, z, z, z, near, subln]
    out_specs = [pl.BlockSpec((S, DA_DV), lambda b, h: (b, h))]
    out_shape = [jax.ShapeDtypeStruct((B * S, H * DA_DV), BF16)]
    spb, n_tok = 0, 0
    if rider is not None:
        zs, mem_k, mem_v, N, n_tok, q_sec, seq_off = rider
        spb = N // (B * H)
        n_mem = mem_k.shape[1]
        width = MX_HEADS * MX_DH
        assert N == spb * B * H and seq_off % spb == 0 and (MX_HEADS * n_tok) % BF16_SUBLANES == 0
        mem_spec = pl.BlockSpec((spb, n_mem, MX_HEADS, MX_DH), lambda b, h: (seq_off // spb + b * H + h, 0, 0, 0))
        in_specs += [pl.BlockSpec((spb * n_tok, width), lambda b, h: (b * H + h, q_sec)), mem_spec, mem_spec]
        operands += [zs, mem_k, mem_v]
        out_specs.append(pl.BlockSpec((spb * n_tok, width), lambda b, h: (b * H + h, 0)))
        out_shape.append(jax.ShapeDtypeStruct((N * n_tok, width), F32))
    outs = pl.pallas_call(
        functools.partial(_attn_prompt_kernel, t=t, nq=S // t, lam_init=lam_init, ride_spb=spb, n_tok=n_tok),
        grid=(B, H),
        in_specs=in_specs,
        out_specs=out_specs,
        out_shape=out_shape,
        compiler_params=_cparams("parallel", "arbitrary"),
        name="attn_prompt",
    )(*operands)
    return outs if rider is not None else outs[0]


def _attn_sample_kernel(pt_ref, lam_ref, q_ref, kn_ref, vn_ref, far_ref, last_ref, new_ref, sub_ref, *rest,
                        spb, pps, n_tok, page, lam_init):
    n_in = spb * pps
    kp_refs, vp_refs = rest[:n_in], rest[n_in:2 * n_in]
    o_ref, qq_ref, m_ref, acc_ref = rest[2 * n_in:]
    p = pl.program_id(1)
    last = pl.num_programs(1) - 1
    hr = 2 * n_tok
    flat = page * DA_HEADS

    @pl.when(p == 0)
    def _():
        for j in range(spb):
            q = q_ref[j * n_tok:(j + 1) * n_tok, :]
            for h in range(DA_HEADS):
                qq_ref[j, h * hr:(h + 1) * hr, :] = _split_components(q[:, h * DA_DV:(h + 1) * DA_DV]).astype(BF16)
        m_ref[...] = jnp.full(m_ref.shape, -jnp.inf, F32)
        acc_ref[...] = jnp.zeros(acc_ref.shape, F32)

    far = far_ref[...]
    tail = jnp.where(p == last, last_ref[...], far)
    for j in range(spb):
        qq = qq_ref[j]
        s_list, v_list = [], []
        for r in range(pps):
            s = _nt_dot(qq, kp_refs[j * pps + r][...].reshape(flat, DA_DV).astype(BF16))
            s_list.append(s + (tail if r == pps - 1 else far))
            v_list.append(vp_refs[j * pps + r][...].reshape(flat, DA_DV).astype(BF16))
        m, acc = _softmax_update(m_ref[j], acc_ref[j], s_list, v_list, False)
        m_ref[j] = m
        acc_ref[j] = acc

    @pl.when(p == last)
    def _():
        pad = jnp.zeros((page - n_tok * DA_HEADS, DA_DV), F32)
        for j in range(spb):
            toks = slice(j * n_tok, (j + 1) * n_tok)

            def new_rows(ref):
                return jnp.concatenate([ref[toks].reshape(n_tok * DA_HEADS, DA_DV), pad], axis=0).astype(BF16)

            s_new = _nt_dot(qq_ref[j], new_rows(kn_ref)) + new_ref[...]
            _, acc2 = _softmax_update(m_ref[j], acc_ref[j], [s_new], [new_rows(vn_ref)], False)
            o = acc2[:, :DA_DV] / acc2[:, DA_DV:]
            for h in range(DA_HEADS):
                oh = o[h * hr:(h + 1) * hr]
                out = _diff_out(oh[:n_tok], oh[n_tok:], lam_ref[0:1, :], sub_ref[...], lam_init)
                o_ref[toks, h * DA_DV:(h + 1) * DA_DV] = out.astype(o_ref.dtype)


def _attn_sample(z, k_new, v_new, page_idx, cache_k, cache_v, lam, far, last, new, subln, N, n_tok, lam_init, spb, pps):
    n_pages = page_idx.shape[1]
    page = cache_k.shape[1]
    width = DA_HEADS * DA_DV
    rows = DA_HEADS * 2 * n_tok
    assert (2 * n_tok) % BF16_SUBLANES == 0 and n_pages % pps == 0 and N % spb == 0

    def const(shape):
        return pl.BlockSpec(shape, lambda n, p, pt: (0,) * len(shape))

    def page_spec(j, r):
        return pl.BlockSpec((None, page, DA_HEADS, DA_DV), lambda n, p, pt: (pt[n * spb + j, p * pps + r], 0, 0, 0))

    page_specs = [page_spec(j, r) for j in range(spb) for r in range(pps)]
    new_spec = pl.BlockSpec((spb * n_tok, DA_HEADS, DA_DV), lambda n, p, pt: (n, 0, 0))
    grid_spec = pltpu.PrefetchScalarGridSpec(
        num_scalar_prefetch=1,
        grid=(N // spb, n_pages // pps),
        in_specs=[
            const((8, LANES)),
            pl.BlockSpec((spb * n_tok, width), lambda n, p, pt: (n, 0)),
            new_spec, new_spec,
            const((rows, page * DA_HEADS)), const((rows, page * DA_HEADS)), const((rows, page)),
            const((1, DA_DV)),
        ] + page_specs + page_specs,
        out_specs=pl.BlockSpec((spb * n_tok, width), lambda n, p, pt: (n, 0)),
        scratch_shapes=[pltpu.VMEM((spb, rows, DA_DV), BF16), pltpu.VMEM((spb, rows, LANES), F32),
                        pltpu.VMEM((spb, rows, 2 * LANES), F32)],
    )
    n_in = spb * pps
    return pl.pallas_call(
        functools.partial(_attn_sample_kernel, spb=spb, pps=pps, n_tok=n_tok, page=page, lam_init=lam_init),
        grid_spec=grid_spec,
        out_shape=jax.ShapeDtypeStruct((N * n_tok, width), F32),
        compiler_params=_cparams("parallel", "arbitrary"),
        name="attn_sample",
    )(page_idx, lam, z, k_new, v_new, far, last, new, subln, *([cache_k] * n_in), *([cache_v] * n_in))


def _sgate_kernel(u_ref, vn_ref, w_ref, b_ref, o_ref, *, seg, n_chunk):
    i = lax.broadcasted_iota(jnp.int32, (CHUNK, CHUNK), 0)
    j = lax.broadcasted_iota(jnp.int32, (CHUNK, CHUNK), 1)
    keep = (_div_pow2(i, seg) == _div_pow2(j, seg)) & (j <= i)
    gd = u_ref.shape[1] // SG_GROUPS
    for g in range(SG_GROUPS):
        w = jnp.where(keep, w_ref[g], 0.0).astype(BF16)
        cols = slice(g * gd, (g + 1) * gd)
        for c in range(n_chunk):
            rws = slice(c * CHUNK, (c + 1) * CHUNK)
            mix = jnp.dot(w, vn_ref[rws, cols].astype(BF16), preferred_element_type=F32) + b_ref[:, cols]
            o_ref[rws, cols] = (u_ref[rws, cols].astype(F32) * mix).astype(o_ref.dtype)


def _sgate(z, w, b, u_sec, vn_sec, n_chunk):
    T = z.shape[0]
    width = b.shape[1]
    rb = n_chunk * CHUNK
    return pl.pallas_call(
        functools.partial(_sgate_kernel, seg=CHUNK, n_chunk=n_chunk),
        grid=(T // rb,),
        in_specs=[
            pl.BlockSpec((rb, width), lambda i: (i, u_sec)),
            pl.BlockSpec((rb, width), lambda i: (i, vn_sec)),
            pl.BlockSpec((SG_GROUPS, CHUNK, CHUNK), lambda i: (0, 0, 0)),
            pl.BlockSpec((CHUNK, width), lambda i: (0, 0)),
        ],
        out_specs=pl.BlockSpec((rb, width), lambda i: (i, 0)),
        out_shape=jax.ShapeDtypeStruct((T, width), BF16),
        compiler_params=_cparams("parallel"),
        name="sgate",
    )(z, z, w, b)


def _sgate_seq_kernel(w_ref, b_ref, u_ref, vn_ref, o_ref, wbd_ref, bias_ref, *, seg):
    gd = u_ref.shape[1] // SG_GROUPS

    @pl.when(pl.program_id(0) == 0)
    def _():
        i = lax.broadcasted_iota(jnp.int32, (CHUNK, CHUNK), 0)
        j = lax.broadcasted_iota(jnp.int32, (CHUNK, CHUNK), 1)
        same = _div_pow2(i, seg) == _div_pow2(j, seg)
        ti, tj = i & (seg - 1), j & (seg - 1)
        for g in range(SG_GROUPS):
            w = jnp.zeros((CHUNK, CHUNK), F32)
            bias = jnp.zeros((CHUNK, gd), F32)
            for a in range(seg):
                bias = jnp.where(ti == a, b_ref[g, a], bias)
                for b in range(a + 1):
                    w = jnp.where((ti == a) & (tj == b), w_ref[g * seg + a, b], w)
            wbd_ref[g] = jnp.where(same, w, 0.0).astype(BF16)
            bias_ref[:, g * gd:(g + 1) * gd] = bias

    for g in range(SG_GROUPS):
        cols = slice(g * gd, (g + 1) * gd)
        mix = jnp.dot(wbd_ref[g], vn_ref[:, cols].astype(BF16), preferred_element_type=F32) + bias_ref[:, cols]
        o_ref[:, cols] = (u_ref[:, cols].astype(F32) * mix).astype(o_ref.dtype)


def _sgate_seq(z, w, b, seg, u_sec, vn_sec):
    T = z.shape[0]
    width = SG_GROUPS * CHUNK
    assert CHUNK % seg == 0 and seg & (seg - 1) == 0 and T % CHUNK == 0
    return pl.pallas_call(
        functools.partial(_sgate_seq_kernel, seg=seg),
        grid=(T // CHUNK,),
        in_specs=[
            pl.BlockSpec(memory_space=pltpu.SMEM), pl.BlockSpec(memory_space=pltpu.SMEM),
            pl.BlockSpec((CHUNK, width), lambda i: (i, u_sec)),
            pl.BlockSpec((CHUNK, width), lambda i: (i, vn_sec)),
        ],
        out_specs=pl.BlockSpec((CHUNK, width), lambda i: (i, 0)),
        out_shape=jax.ShapeDtypeStruct((T, width), BF16),
        scratch_shapes=[pltpu.VMEM((SG_GROUPS, CHUNK, CHUNK), BF16), pltpu.VMEM((CHUNK, width), F32)],
        compiler_params=_cparams("arbitrary"),
        name="sgate_seq",
    )(w[:, :seg, :seg].reshape(SG_GROUPS * seg, seg), b[:, :seg], z, z)


def _xattn_prompt_kernel(q_ref, k_ref, v_ref, o_ref):
    heads = [slice(h * MX_DH, (h + 1) * MX_DH) for h in range(MX_HEADS)]
    scores = [_nt_dot(q_ref[:, cols], k_ref[:, cols]) for cols in heads]
    exps = [jnp.exp(s - jnp.max(s, axis=-1, keepdims=True)) for s in scores]
    for cols, e in zip(heads, exps):
        o = jnp.dot(e.astype(BF16), v_ref[:, cols], preferred_element_type=F32)
        o_ref[:, cols] = (o / jnp.sum(e, axis=-1, keepdims=True)).astype(o_ref.dtype)


def _xattn_prompt(z, mem_kv, B, S, tq, q_sec):
    nq = S // tq
    n_mem = mem_kv.shape[0] // B
    width = MX_HEADS * MX_DH
    return pl.pallas_call(
        _xattn_prompt_kernel,
        grid=(B, nq),
        in_specs=[
            pl.BlockSpec((tq, width), lambda b, i: (b * nq + i, q_sec)),
            pl.BlockSpec((n_mem, width), lambda b, i: (b, 0)),
            pl.BlockSpec((n_mem, width), lambda b, i: (b, 1)),
        ],
        out_specs=pl.BlockSpec((tq, width), lambda b, i: (b * nq + i, 0)),
        out_shape=jax.ShapeDtypeStruct((B * S, width), BF16),
        compiler_params=_cparams("parallel", "arbitrary"),
        name="xattn_prompt",
    )(z, mem_kv, mem_kv)


def _xattn_sample_kernel(q_ref, k_ref, v_ref, o_ref, *, spb, n_tok):
    n_mem = k_ref.shape[1]
    rows, flat = MX_HEADS * n_tok, n_mem * MX_HEADS
    r = lax.broadcasted_iota(jnp.int32, (rows, flat), 0)
    c = lax.broadcasted_iota(jnp.int32, (rows, flat), 1)
    own = (c & (MX_HEADS - 1)) == _div_pow2(r, n_tok)
    for j in range(spb):
        toks = slice(j * n_tok, (j + 1) * n_tok)
        q = q_ref[toks, :]
        qq = jnp.concatenate([q[:, h * MX_DH:(h + 1) * MX_DH] for h in range(MX_HEADS)], axis=0).astype(BF16)
        s = jnp.where(own, _nt_dot(qq, k_ref[j].reshape(flat, MX_DH).astype(BF16)), NEG_INF)
        e = jnp.exp(s - jnp.max(s, axis=-1, keepdims=True))
        o = jnp.dot(e.astype(BF16), v_ref[j].reshape(flat, MX_DH).astype(BF16), preferred_element_type=F32)
        o = o / jnp.sum(e, axis=-1, keepdims=True)
        for h in range(MX_HEADS):
            o_ref[toks, h * MX_DH:(h + 1) * MX_DH] = o[h * n_tok:(h + 1) * n_tok].astype(o_ref.dtype)


def _xattn_sample(z, mem_k, mem_v, N, n_tok, q_sec, seq_off, spb):
    n_mem = mem_k.shape[1]
    width = MX_HEADS * MX_DH
    assert (MX_HEADS * n_tok) % BF16_SUBLANES == 0 and N % spb == 0 and seq_off % spb == 0
    mem_spec = pl.BlockSpec((spb, n_mem, MX_HEADS, MX_DH), lambda n: (seq_off // spb + n, 0, 0, 0))
    return pl.pallas_call(
        functools.partial(_xattn_sample_kernel, spb=spb, n_tok=n_tok),
        grid=(N // spb,),
        in_specs=[pl.BlockSpec((spb * n_tok, width), lambda n: (n, q_sec)), mem_spec, mem_spec],
        out_specs=pl.BlockSpec((spb * n_tok, width), lambda n: (n, 0)),
        out_shape=jax.ShapeDtypeStruct((N * n_tok, width), F32),
        compiler_params=_cparams("parallel"),
        name="xattn_sample",
    )(z, mem_k, mem_v)


def _post_kernel(x_ref, da_ref, sg_ref, mx_ref, gate_ref, wb_ref, wo_ref, gf_ref, wu_ref, wd_ref, gfin_ref,
                 o_ref, y_ref, *, d, d_ff, fc, n_sub, final_norm):
    branches = (da_ref, sg_ref, mx_ref)
    rs = x_ref.shape[0] // n_sub
    subs = [slice(sb * rs, (sb + 1) * rs) for sb in range(n_sub)]
    merged = [None] * n_sub
    for k in range(N_BRANCH):
        for i, rws in enumerate(subs):
            br = jnp.dot(branches[k][rws, :].astype(BF16), wb_ref[k * d:(k + 1) * d, :], preferred_element_type=F32)
            term = gate_ref[rws, k * d:(k + 1) * d].astype(F32) * br
            merged[i] = term if merged[i] is None else merged[i] + term
    h2 = []
    for i, rws in enumerate(subs):
        x1 = x_ref[rws, :] + jnp.dot(merged[i].astype(BF16), wo_ref[...], preferred_element_type=F32)
        y_ref[rws, :] = x1
        h2.append(_rms(x1, gf_ref[...]).astype(BF16))
    for c in range(d_ff // fc):
        up = [(jnp.dot(h, wu_ref[:, c * fc:(c + 1) * fc], preferred_element_type=F32),
               jnp.dot(h, wu_ref[:, d_ff + c * fc:d_ff + (c + 1) * fc], preferred_element_type=F32)) for h in h2]
        for (a, b), rws in zip(up, subs):
            act = (jax.nn.silu(a) * b).astype(BF16)
            y_ref[rws, :] += jnp.dot(act, wd_ref[c * fc:(c + 1) * fc, :], preferred_element_type=F32)
    for rws in subs:
        y = y_ref[rws, :]
        o_ref[rws, :] = _rms(y, gfin_ref[...]) if final_norm else y


def _post(x, o_da, o_sg, o_mx, z, gate_sec, wb, wo, g_ffn, wu, wd, g_final, tm, final_norm):
    T, d = x.shape
    d_ff = wd.shape[0]
    fc = 256
    n_sub = 2 if tm >= 512 else 1
    row = lambda i: (i, 0)
    whole = lambda i: (0, 0)

    def resident(shape):
        return pl.BlockSpec(shape, whole, pipeline_mode=pl.Buffered(1))

    return pl.pallas_call(
        functools.partial(_post_kernel, d=d, d_ff=d_ff, fc=fc, n_sub=n_sub, final_norm=final_norm),
        grid=(T // tm,),
        in_specs=[
            pl.BlockSpec((tm, d), row), pl.BlockSpec((tm, d), row), pl.BlockSpec((tm, d), row), pl.BlockSpec((tm, d), row),
            pl.BlockSpec((tm, N_BRANCH * d), lambda i: (i, gate_sec)),
            resident(wb.shape), resident(wo.shape), resident((1, d)), resident(wu.shape), resident(wd.shape),
            resident((1, d)),
        ],
        out_specs=pl.BlockSpec((tm, d), row),
        out_shape=jax.ShapeDtypeStruct((T, d), F32),
        scratch_shapes=[pltpu.VMEM((tm, d), F32)],
        compiler_params=_cparams("parallel"),
        name="post",
    )(x, o_da, o_sg, o_mx, z, wb, wo, g_ffn, wu, wd, g_final)


_IN_SECTIONS = ("q", "k", "v", "u", "s", "m", "g0", "g1", "g2")
_IN_ACTS = {
    "q": ("scale", DA_DK ** -0.5 * LOG2E), "k": ("none",), "v": ("none",), "u": ("gelu",), "s": ("gelu_rms",),
    "m": ("scale", MX_DH ** -0.5), "g0": ("sigmoid",), "g1": ("sigmoid",), "g2": ("sigmoid",),
}
_IN_COPIES = {"k": 0, "v": 1}


def kernel(x_prompt, x_sample, mem_prompt, cache_da_k, cache_da_v, cache_mem_k, cache_mem_v, page_table, g_attn, w_in,
           da_lam, da_subln, rel_bias, sg_norm, sg_w, sg_b, g_mem, w_mem_kv, w_branch, w_out, g_ffn, w_up, w_down,
           g_final):
    B, S, D = x_prompt.shape
    N, n_tok, _ = x_sample.shape
    depth, n_pool, page = cache_da_k.shape[:3]
    n_mem = mem_prompt.shape[1]
    n_pages = page_table.shape[1]
    width = DA_HEADS * DA_DV
    assert D == width == SG_GROUPS * CHUNK == MX_HEADS * MX_DH and page == CHUNK and n_tok <= CHUNK

    t_attn = min(512, S)
    pps = max(d for d in range(1, MAX_PAGES_PER_STEP + 1) if n_pages % d == 0)
    spb_da = 1
    spb_mx = 4 if N % 4 == 0 else 1
    ride_mx = N % (B * DA_HEADS) == 0 and N // (B * DA_HEADS) <= spb_mx
    tm_p = 256
    tm_s = min(256, N * n_tok)
    tm_post = 512
    sec = {n: i for i, n in enumerate(_IN_SECTIONS)}
    in_sections = [(_IN_ACTS[n], _IN_COPIES.get(n)) for n in _IN_SECTIONS]

    xp = x_prompt.reshape(B * S, D)
    xs = x_sample.reshape(N * n_tok, D)
    mem = mem_prompt.reshape(B * n_mem, D)
    ck = cache_da_k.reshape(depth * n_pool, page, DA_HEADS, DA_DV)
    cv = cache_da_v.reshape(depth * n_pool, page, DA_HEADS, DA_DV)
    cmk = cache_mem_k.reshape(depth * N, n_mem, MX_HEADS, MX_DH)
    cmv = cache_mem_v.reshape(depth * N, n_mem, MX_HEADS, MX_DH)
    row = lambda a: a.reshape(1, -1)

    near_bias = _prompt_bias(rel_bias)
    seg_s = n_tok
    outs = {k: [] for k in ("dkp", "dvp", "dks", "dvs", "mkp", "mvp", "sgs")}
    for l in range(depth):
        lam_init = 0.8 - 0.6 * math.exp(-0.3 * l)
        w_in_l = w_in[l].astype(BF16)
        wb_l, wo_l = w_branch[l].astype(BF16), w_out[l].astype(BF16)
        wu_l, wd_l = w_up[l].astype(BF16), w_down[l].astype(BF16)
        far, last, new, lam = _sample_bias(rel_bias, da_lam[l], n_tok, page, lam_init)
        subln = row(da_subln[l])
        sgb_p = jnp.repeat(sg_b[l].T, CHUNK, axis=1)

        zs, ks32, vs32 = _norm_matmul(xs, row(g_attn[l]), w_in_l, row(sg_norm[l]), in_sections, F32, tm_s,
                                      "inproj_sample", copy_heads=DA_HEADS)
        zs, xp_l = lax.optimization_barrier((zs, xp))
        zp, kp32, vp32 = _norm_matmul(xp_l, row(g_attn[l]), w_in_l, row(sg_norm[l]), in_sections, BF16, tm_p,
                                      "inproj_prompt")
        if ride_mx:
            oda_p, omx_s = _attn_prompt(zp, lam, near_bias, subln, B, S, t_attn, lam_init,
                                        rider=(zs, cmk, cmv, N, n_tok, sec["m"], l * N))
        else:
            oda_p = _attn_prompt(zp, lam, near_bias, subln, B, S, t_attn, lam_init)
            omx_s = _xattn_sample(zs, cmk, cmv, N, n_tok, sec["m"], l * N, spb_mx)
        oda_p, omx_s, zp, zs, mem_l = lax.optimization_barrier((oda_p, omx_s, zp, zs, mem))
        oda_s = _attn_sample(zs, ks32, vs32, page_table + l * n_pool, ck, cv, lam, far, last, new, subln, N, n_tok,
                             lam_init, spb_da, pps)

        mkv16, mk32, mv32 = _norm_matmul(mem_l, row(g_mem[l]), w_mem_kv[l].astype(BF16), row(sg_norm[l]),
                                         [(("none",), 0), (("none",), 1)], BF16, min(512, B * n_mem), "mem_kv",
                                         copy_heads=MX_HEADS)
        osg_p = _sgate(zp, sg_w[l], sgb_p, sec["u"], sec["s"], 8)
        omx_p = _xattn_prompt(zp, mkv16, B, S, 1024 if S % 1024 == 0 else 512, sec["m"])
        osg_s = _sgate_seq(zs, sg_w[l], sg_b[l], seg_s, sec["u"], sec["s"])

        xs = _post(xs, oda_s, osg_s, omx_s, zs, sec["g0"] // N_BRANCH, wb_l, wo_l, row(g_ffn[l]), wu_l, wd_l,
                   row(g_final), tm_s, l == depth - 1)
        xp = _post(xp, oda_p, osg_p, omx_p, zp, sec["g0"] // N_BRANCH, wb_l, wo_l, row(g_ffn[l]), wu_l, wd_l,
                   row(g_final), tm_post, l == depth - 1)

        outs["dkp"].append(kp32.reshape(B, S, DA_HEADS, 2 * DA_DK))
        outs["dvp"].append(vp32.reshape(B, S, DA_HEADS, DA_DV))
        outs["dks"].append(ks32.reshape(N, n_tok, DA_HEADS, 2 * DA_DK))
        outs["dvs"].append(vs32.reshape(N, n_tok, DA_HEADS, DA_DV))
        outs["mkp"].append(mk32.reshape(B, n_mem, MX_HEADS, MX_DH))
        outs["mvp"].append(mv32.reshape(B, n_mem, MX_HEADS, MX_DH))
        outs["sgs"].append(zs[:, sec["s"] * D:(sec["s"] + 1) * D].reshape(N, n_tok, D))

    return (xp.reshape(B, S, D), xs.reshape(N, n_tok, D), jnp.stack(outs["dkp"]), jnp.stack(outs["dvp"]),
            jnp.stack(outs["dks"]), jnp.stack(outs["dvs"]), jnp.stack(outs["mkp"]), jnp.stack(outs["mvp"]),
            jnp.stack(outs["sgs"]))
```

```python
import functools
import math

import jax
import jax.numpy as jnp
from jax import lax
from jax.experimental import pallas as pl
from jax.experimental.pallas import tpu as pltpu

F32 = jnp.float32
BF16 = jnp.bfloat16

DA_HEADS = 8
DA_DK = 64
DA_DV = 2 * DA_DK
SG_GROUPS = 8
CHUNK = 128
MX_HEADS = 4
MX_DH = 256
N_BUCKETS = 32
MAX_DISTANCE = 128
N_BRANCH = 3
RMS_EPS = 1e-6
NEG_INF = -1e30
LOG2E = math.log2(math.e)

LANES = 128
BF16_SUBLANES = 16
VMEM_LIMIT_BYTES = 56 * 1024 * 1024
MAX_PAGES_PER_STEP = 16


def _cparams(*sem):
    return pltpu.CompilerParams(dimension_semantics=sem, vmem_limit_bytes=VMEM_LIMIT_BYTES)


def _rms(x, g):
    ms = jnp.mean(x * x, axis=-1, keepdims=True)
    return x * lax.rsqrt(ms + RMS_EPS) * g


def _div_pow2(x, n):
    assert n > 0 and n & (n - 1) == 0, n
    return x >> (n.bit_length() - 1)


def _nt_dot(a, b):
    return lax.dot_general(a, b, (((1,), (1,)), ((), ())), preferred_element_type=F32)


def _lane_tile(x, n):
    return x if n == 1 else jnp.concatenate([x] * n, axis=1)


def _norm_matmul_kernel(x_ref, g_ref, w_ref, sgn_ref, z_ref, *copy_refs, sections, tn):
    hn = _rms(x_ref[...], g_ref[...]).astype(BF16)
    for sec, (act, copy_idx) in enumerate(sections):
        cols = slice(sec * tn, (sec + 1) * tn)
        acc = jnp.dot(hn, w_ref[:, cols], preferred_element_type=F32)
        if act[0] == "scale":
            val = acc * act[1]
        elif act[0] == "gelu":
            val = jax.nn.gelu(acc)
        elif act[0] == "gelu_rms":
            val = _rms(jax.nn.gelu(acc), sgn_ref[...])
        elif act[0] == "sigmoid":
            val = jax.nn.sigmoid(acc)
        else:
            val = acc
        z_ref[:, cols] = val.astype(z_ref.dtype)
        if copy_idx is not None:
            cref = copy_refs[copy_idx]
            if len(cref.shape) == 2:
                cref[...] = val
            else:
                hd = cref.shape[2]
                for hh in range(cref.shape[1]):
                    cref[:, hh, :] = val[:, hh * hd:(hh + 1) * hd]


def _norm_matmul(x, g, w, sgn, sections, z_dtype, tm, name, copy_heads=1):
    T, D = x.shape
    n_sec = len(sections)
    tn = w.shape[1] // n_sec
    n_copy = sum(1 for _, c in sections if c is not None)
    out_shape = [jax.ShapeDtypeStruct((T, n_sec * tn), z_dtype)]
    out_specs = [pl.BlockSpec((tm, n_sec * tn), lambda i: (i, 0))]
    for _ in range(n_copy):
        if copy_heads == 1:
            out_shape.append(jax.ShapeDtypeStruct((T, tn), F32))
            out_specs.append(pl.BlockSpec((tm, tn), lambda i: (i, 0)))
        else:
            out_shape.append(jax.ShapeDtypeStruct((T, copy_heads, tn // copy_heads), F32))
            out_specs.append(pl.BlockSpec((tm, copy_heads, tn // copy_heads), lambda i: (i, 0, 0)))

    def resident(shape):
        return pl.BlockSpec(shape, lambda i: (0, 0), pipeline_mode=pl.Buffered(1))

    return pl.pallas_call(
        functools.partial(_norm_matmul_kernel, sections=tuple(sections), tn=tn),
        grid=(T // tm,),
        in_specs=[pl.BlockSpec((tm, D), lambda i: (i, 0)), resident((1, D)), resident(w.shape), resident((1, tn))],
        out_specs=out_specs,
        out_shape=out_shape,
        compiler_params=_cparams("parallel"),
        name=name,
    )(x, g, w, sgn)


def _t5_bias(dist, rb_ref, h):
    n = jnp.maximum(dist, 0)
    max_exact = N_BUCKETS // 2
    nf = jnp.maximum(n, 1).astype(F32)
    rel = jnp.log(nf / max_exact) / math.log(MAX_DISTANCE / max_exact) * (N_BUCKETS - max_exact)
    out = jnp.zeros(dist.shape, F32)
    for b in range(max_exact):
        out = jnp.where(n == b, rb_ref[b, h], out)
    out = jnp.where(n >= max_exact, rb_ref[max_exact, h], out)
    for b in range(max_exact + 1, N_BUCKETS):
        out = jnp.where(rel >= b - max_exact, rb_ref[b, h], out)
    return jnp.where(dist >= 0, out * LOG2E, NEG_INF)


def _far_bias(rb_ref, h):
    return rb_ref[N_BUCKETS - 1, h] * LOG2E


def _prompt_bias_kernel(rb_ref, near_ref):
    d = lax.broadcasted_iota(jnp.int32, (CHUNK, CHUNK), 0) - lax.broadcasted_iota(jnp.int32, (CHUNK, CHUNK), 1)
    for h in range(near_ref.shape[0]):
        far = _far_bias(rb_ref, h)
        near_ref[h, 0] = _t5_bias(d, rb_ref, h) - far
        near_ref[h, 1] = _t5_bias(d + CHUNK, rb_ref, h) - far


def _prompt_bias(rel_bias):
    assert MAX_DISTANCE <= CHUNK + 1
    H = rel_bias.shape[1]
    return pl.pallas_call(
        _prompt_bias_kernel,
        in_specs=[pl.BlockSpec(memory_space=pltpu.SMEM)],
        out_shape=jax.ShapeDtypeStruct((H, 2, CHUNK, CHUNK), F32),
        name="prompt_bias",
    )(rel_bias)


def _tile_bias(near_ref, rows, cols):
    fill = {-1: jnp.full((CHUNK, CHUNK), NEG_INF, F32), 2: jnp.zeros((CHUNK, CHUNK), F32)}

    def block(bd):
        return near_ref[bd] if 0 <= bd < 2 else fill[max(min(bd, 2), -1)]

    return jnp.concatenate([jnp.concatenate([block(bi - bj) for bj in range(cols // CHUNK)], axis=1)
                            for bi in range(rows // CHUNK)], axis=0)


def _sample_bias_kernel(rb_ref, lam_p_ref, far_ref, last_ref, new_ref, lam_ref, *, n_tok, page, lam_init):
    hr = 2 * n_tok

    def grid(cols):
        tok = lax.broadcasted_iota(jnp.int32, (hr, cols), 0) & (n_tok - 1)
        col = lax.broadcasted_iota(jnp.int32, (hr, cols), 1)
        return tok, col, _div_pow2(col, DA_HEADS), col & (DA_HEADS - 1)

    tok, _, key, kh = grid(page * DA_HEADS)
    tok_n, col_n, key_n, kh_n = grid(page)
    for h in range(DA_HEADS):
        sl = slice(h * hr, (h + 1) * hr)
        far_ref[sl, :] = jnp.where(kh == h, _far_bias(rb_ref, h), NEG_INF)
        last_ref[sl, :] = jnp.where(kh == h, _t5_bias(page + tok - key, rb_ref, h), NEG_INF)
        new_ref[sl, :] = jnp.where((kh_n == h) & (col_n < n_tok * DA_HEADS), _t5_bias(tok_n - key_n, rb_ref, h), NEG_INF)
    lp = lam_p_ref[...]
    s1 = jnp.sum(lp[0:1, :] * lp[1:2, :], axis=-1, keepdims=True)
    s2 = jnp.sum(lp[2:3, :] * lp[3:4, :], axis=-1, keepdims=True)
    lam = jnp.exp(s1) - jnp.exp(s2) + lam_init
    lam_ref[...] = jnp.broadcast_to(lam, lam_ref.shape)


def _sample_bias(rel_bias, da_lam_l, n_tok, page, lam_init):
    assert n_tok & (n_tok - 1) == 0 and n_tok * DA_HEADS <= page
    rows = DA_HEADS * 2 * n_tok
    wide = jax.ShapeDtypeStruct((rows, page * DA_HEADS), F32)
    return pl.pallas_call(
        functools.partial(_sample_bias_kernel, n_tok=n_tok, page=page, lam_init=lam_init),
        in_specs=[pl.BlockSpec(memory_space=pltpu.SMEM), pl.BlockSpec(memory_space=pltpu.VMEM)],
        out_shape=[wide, wide, jax.ShapeDtypeStruct((rows, page), F32), jax.ShapeDtypeStruct((8, LANES), F32)],
        name="sample_bias",
    )(rel_bias, da_lam_l)


def _split_components(q):
    lane = lax.broadcasted_iota(jnp.int32, q.shape, 1)
    zero = jnp.zeros_like(q)
    return jnp.concatenate([jnp.where(lane < DA_DK, q, zero), jnp.where(lane >= DA_DK, q, zero)], axis=0)


def _diff_out(o1, o2, lam_row, subln, lam_init):
    o = o1 - lam_row * o2
    return _rms(o, subln) * (1.0 - lam_init)


def _softmax_update(m, acc, s_list, v_list, mxu_sums):
    m_new = m
    for s in s_list:
        m_new = jnp.maximum(m_new, jnp.max(s, axis=-1, keepdims=True))
    acc = acc * _lane_tile(jnp.exp2(m - m_new), 2)
    for s, v in zip(s_list, v_list):
        e = jnp.exp2(s - _lane_tile(m_new, s.shape[1] // LANES))
        if mxu_sums:
            v1 = jnp.concatenate([v, jnp.ones(v.shape, BF16)], axis=1)
            acc = acc + jnp.dot(e.astype(BF16), v1, preferred_element_type=F32)
        else:
            pv = jnp.dot(e.astype(BF16), v, preferred_element_type=F32)
            l = jnp.broadcast_to(jnp.sum(e, axis=-1, keepdims=True), pv.shape)
            acc = acc + jnp.concatenate([pv, l], axis=1)
    return m_new, acc


def _attn_prompt_kernel(lam_ref, q_ref, k_ref, v_ref, near_ref, sub_ref, *rest, t, nq, lam_init, riders):
    n_in = sum(n for _, n in riders)
    pos = 0
    for idx, (body, n) in enumerate(riders):
        body(*rest[pos:pos + n], rest[n_in + 1 + idx])
        pos += n
    _prompt_tiles(lam_ref, q_ref, k_ref, v_ref, near_ref, sub_ref, rest[n_in], t, range(nq), lam_init)


def _prompt_tiles(lam_ref, q_ref, k_ref, v_ref, near_ref, sub_ref, o_ref, t, tiles, lam_init):
    hb = t // 2
    near1 = near_ref[1]
    bias_a = _tile_bias(near_ref, t, hb)
    bias_b = _tile_bias(near_ref, hb, hb)

    def both(s, b):
        n = s.shape[0] // 2
        return jnp.concatenate([s[:n] + b, s[n:] + b], axis=0)

    def sub_fix(sr):
        return jnp.concatenate([sr[:, :t - CHUNK], sr[:, t - CHUNK:] + near1], axis=1)

    def second_half(x):
        return jnp.concatenate([x[hb:t], x[t + hb:]], axis=0)

    def put_second_half(x, xb):
        return jnp.concatenate([x[:hb], xb[:hb], x[t:t + hb], xb[hb:]], axis=0)

    for qi in tiles:
        q0 = qi * t
        r0 = (qi - tiles[0]) * t
        qq = _split_components(q_ref[r0:r0 + t, :])
        m = jnp.full((2 * t, LANES), -jnp.inf, F32)
        acc = jnp.zeros((2 * t, 2 * LANES), F32)
        for kt in range(qi):
            keys = slice(kt * t, (kt + 1) * t)
            s = _nt_dot(qq, k_ref[keys, :])
            if kt == qi - 1:
                s = jnp.concatenate([sub_fix(s[:CHUNK]), s[CHUNK:t], sub_fix(s[t:t + CHUNK]), s[t + CHUNK:]], axis=0)
            m, acc = _softmax_update(m, acc, [s], [v_ref[keys, :]], True)
        keys = slice(q0, q0 + hb)
        m, acc = _softmax_update(m, acc, [both(_nt_dot(qq, k_ref[keys, :]), bias_a)], [v_ref[keys, :]], True)
        keys = slice(q0 + hb, q0 + t)
        s = both(_nt_dot(second_half(qq), k_ref[keys, :]), bias_b)
        _, acc_b = _softmax_update(second_half(m), second_half(acc), [s], [v_ref[keys, :]], True)
        acc = put_second_half(acc, acc_b)
        o = acc[:, :DA_DV] / acc[:, DA_DV:]
        out = _diff_out(o[:t], o[t:], lam_ref[0:1, :], sub_ref[...], lam_init)
        o_ref[r0:r0 + t, :] = out.astype(o_ref.dtype)


def _xattn_rider(zs, mem_k, mem_v, N, n_tok, q_sec, seq_off, steps):
    spb = N // steps
    n_mem = mem_k.shape[1]
    width = MX_HEADS * MX_DH
    assert N == spb * steps and seq_off % spb == 0 and (MX_HEADS * n_tok) % BF16_SUBLANES == 0
    mem_spec = pl.BlockSpec((spb, n_mem, MX_HEADS, MX_DH), lambda b, h: (seq_off // spb + b * DA_HEADS + h, 0, 0, 0))
    return dict(
        body=functools.partial(_xattn_sample_kernel, spb=spb, n_tok=n_tok),
        in_specs=[pl.BlockSpec((spb * n_tok, width), lambda b, h: (b * DA_HEADS + h, q_sec)), mem_spec, mem_spec],
        operands=[zs, mem_k, mem_v],
        out_spec=pl.BlockSpec((spb * n_tok, width), lambda b, h: (b * DA_HEADS + h, 0)),
        out_shape=jax.ShapeDtypeStruct((N * n_tok, width), F32))


def _sgate_rider(z, w, b, u_sec, vn_sec, steps):
    T = z.shape[0]
    width = b.shape[1]
    rb = T // steps
    assert T == rb * steps and rb % CHUNK == 0
    return dict(
        body=functools.partial(_sgate_kernel, seg=CHUNK, n_chunk=rb // CHUNK),
        in_specs=[pl.BlockSpec((rb, width), lambda b_, h: (b_ * DA_HEADS + h, u_sec)),
                  pl.BlockSpec((rb, width), lambda b_, h: (b_ * DA_HEADS + h, vn_sec)),
                  pl.BlockSpec((SG_GROUPS, CHUNK, CHUNK), lambda b_, h: (0, 0, 0)),
                  pl.BlockSpec((CHUNK, width), lambda b_, h: (0, 0))],
        operands=[z, z, w, b],
        out_spec=pl.BlockSpec((rb, width), lambda b_, h: (b_ * DA_HEADS + h, 0)),
        out_shape=jax.ShapeDtypeStruct((T, width), BF16))


def _attn_prompt(z, lam, near, subln, B, S, t, lam_init, riders=()):
    assert S % t == 0 and t % (2 * CHUNK) == 0 and MAX_DISTANCE <= CHUNK + 1
    H = DA_HEADS
    in_specs = [
        pl.BlockSpec((8, LANES), lambda b, h: (0, 0)),
        pl.BlockSpec((S, DA_DV), lambda b, h: (b, h)),
        pl.BlockSpec((S, DA_DV), lambda b, h: (b, H + h)),
        pl.BlockSpec((S, DA_DV), lambda b, h: (b, 2 * H + h)),
        pl.BlockSpec((None, 2, CHUNK, CHUNK), lambda b, h: (h, 0, 0, 0)),
        pl.BlockSpec((1, DA_DV), lambda b, h: (0, 0)),
    ]
    operands = [lam, z, z, z, near, subln]
    for r in riders:
        in_specs += r["in_specs"]
        operands += r["operands"]
    return pl.pallas_call(
        functools.partial(_attn_prompt_kernel, t=t, nq=S // t, lam_init=lam_init,
                          riders=tuple((r["body"], len(r["in_specs"])) for r in riders)),
        grid=(B, H),
        in_specs=in_specs,
        out_specs=[pl.BlockSpec((S, DA_DV), lambda b, h: (b, h))] + [r["out_spec"] for r in riders],
        out_shape=[jax.ShapeDtypeStruct((B * S, H * DA_DV), BF16)] + [r["out_shape"] for r in riders],
        compiler_params=_cparams("parallel", "arbitrary"),
        name="attn_prompt",
    )(*operands)


def _attn_sample_kernel(pt_ref, lam_ref, q_ref, kn_ref, vn_ref, far_ref, last_ref, new_ref, sub_ref, *rest,
                        spb, pps, n_tok, page, lam_init):
    n_in = spb * pps
    kp_refs, vp_refs = rest[:n_in], rest[n_in:2 * n_in]
    o_ref, qq_ref, m_ref, acc_ref = rest[2 * n_in:]
    p = pl.program_id(1)
    last = pl.num_programs(1) - 1
    hr = 2 * n_tok
    flat = page * DA_HEADS

    @pl.when(p == 0)
    def _():
        for j in range(spb):
            q = q_ref[j * n_tok:(j + 1) * n_tok, :]
            for h in range(DA_HEADS):
                qq_ref[j, h * hr:(h + 1) * hr, :] = _split_components(q[:, h * DA_DV:(h + 1) * DA_DV]).astype(BF16)
        m_ref[...] = jnp.full(m_ref.shape, -jnp.inf, F32)
        acc_ref[...] = jnp.zeros(acc_ref.shape, F32)

    far = far_ref[...]
    tail = jnp.where(p == last, last_ref[...], far)
    for j in range(spb):
        qq = qq_ref[j]
        s_list, v_list = [], []
        for r in range(pps):
            s = _nt_dot(qq, kp_refs[j * pps + r][...].reshape(flat, DA_DV).astype(BF16))
            s_list.append(s + (tail if r == pps - 1 else far))
            v_list.append(vp_refs[j * pps + r][...].reshape(flat, DA_DV).astype(BF16))
        m, acc = _softmax_update(m_ref[j], acc_ref[j], s_list, v_list, False)
        m_ref[j] = m
        acc_ref[j] = acc

    @pl.when(p == last)
    def _():
        pad = jnp.zeros((page - n_tok * DA_HEADS, DA_DV), F32)
        for j in range(spb):
            toks = slice(j * n_tok, (j + 1) * n_tok)

            def new_rows(ref):
                return jnp.concatenate([ref[toks].reshape(n_tok * DA_HEADS, DA_DV), pad], axis=0).astype(BF16)

            s_new = _nt_dot(qq_ref[j], new_rows(kn_ref)) + new_ref[...]
            _, acc2 = _softmax_update(m_ref[j], acc_ref[j], [s_new], [new_rows(vn_ref)], False)
            o = acc2[:, :DA_DV] / acc2[:, DA_DV:]
            for h in range(DA_HEADS):
                oh = o[h * hr:(h + 1) * hr]
                out = _diff_out(oh[:n_tok], oh[n_tok:], lam_ref[0:1, :], sub_ref[...], lam_init)
                o_ref[toks, h * DA_DV:(h + 1) * DA_DV] = out.astype(o_ref.dtype)


def _attn_sample(z, k_new, v_new, page_idx, cache_k, cache_v, lam, far, last, new, subln, N, n_tok, lam_init, spb, pps):
    n_pages = page_idx.shape[1]
    page = cache_k.shape[1]
    width = DA_HEADS * DA_DV
    rows = DA_HEADS * 2 * n_tok
    assert (2 * n_tok) % BF16_SUBLANES == 0 and n_pages % pps == 0 and N % spb == 0

    def const(shape):
        return pl.BlockSpec(shape, lambda n, p, pt: (0,) * len(shape))

    def page_spec(j, r):
        return pl.BlockSpec((None, page, DA_HEADS, DA_DV), lambda n, p, pt: (pt[n * spb + j, p * pps + r], 0, 0, 0))

    page_specs = [page_spec(j, r) for j in range(spb) for r in range(pps)]
    new_spec = pl.BlockSpec((spb * n_tok, DA_HEADS, DA_DV), lambda n, p, pt: (n, 0, 0))
    grid_spec = pltpu.PrefetchScalarGridSpec(
        num_scalar_prefetch=1,
        grid=(N // spb, n_pages // pps),
        in_specs=[
            const((8, LANES)),
            pl.BlockSpec((spb * n_tok, width), lambda n, p, pt: (n, 0)),
            new_spec, new_spec,
            const((rows, page * DA_HEADS)), const((rows, page * DA_HEADS)), const((rows, page)),
            const((1, DA_DV)),
        ] + page_specs + page_specs,
        out_specs=pl.BlockSpec((spb * n_tok, width), lambda n, p, pt: (n, 0)),
        scratch_shapes=[pltpu.VMEM((spb, rows, DA_DV), BF16), pltpu.VMEM((spb, rows, LANES), F32),
                        pltpu.VMEM((spb, rows, 2 * LANES), F32)],
    )
    n_in = spb * pps
    return pl.pallas_call(
        functools.partial(_attn_sample_kernel, spb=spb, pps=pps, n_tok=n_tok, page=page, lam_init=lam_init),
        grid_spec=grid_spec,
        out_shape=jax.ShapeDtypeStruct((N * n_tok, width), F32),
        compiler_params=_cparams("parallel", "arbitrary"),
        name="attn_sample",
    )(page_idx, lam, z, k_new, v_new, far, last, new, subln, *([cache_k] * n_in), *([cache_v] * n_in))


def _sgate_kernel(u_ref, vn_ref, w_ref, b_ref, o_ref, *, seg, n_chunk):
    i = lax.broadcasted_iota(jnp.int32, (CHUNK, CHUNK), 0)
    j = lax.broadcasted_iota(jnp.int32, (CHUNK, CHUNK), 1)
    keep = (_div_pow2(i, seg) == _div_pow2(j, seg)) & (j <= i)
    gd = u_ref.shape[1] // SG_GROUPS
    for g in range(SG_GROUPS):
        w = jnp.where(keep, w_ref[g], 0.0).astype(BF16)
        cols = slice(g * gd, (g + 1) * gd)
        for c in range(n_chunk):
            rws = slice(c * CHUNK, (c + 1) * CHUNK)
            mix = jnp.dot(w, vn_ref[rws, cols].astype(BF16), preferred_element_type=F32) + b_ref[:, cols]
            o_ref[rws, cols] = (u_ref[rws, cols].astype(F32) * mix).astype(o_ref.dtype)


def _sgate(z, w, b, u_sec, vn_sec, n_chunk):
    T = z.shape[0]
    width = b.shape[1]
    rb = n_chunk * CHUNK
    return pl.pallas_call(
        functools.partial(_sgate_kernel, seg=CHUNK, n_chunk=n_chunk),
        grid=(T // rb,),
        in_specs=[
            pl.BlockSpec((rb, width), lambda i: (i, u_sec)),
            pl.BlockSpec((rb, width), lambda i: (i, vn_sec)),
            pl.BlockSpec((SG_GROUPS, CHUNK, CHUNK), lambda i: (0, 0, 0)),
            pl.BlockSpec((CHUNK, width), lambda i: (0, 0)),
        ],
        out_specs=pl.BlockSpec((rb, width), lambda i: (i, 0)),
        out_shape=jax.ShapeDtypeStruct((T, width), BF16),
        compiler_params=_cparams("parallel"),
        name="sgate",
    )(z, z, w, b)


def _sgate_seq_kernel(w_ref, b_ref, u_ref, vn_ref, o_ref, wbd_ref, bias_ref, *, seg):
    gd = u_ref.shape[1] // SG_GROUPS

    @pl.when(pl.program_id(0) == 0)
    def _():
        i = lax.broadcasted_iota(jnp.int32, (CHUNK, CHUNK), 0)
        j = lax.broadcasted_iota(jnp.int32, (CHUNK, CHUNK), 1)
        same = _div_pow2(i, seg) == _div_pow2(j, seg)
        ti, tj = i & (seg - 1), j & (seg - 1)
        for g in range(SG_GROUPS):
            w = jnp.zeros((CHUNK, CHUNK), F32)
            bias = jnp.zeros((CHUNK, gd), F32)
            for a in range(seg):
                bias = jnp.where(ti == a, b_ref[g, a], bias)
                for b in range(a + 1):
                    w = jnp.where((ti == a) & (tj == b), w_ref[g * seg + a, b], w)
            wbd_ref[g] = jnp.where(same, w, 0.0).astype(BF16)
            bias_ref[:, g * gd:(g + 1) * gd] = bias

    for g in range(SG_GROUPS):
        cols = slice(g * gd, (g + 1) * gd)
        mix = jnp.dot(wbd_ref[g], vn_ref[:, cols].astype(BF16), preferred_element_type=F32) + bias_ref[:, cols]
        o_ref[:, cols] = (u_ref[:, cols].astype(F32) * mix).astype(o_ref.dtype)


def _sgate_seq(z, w, b, seg, u_sec, vn_sec):
    T = z.shape[0]
    width = SG_GROUPS * CHUNK
    assert CHUNK % seg == 0 and seg & (seg - 1) == 0 and T % CHUNK == 0
    return pl.pallas_call(
        functools.partial(_sgate_seq_kernel, seg=seg),
        grid=(T // CHUNK,),
        in_specs=[
            pl.BlockSpec(memory_space=pltpu.SMEM), pl.BlockSpec(memory_space=pltpu.SMEM),
            pl.BlockSpec((CHUNK, width), lambda i: (i, u_sec)),
            pl.BlockSpec((CHUNK, width), lambda i: (i, vn_sec)),
        ],
        out_specs=pl.BlockSpec((CHUNK, width), lambda i: (i, 0)),
        out_shape=jax.ShapeDtypeStruct((T, width), BF16),
        scratch_shapes=[pltpu.VMEM((SG_GROUPS, CHUNK, CHUNK), BF16), pltpu.VMEM((CHUNK, width), F32)],
        compiler_params=_cparams("arbitrary"),
        name="sgate_seq",
    )(w[:, :seg, :seg].reshape(SG_GROUPS * seg, seg), b[:, :seg], z, z)


def _xattn_prompt_kernel(q_ref, k_ref, v_ref, o_ref):
    heads = [slice(h * MX_DH, (h + 1) * MX_DH) for h in range(MX_HEADS)]
    scores = [_nt_dot(q_ref[:, cols], k_ref[:, cols]) for cols in heads]
    exps = [jnp.exp(s - jnp.max(s, axis=-1, keepdims=True)) for s in scores]
    for cols, e in zip(heads, exps):
        o = jnp.dot(e.astype(BF16), v_ref[:, cols], preferred_element_type=F32)
        o_ref[:, cols] = (o / jnp.sum(e, axis=-1, keepdims=True)).astype(o_ref.dtype)


def _xattn_prompt(z, mem_kv, B, S, tq, q_sec):
    nq = S // tq
    n_mem = mem_kv.shape[0] // B
    width = MX_HEADS * MX_DH
    return pl.pallas_call(
        _xattn_prompt_kernel,
        grid=(B, nq),
        in_specs=[
            pl.BlockSpec((tq, width), lambda b, i: (b * nq + i, q_sec)),
            pl.BlockSpec((n_mem, width), lambda b, i: (b, 0)),
            pl.BlockSpec((n_mem, width), lambda b, i: (b, 1)),
        ],
        out_specs=pl.BlockSpec((tq, width), lambda b, i: (b * nq + i, 0)),
        out_shape=jax.ShapeDtypeStruct((B * S, width), BF16),
        compiler_params=_cparams("parallel", "arbitrary"),
        name="xattn_prompt",
    )(z, mem_kv, mem_kv)


def _xattn_sample_kernel(q_ref, k_ref, v_ref, o_ref, *, spb, n_tok):
    n_mem = k_ref.shape[1]
    rows, flat = MX_HEADS * n_tok, n_mem * MX_HEADS
    r = lax.broadcasted_iota(jnp.int32, (rows, flat), 0)
    c = lax.broadcasted_iota(jnp.int32, (rows, flat), 1)
    own = (c & (MX_HEADS - 1)) == _div_pow2(r, n_tok)
    for j in range(spb):
        toks = slice(j * n_tok, (j + 1) * n_tok)
        q = q_ref[toks, :]
        qq = jnp.concatenate([q[:, h * MX_DH:(h + 1) * MX_DH] for h in range(MX_HEADS)], axis=0).astype(BF16)
        s = jnp.where(own, _nt_dot(qq, k_ref[j].reshape(flat, MX_DH).astype(BF16)), NEG_INF)
        e = jnp.exp(s - jnp.max(s, axis=-1, keepdims=True))
        o = jnp.dot(e.astype(BF16), v_ref[j].reshape(flat, MX_DH).astype(BF16), preferred_element_type=F32)
        o = o / jnp.sum(e, axis=-1, keepdims=True)
        for h in range(MX_HEADS):
            o_ref[toks, h * MX_DH:(h + 1) * MX_DH] = o[h * n_tok:(h + 1) * n_tok].astype(o_ref.dtype)


def _xattn_sample(z, mem_k, mem_v, N, n_tok, q_sec, seq_off, spb):
    n_mem = mem_k.shape[1]
    width = MX_HEADS * MX_DH
    assert (MX_HEADS * n_tok) % BF16_SUBLANES == 0 and N % spb == 0 and seq_off % spb == 0
    mem_spec = pl.BlockSpec((spb, n_mem, MX_HEADS, MX_DH), lambda n: (seq_off // spb + n, 0, 0, 0))
    return pl.pallas_call(
        functools.partial(_xattn_sample_kernel, spb=spb, n_tok=n_tok),
        grid=(N // spb,),
        in_specs=[pl.BlockSpec((spb * n_tok, width), lambda n: (n, q_sec)), mem_spec, mem_spec],
        out_specs=pl.BlockSpec((spb * n_tok, width), lambda n: (n, 0)),
        out_shape=jax.ShapeDtypeStruct((N * n_tok, width), F32),
        compiler_params=_cparams("parallel"),
        name="xattn_sample",
    )(z, mem_k, mem_v)


def _post_kernel(x_ref, da_ref, sg_ref, mx_ref, gate_ref, wb_ref, wo_ref, gf_ref, wu_ref, wd_ref, gfin_ref,
                 o_ref, y_ref, *, d, d_ff, fc, n_sub, final_norm):
    branches = (da_ref, sg_ref, mx_ref)
    rs = x_ref.shape[0] // n_sub
    subs = [slice(sb * rs, (sb + 1) * rs) for sb in range(n_sub)]
    merged = [None] * n_sub
    for k in range(N_BRANCH):
        for i, rws in enumerate(subs):
            br = jnp.dot(branches[k][rws, :].astype(BF16), wb_ref[k * d:(k + 1) * d, :], preferred_element_type=F32)
            term = gate_ref[rws, k * d:(k + 1) * d].astype(F32) * br
            merged[i] = term if merged[i] is None else merged[i] + term
    h2 = []
    for i, rws in enumerate(subs):
        x1 = x_ref[rws, :] + jnp.dot(merged[i].astype(BF16), wo_ref[...], preferred_element_type=F32)
        y_ref[rws, :] = x1
        h2.append(_rms(x1, gf_ref[...]).astype(BF16))
    for c in range(d_ff // fc):
        up = [(jnp.dot(h, wu_ref[:, c * fc:(c + 1) * fc], preferred_element_type=F32),
               jnp.dot(h, wu_ref[:, d_ff + c * fc:d_ff + (c + 1) * fc], preferred_element_type=F32)) for h in h2]
        for (a, b), rws in zip(up, subs):
            act = (jax.nn.silu(a) * b).astype(BF16)
            y_ref[rws, :] += jnp.dot(act, wd_ref[c * fc:(c + 1) * fc, :], preferred_element_type=F32)
    for rws in subs:
        y = y_ref[rws, :]
        o_ref[rws, :] = _rms(y, gfin_ref[...]) if final_norm else y


def _post(x, o_da, o_sg, o_mx, z, gate_sec, wb, wo, g_ffn, wu, wd, g_final, tm, final_norm):
    T, d = x.shape
    d_ff = wd.shape[0]
    fc = 256
    n_sub = 2 if tm >= 512 else 1
    row = lambda i: (i, 0)
    whole = lambda i: (0, 0)

    def resident(shape):
        return pl.BlockSpec(shape, whole, pipeline_mode=pl.Buffered(1))

    return pl.pallas_call(
        functools.partial(_post_kernel, d=d, d_ff=d_ff, fc=fc, n_sub=n_sub, final_norm=final_norm),
        grid=(T // tm,),
        in_specs=[
            pl.BlockSpec((tm, d), row), pl.BlockSpec((tm, d), row), pl.BlockSpec((tm, d), row), pl.BlockSpec((tm, d), row),
            pl.BlockSpec((tm, N_BRANCH * d), lambda i: (i, gate_sec)),
            resident(wb.shape), resident(wo.shape), resident((1, d)), resident(wu.shape), resident(wd.shape),
            resident((1, d)),
        ],
        out_specs=pl.BlockSpec((tm, d), row),
        out_shape=jax.ShapeDtypeStruct((T, d), F32),
        scratch_shapes=[pltpu.VMEM((tm, d), F32)],
        compiler_params=_cparams("parallel"),
        name="post",
    )(x, o_da, o_sg, o_mx, z, wb, wo, g_ffn, wu, wd, g_final)


_IN_SECTIONS = ("q", "k", "v", "u", "s", "m", "g0", "g1", "g2")
_IN_ACTS = {
    "q": ("scale", DA_DK ** -0.5 * LOG2E), "k": ("none",), "v": ("none",), "u": ("gelu",), "s": ("gelu_rms",),
    "m": ("scale", MX_DH ** -0.5), "g0": ("sigmoid",), "g1": ("sigmoid",), "g2": ("sigmoid",),
}
_IN_COPIES = {"k": 0, "v": 1}


def kernel(x_prompt, x_sample, mem_prompt, cache_da_k, cache_da_v, cache_mem_k, cache_mem_v, page_table, g_attn, w_in,
           da_lam, da_subln, rel_bias, sg_norm, sg_w, sg_b, g_mem, w_mem_kv, w_branch, w_out, g_ffn, w_up, w_down,
           g_final):
    B, S, D = x_prompt.shape
    N, n_tok, _ = x_sample.shape
    depth, n_pool, page = cache_da_k.shape[:3]
    n_mem = mem_prompt.shape[1]
    n_pages = page_table.shape[1]
    width = DA_HEADS * DA_DV
    assert D == width == SG_GROUPS * CHUNK == MX_HEADS * MX_DH and page == CHUNK and n_tok <= CHUNK

    t_attn = min(512, S)
    pps = max(d for d in range(1, MAX_PAGES_PER_STEP + 1) if n_pages % d == 0)
    spb_da = 1
    spb_mx = 4 if N % 4 == 0 else 1
    ride_mx = N % (B * DA_HEADS) == 0 and N // (B * DA_HEADS) <= spb_mx
    ride_sg = S % (DA_HEADS * CHUNK) == 0
    tm_p = 256
    tm_s = min(256, N * n_tok)
    tm_post = 512
    sec = {n: i for i, n in enumerate(_IN_SECTIONS)}
    in_sections = [(_IN_ACTS[n], _IN_COPIES.get(n)) for n in _IN_SECTIONS]

    xp = x_prompt.reshape(B * S, D)
    xs = x_sample.reshape(N * n_tok, D)
    mem = mem_prompt.reshape(B * n_mem, D)
    ck = cache_da_k.reshape(depth * n_pool, page, DA_HEADS, DA_DV)
    cv = cache_da_v.reshape(depth * n_pool, page, DA_HEADS, DA_DV)
    cmk = cache_mem_k.reshape(depth * N, n_mem, MX_HEADS, MX_DH)
    cmv = cache_mem_v.reshape(depth * N, n_mem, MX_HEADS, MX_DH)
    row = lambda a: a.reshape(1, -1)

    near_bias = _prompt_bias(rel_bias)
    seg_s = n_tok
    outs = {k: [] for k in ("dkp", "dvp", "dks", "dvs", "mkp", "mvp", "sgs")}
    for l in range(depth):
        lam_init = 0.8 - 0.6 * math.exp(-0.3 * l)
        w_in_l = w_in[l].astype(BF16)
        wb_l, wo_l = w_branch[l].astype(BF16), w_out[l].astype(BF16)
        wu_l, wd_l = w_up[l].astype(BF16), w_down[l].astype(BF16)
        far, last, new, lam = _sample_bias(rel_bias, da_lam[l], n_tok, page, lam_init)
        subln = row(da_subln[l])
        sgb_p = jnp.repeat(sg_b[l].T, CHUNK, axis=1)

        zs, ks32, vs32 = _norm_matmul(xs, row(g_attn[l]), w_in_l, row(sg_norm[l]), in_sections, F32, tm_s,
                                      "inproj_sample", copy_heads=DA_HEADS)
        zs, xp_l = lax.optimization_barrier((zs, xp))
        zp, kp32, vp32 = _norm_matmul(xp_l, row(g_attn[l]), w_in_l, row(sg_norm[l]), in_sections, BF16, tm_p,
                                      "inproj_prompt")
        steps = B * DA_HEADS
        riders = []
        if ride_mx:
            riders.append(_xattn_rider(zs, cmk, cmv, N, n_tok, sec["m"], l * N, steps))
        if ride_sg:
            riders.append(_sgate_rider(zp, sg_w[l], sgb_p, sec["u"], sec["s"], steps))
        oda_p, *rode = _attn_prompt(zp, lam, near_bias, subln, B, S, t_attn, lam_init, riders)
        omx_s = rode.pop(0) if ride_mx else _xattn_sample(zs, cmk, cmv, N, n_tok, sec["m"], l * N, spb_mx)
        osg_p = rode.pop(0) if ride_sg else _sgate(zp, sg_w[l], sgb_p, sec["u"], sec["s"], 8)
        oda_p, omx_s, osg_p, zp, zs, mem_l = lax.optimization_barrier((oda_p, omx_s, osg_p, zp, zs, mem))
        oda_s = _attn_sample(zs, ks32, vs32, page_table + l * n_pool, ck, cv, lam, far, last, new, subln, N, n_tok,
                             lam_init, spb_da, pps)

        mkv16, mk32, mv32 = _norm_matmul(mem_l, row(g_mem[l]), w_mem_kv[l].astype(BF16), row(sg_norm[l]),
                                         [(("none",), 0), (("none",), 1)], BF16, min(512, B * n_mem), "mem_kv",
                                         copy_heads=MX_HEADS)
        omx_p = _xattn_prompt(zp, mkv16, B, S, 1024 if S % 1024 == 0 else 512, sec["m"])
        osg_s = _sgate_seq(zs, sg_w[l], sg_b[l], seg_s, sec["u"], sec["s"])

        xs = _post(xs, oda_s, osg_s, omx_s, zs, sec["g0"] // N_BRANCH, wb_l, wo_l, row(g_ffn[l]), wu_l, wd_l,
                   row(g_final), tm_s, l == depth - 1)
        xp = _post(xp, oda_p, osg_p, omx_p, zp, sec["g0"] // N_BRANCH, wb_l, wo_l, row(g_ffn[l]), wu_l, wd_l,
                   row(g_final), tm_post, l == depth - 1)

        outs["dkp"].append(kp32.reshape(B, S, DA_HEADS, 2 * DA_DK))
        outs["dvp"].append(vp32.reshape(B, S, DA_HEADS, DA_DV))
        outs["dks"].append(ks32.reshape(N, n_tok, DA_HEADS, 2 * DA_DK))
        outs["dvs"].append(vs32.reshape(N, n_tok, DA_HEADS, DA_DV))
        outs["mkp"].append(mk32.reshape(B, n_mem, MX_HEADS, MX_DH))
        outs["mvp"].append(mv32.reshape(B, n_mem, MX_HEADS, MX_DH))
        outs["sgs"].append(zs[:, sec["s"] * D:(sec["s"] + 1) * D].reshape(N, n_tok, D))

    return (xp.reshape(B, S, D), xs.reshape(N, n_tok, D), jnp.stack(outs["dkp"]), jnp.stack(outs["dvp"]),
            jnp.stack(outs["dks"]), jnp.stack(outs["dvs"]), jnp.stack(outs["mkp"]), jnp.stack(outs["mvp"]),
            jnp.stack(outs["sgs"]))
```

```python
import functools
import math

import jax
import jax.numpy as jnp
from jax import lax
from jax.experimental import pallas as pl
from jax.experimental.pallas import tpu as pltpu

F32 = jnp.float32
BF16 = jnp.bfloat16

DA_HEADS = 8
DA_DK = 64
DA_DV = 2 * DA_DK
SG_GROUPS = 8
CHUNK = 128
MX_HEADS = 4
MX_DH = 256
N_BUCKETS = 32
MAX_DISTANCE = 128
N_BRANCH = 3
RMS_EPS = 1e-6
NEG_INF = -1e30
LOG2E = math.log2(math.e)

LANES = 128
BF16_SUBLANES = 16
VMEM_LIMIT_BYTES = 56 * 1024 * 1024
MAX_PAGES_PER_STEP = 16


def _cparams(*sem):
    return pltpu.CompilerParams(dimension_semantics=sem, vmem_limit_bytes=VMEM_LIMIT_BYTES)


def _rms(x, g):
    ms = jnp.mean(x * x, axis=-1, keepdims=True)
    return x * lax.rsqrt(ms + RMS_EPS) * g


def _div_pow2(x, n):
    assert n > 0 and n & (n - 1) == 0, n
    return x >> (n.bit_length() - 1)


def _nt_dot(a, b):
    return lax.dot_general(a, b, (((1,), (1,)), ((), ())), preferred_element_type=F32)


def _lane_tile(x, n):
    return x if n == 1 else jnp.concatenate([x] * n, axis=1)


def _norm_matmul_kernel(x_ref, g_ref, w_ref, sgn_ref, z_ref, *copy_refs, sections, tn):
    hn = _rms(x_ref[...], g_ref[...]).astype(BF16)
    for sec, (act, copy_idx) in enumerate(sections):
        cols = slice(sec * tn, (sec + 1) * tn)
        acc = jnp.dot(hn, w_ref[:, cols], preferred_element_type=F32)
        if act[0] == "scale":
            val = acc * act[1]
        elif act[0] == "gelu":
            val = jax.nn.gelu(acc)
        elif act[0] == "gelu_rms":
            val = _rms(jax.nn.gelu(acc), sgn_ref[...])
        elif act[0] == "sigmoid":
            val = jax.nn.sigmoid(acc)
        else:
            val = acc
        z_ref[:, cols] = val.astype(z_ref.dtype)
        if copy_idx is not None:
            cref = copy_refs[copy_idx]
            if len(cref.shape) == 2:
                cref[...] = val
            else:
                hd = cref.shape[2]
                for hh in range(cref.shape[1]):
                    cref[:, hh, :] = val[:, hh * hd:(hh + 1) * hd]


def _norm_matmul(x, g, w, sgn, sections, z_dtype, tm, name, copy_heads=1):
    T, D = x.shape
    n_sec = len(sections)
    tn = w.shape[1] // n_sec
    n_copy = sum(1 for _, c in sections if c is not None)
    out_shape = [jax.ShapeDtypeStruct((T, n_sec * tn), z_dtype)]
    out_specs = [pl.BlockSpec((tm, n_sec * tn), lambda i: (i, 0))]
    for _ in range(n_copy):
        if copy_heads == 1:
            out_shape.append(jax.ShapeDtypeStruct((T, tn), F32))
            out_specs.append(pl.BlockSpec((tm, tn), lambda i: (i, 0)))
        else:
            out_shape.append(jax.ShapeDtypeStruct((T, copy_heads, tn // copy_heads), F32))
            out_specs.append(pl.BlockSpec((tm, copy_heads, tn // copy_heads), lambda i: (i, 0, 0)))

    def resident(shape):
        return pl.BlockSpec(shape, lambda i: (0, 0), pipeline_mode=pl.Buffered(1))

    return pl.pallas_call(
        functools.partial(_norm_matmul_kernel, sections=tuple(sections), tn=tn),
        grid=(T // tm,),
        in_specs=[pl.BlockSpec((tm, D), lambda i: (i, 0)), resident((1, D)), resident(w.shape), resident((1, tn))],
        out_specs=out_specs,
        out_shape=out_shape,
        compiler_params=_cparams("parallel"),
        name=name,
    )(x, g, w, sgn)


def _t5_bias(dist, rb_ref, h):
    n = jnp.maximum(dist, 0)
    max_exact = N_BUCKETS // 2
    nf = jnp.maximum(n, 1).astype(F32)
    rel = jnp.log(nf / max_exact) / math.log(MAX_DISTANCE / max_exact) * (N_BUCKETS - max_exact)
    out = jnp.zeros(dist.shape, F32)
    for b in range(max_exact):
        out = jnp.where(n == b, rb_ref[b, h], out)
    out = jnp.where(n >= max_exact, rb_ref[max_exact, h], out)
    for b in range(max_exact + 1, N_BUCKETS):
        out = jnp.where(rel >= b - max_exact, rb_ref[b, h], out)
    return jnp.where(dist >= 0, out * LOG2E, NEG_INF)


def _far_bias(rb_ref, h):
    return rb_ref[N_BUCKETS - 1, h] * LOG2E


def _prompt_bias_kernel(rb_ref, near_ref):
    d = lax.broadcasted_iota(jnp.int32, (CHUNK, CHUNK), 0) - lax.broadcasted_iota(jnp.int32, (CHUNK, CHUNK), 1)
    for h in range(near_ref.shape[0]):
        far = _far_bias(rb_ref, h)
        near_ref[h, 0] = _t5_bias(d, rb_ref, h) - far
        near_ref[h, 1] = _t5_bias(d + CHUNK, rb_ref, h) - far


def _prompt_bias(rel_bias):
    assert MAX_DISTANCE <= CHUNK + 1
    H = rel_bias.shape[1]
    return pl.pallas_call(
        _prompt_bias_kernel,
        in_specs=[pl.BlockSpec(memory_space=pltpu.SMEM)],
        out_shape=jax.ShapeDtypeStruct((H, 2, CHUNK, CHUNK), F32),
        name="prompt_bias",
    )(rel_bias)


def _tile_bias(near_ref, rows, cols):
    fill = {-1: jnp.full((CHUNK, CHUNK), NEG_INF, F32), 2: jnp.zeros((CHUNK, CHUNK), F32)}

    def block(bd):
        return near_ref[bd] if 0 <= bd < 2 else fill[max(min(bd, 2), -1)]

    return jnp.concatenate([jnp.concatenate([block(bi - bj) for bj in range(cols // CHUNK)], axis=1)
                            for bi in range(rows // CHUNK)], axis=0)


def _sample_bias_kernel(rb_ref, lam_p_ref, far_ref, last_ref, new_ref, lam_ref, *, n_tok, page, lam_init):
    hr = 2 * n_tok

    def grid(cols):
        tok = lax.broadcasted_iota(jnp.int32, (hr, cols), 0) & (n_tok - 1)
        col = lax.broadcasted_iota(jnp.int32, (hr, cols), 1)
        return tok, col, _div_pow2(col, DA_HEADS), col & (DA_HEADS - 1)

    tok, _, key, kh = grid(page * DA_HEADS)
    tok_n, col_n, key_n, kh_n = grid(page)
    for h in range(DA_HEADS):
        sl = slice(h * hr, (h + 1) * hr)
        far_ref[sl, :] = jnp.where(kh == h, _far_bias(rb_ref, h), NEG_INF)
        last_ref[sl, :] = jnp.where(kh == h, _t5_bias(page + tok - key, rb_ref, h), NEG_INF)
        new_ref[sl, :] = jnp.where((kh_n == h) & (col_n < n_tok * DA_HEADS), _t5_bias(tok_n - key_n, rb_ref, h), NEG_INF)
    lp = lam_p_ref[...]
    s1 = jnp.sum(lp[0:1, :] * lp[1:2, :], axis=-1, keepdims=True)
    s2 = jnp.sum(lp[2:3, :] * lp[3:4, :], axis=-1, keepdims=True)
    lam = jnp.exp(s1) - jnp.exp(s2) + lam_init
    lam_ref[...] = jnp.broadcast_to(lam, lam_ref.shape)


def _sample_bias(rel_bias, da_lam_l, n_tok, page, lam_init):
    assert n_tok & (n_tok - 1) == 0 and n_tok * DA_HEADS <= page
    rows = DA_HEADS * 2 * n_tok
    wide = jax.ShapeDtypeStruct((rows, page * DA_HEADS), F32)
    return pl.pallas_call(
        functools.partial(_sample_bias_kernel, n_tok=n_tok, page=page, lam_init=lam_init),
        in_specs=[pl.BlockSpec(memory_space=pltpu.SMEM), pl.BlockSpec(memory_space=pltpu.VMEM)],
        out_shape=[wide, wide, jax.ShapeDtypeStruct((rows, page), F32), jax.ShapeDtypeStruct((8, LANES), F32)],
        name="sample_bias",
    )(rel_bias, da_lam_l)


def _split_components(q):
    lane = lax.broadcasted_iota(jnp.int32, q.shape, 1)
    zero = jnp.zeros_like(q)
    return jnp.concatenate([jnp.where(lane < DA_DK, q, zero), jnp.where(lane >= DA_DK, q, zero)], axis=0)


def _diff_out(o1, o2, lam_row, subln, lam_init):
    o = o1 - lam_row * o2
    return _rms(o, subln) * (1.0 - lam_init)


def _softmax_update(m, acc, s_list, v_list, mxu_sums):
    m_new = m
    for s in s_list:
        m_new = jnp.maximum(m_new, jnp.max(s, axis=-1, keepdims=True))
    acc = acc * _lane_tile(jnp.exp2(m - m_new), 2)
    for s, v in zip(s_list, v_list):
        e = jnp.exp2(s - _lane_tile(m_new, s.shape[1] // LANES))
        if mxu_sums:
            v1 = jnp.concatenate([v, jnp.ones(v.shape, BF16)], axis=1)
            acc = acc + jnp.dot(e.astype(BF16), v1, preferred_element_type=F32)
        else:
            pv = jnp.dot(e.astype(BF16), v, preferred_element_type=F32)
            l = jnp.broadcast_to(jnp.sum(e, axis=-1, keepdims=True), pv.shape)
            acc = acc + jnp.concatenate([pv, l], axis=1)
    return m_new, acc


def _attn_prompt_kernel(lam_ref, q_ref, k_ref, v_ref, near_ref, sub_ref, *rest, t, nq, lam_init, riders):
    n_in = sum(n for _, n in riders)
    pos = 0
    for idx, (body, n) in enumerate(riders):
        body(*rest[pos:pos + n], rest[n_in + 1 + idx])
        pos += n
    _prompt_tiles(lam_ref, q_ref, k_ref, v_ref, near_ref, sub_ref, rest[n_in], t, range(nq), lam_init)


def _prompt_tiles(lam_ref, q_ref, k_ref, v_ref, near_ref, sub_ref, o_ref, t, tiles, lam_init):
    hb = t // 2
    near1 = near_ref[1]
    bias_a = _tile_bias(near_ref, t, hb)
    bias_b = _tile_bias(near_ref, hb, hb)

    def both(s, b):
        n = s.shape[0] // 2
        return jnp.concatenate([s[:n] + b, s[n:] + b], axis=0)

    def sub_fix(sr):
        return jnp.concatenate([sr[:, :t - CHUNK], sr[:, t - CHUNK:] + near1], axis=1)

    def second_half(x):
        return jnp.concatenate([x[hb:t], x[t + hb:]], axis=0)

    def put_second_half(x, xb):
        return jnp.concatenate([x[:hb], xb[:hb], x[t:t + hb], xb[hb:]], axis=0)

    for qi in tiles:
        q0 = qi * t
        r0 = (qi - tiles[0]) * t
        qq = _split_components(q_ref[r0:r0 + t, :])
        m = jnp.full((2 * t, LANES), -jnp.inf, F32)
        acc = jnp.zeros((2 * t, 2 * LANES), F32)
        for kt in range(qi):
            keys = slice(kt * t, (kt + 1) * t)
            s = _nt_dot(qq, k_ref[keys, :])
            if kt == qi - 1:
                s = jnp.concatenate([sub_fix(s[:CHUNK]), s[CHUNK:t], sub_fix(s[t:t + CHUNK]), s[t + CHUNK:]], axis=0)
            m, acc = _softmax_update(m, acc, [s], [v_ref[keys, :]], True)
        keys = slice(q0, q0 + hb)
        m, acc = _softmax_update(m, acc, [both(_nt_dot(qq, k_ref[keys, :]), bias_a)], [v_ref[keys, :]], True)
        keys = slice(q0 + hb, q0 + t)
        s = both(_nt_dot(second_half(qq), k_ref[keys, :]), bias_b)
        _, acc_b = _softmax_update(second_half(m), second_half(acc), [s], [v_ref[keys, :]], True)
        acc = put_second_half(acc, acc_b)
        o = acc[:, :DA_DV] / acc[:, DA_DV:]
        out = _diff_out(o[:t], o[t:], lam_ref[0:1, :], sub_ref[...], lam_init)
        o_ref[r0:r0 + t, :] = out.astype(o_ref.dtype)


def _xattn_rider(zs, mem_k, mem_v, N, n_tok, q_sec, seq_off, steps):
    spb = N // steps
    n_mem = mem_k.shape[1]
    width = MX_HEADS * MX_DH
    assert N == spb * steps and seq_off % spb == 0 and (MX_HEADS * n_tok) % BF16_SUBLANES == 0
    mem_spec = pl.BlockSpec((spb, n_mem, MX_HEADS, MX_DH), lambda b, h: (seq_off // spb + b * DA_HEADS + h, 0, 0, 0))
    return dict(
        body=functools.partial(_xattn_sample_kernel, spb=spb, n_tok=n_tok),
        in_specs=[pl.BlockSpec((spb * n_tok, width), lambda b, h: (b * DA_HEADS + h, q_sec)), mem_spec, mem_spec],
        operands=[zs, mem_k, mem_v],
        out_spec=pl.BlockSpec((spb * n_tok, width), lambda b, h: (b * DA_HEADS + h, 0)),
        out_shape=jax.ShapeDtypeStruct((N * n_tok, width), F32))


def _sgate_rider(z, w, b, u_sec, vn_sec, steps):
    T = z.shape[0]
    width = b.shape[1]
    rb = T // steps
    assert T == rb * steps and rb % CHUNK == 0
    return dict(
        body=functools.partial(_sgate_kernel, seg=CHUNK, n_chunk=rb // CHUNK),
        in_specs=[pl.BlockSpec((rb, width), lambda b_, h: (b_ * DA_HEADS + h, u_sec)),
                  pl.BlockSpec((rb, width), lambda b_, h: (b_ * DA_HEADS + h, vn_sec)),
                  pl.BlockSpec((SG_GROUPS, CHUNK, CHUNK), lambda b_, h: (0, 0, 0)),
                  pl.BlockSpec((CHUNK, width), lambda b_, h: (0, 0))],
        operands=[z, z, w, b],
        out_spec=pl.BlockSpec((rb, width), lambda b_, h: (b_ * DA_HEADS + h, 0)),
        out_shape=jax.ShapeDtypeStruct((T, width), BF16))


def _cast_kernel(x_ref, o_ref):
    o_ref[...] = x_ref[...].astype(o_ref.dtype)


def _cast_rider(w, steps):
    R, C = w.shape
    blocks = [n for n in range(1, steps + 1) if R % n == 0 and (R // n) % BF16_SUBLANES == 0]
    if not blocks:
        return None
    nblk = blocks[-1]

    def index(b, h):
        return (jnp.minimum(b * DA_HEADS + h, nblk - 1), 0)

    return dict(body=_cast_kernel, in_specs=[pl.BlockSpec((R // nblk, C), index)], operands=[w],
                out_spec=pl.BlockSpec((R // nblk, C), index), out_shape=jax.ShapeDtypeStruct((R, C), BF16))


def _attn_prompt(z, lam, near, subln, B, S, t, lam_init, riders=()):
    assert S % t == 0 and t % (2 * CHUNK) == 0 and MAX_DISTANCE <= CHUNK + 1
    H = DA_HEADS
    in_specs = [
        pl.BlockSpec((8, LANES), lambda b, h: (0, 0)),
        pl.BlockSpec((S, DA_DV), lambda b, h: (b, h)),
        pl.BlockSpec((S, DA_DV), lambda b, h: (b, H + h)),
        pl.BlockSpec((S, DA_DV), lambda b, h: (b, 2 * H + h)),
        pl.BlockSpec((None, 2, CHUNK, CHUNK), lambda b, h: (h, 0, 0, 0)),
        pl.BlockSpec((1, DA_DV), lambda b, h: (0, 0)),
    ]
    operands = [lam, z, z, z, near, subln]
    for r in riders:
        in_specs += r["in_specs"]
        operands += r["operands"]
    return pl.pallas_call(
        functools.partial(_attn_prompt_kernel, t=t, nq=S // t, lam_init=lam_init,
                          riders=tuple((r["body"], len(r["in_specs"])) for r in riders)),
        grid=(B, H),
        in_specs=in_specs,
        out_specs=[pl.BlockSpec((S, DA_DV), lambda b, h: (b, h))] + [r["out_spec"] for r in riders],
        out_shape=[jax.ShapeDtypeStruct((B * S, H * DA_DV), BF16)] + [r["out_shape"] for r in riders],
        compiler_params=_cparams("parallel", "arbitrary"),
        name="attn_prompt",
    )(*operands)


def _attn_sample_kernel(pt_ref, lam_ref, q_ref, kn_ref, vn_ref, far_ref, last_ref, new_ref, sub_ref, *rest,
                        spb, pps, n_tok, page, lam_init):
    n_in = spb * pps
    kp_refs, vp_refs = rest[:n_in], rest[n_in:2 * n_in]
    o_ref, qq_ref, m_ref, acc_ref = rest[2 * n_in:]
    p = pl.program_id(1)
    last = pl.num_programs(1) - 1
    hr = 2 * n_tok
    flat = page * DA_HEADS

    @pl.when(p == 0)
    def _():
        for j in range(spb):
            q = q_ref[j * n_tok:(j + 1) * n_tok, :]
            for h in range(DA_HEADS):
                qq_ref[j, h * hr:(h + 1) * hr, :] = _split_components(q[:, h * DA_DV:(h + 1) * DA_DV]).astype(BF16)
        m_ref[...] = jnp.full(m_ref.shape, -jnp.inf, F32)
        acc_ref[...] = jnp.zeros(acc_ref.shape, F32)

    far = far_ref[...]
    tail = jnp.where(p == last, last_ref[...], far)
    for j in range(spb):
        qq = qq_ref[j]
        s_list, v_list = [], []
        for r in range(pps):
            s = _nt_dot(qq, kp_refs[j * pps + r][...].reshape(flat, DA_DV).astype(BF16))
            s_list.append(s + (tail if r == pps - 1 else far))
            v_list.append(vp_refs[j * pps + r][...].reshape(flat, DA_DV).astype(BF16))
        m, acc = _softmax_update(m_ref[j], acc_ref[j], s_list, v_list, False)
        m_ref[j] = m
        acc_ref[j] = acc

    @pl.when(p == last)
    def _():
        pad = jnp.zeros((page - n_tok * DA_HEADS, DA_DV), F32)
        for j in range(spb):
            toks = slice(j * n_tok, (j + 1) * n_tok)

            def new_rows(ref):
                return jnp.concatenate([ref[toks].reshape(n_tok * DA_HEADS, DA_DV), pad], axis=0).astype(BF16)

            s_new = _nt_dot(qq_ref[j], new_rows(kn_ref)) + new_ref[...]
            _, acc2 = _softmax_update(m_ref[j], acc_ref[j], [s_new], [new_rows(vn_ref)], False)
            o = acc2[:, :DA_DV] / acc2[:, DA_DV:]
            for h in range(DA_HEADS):
                oh = o[h * hr:(h + 1) * hr]
                out = _diff_out(oh[:n_tok], oh[n_tok:], lam_ref[0:1, :], sub_ref[...], lam_init)
                o_ref[toks, h * DA_DV:(h + 1) * DA_DV] = out.astype(o_ref.dtype)


def _attn_sample(z, k_new, v_new, page_idx, cache_k, cache_v, lam, far, last, new, subln, N, n_tok, lam_init, spb, pps):
    n_pages = page_idx.shape[1]
    page = cache_k.shape[1]
    width = DA_HEADS * DA_DV
    rows = DA_HEADS * 2 * n_tok
    assert (2 * n_tok) % BF16_SUBLANES == 0 and n_pages % pps == 0 and N % spb == 0

    def const(shape):
        return pl.BlockSpec(shape, lambda n, p, pt: (0,) * len(shape))

    def page_spec(j, r):
        return pl.BlockSpec((None, page, DA_HEADS, DA_DV), lambda n, p, pt: (pt[n * spb + j, p * pps + r], 0, 0, 0))

    page_specs = [page_spec(j, r) for j in range(spb) for r in range(pps)]
    new_spec = pl.BlockSpec((spb * n_tok, DA_HEADS, DA_DV), lambda n, p, pt: (n, 0, 0))
    grid_spec = pltpu.PrefetchScalarGridSpec(
        num_scalar_prefetch=1,
        grid=(N // spb, n_pages // pps),
        in_specs=[
            const((8, LANES)),
            pl.BlockSpec((spb * n_tok, width), lambda n, p, pt: (n, 0)),
            new_spec, new_spec,
            const((rows, page * DA_HEADS)), const((rows, page * DA_HEADS)), const((rows, page)),
            const((1, DA_DV)),
        ] + page_specs + page_specs,
        out_specs=pl.BlockSpec((spb * n_tok, width), lambda n, p, pt: (n, 0)),
        scratch_shapes=[pltpu.VMEM((spb, rows, DA_DV), BF16), pltpu.VMEM((spb, rows, LANES), F32),
                        pltpu.VMEM((spb, rows, 2 * LANES), F32)],
    )
    n_in = spb * pps
    return pl.pallas_call(
        functools.partial(_attn_sample_kernel, spb=spb, pps=pps, n_tok=n_tok, page=page, lam_init=lam_init),
        grid_spec=grid_spec,
        out_shape=jax.ShapeDtypeStruct((N * n_tok, width), F32),
        compiler_params=_cparams("parallel", "arbitrary"),
        name="attn_sample",
    )(page_idx, lam, z, k_new, v_new, far, last, new, subln, *([cache_k] * n_in), *([cache_v] * n_in))


def _sgate_kernel(u_ref, vn_ref, w_ref, b_ref, o_ref, *, seg, n_chunk):
    i = lax.broadcasted_iota(jnp.int32, (CHUNK, CHUNK), 0)
    j = lax.broadcasted_iota(jnp.int32, (CHUNK, CHUNK), 1)
    keep = (_div_pow2(i, seg) == _div_pow2(j, seg)) & (j <= i)
    gd = u_ref.shape[1] // SG_GROUPS
    for g in range(SG_GROUPS):
        w = jnp.where(keep, w_ref[g], 0.0).astype(BF16)
        cols = slice(g * gd, (g + 1) * gd)
        for c in range(n_chunk):
            rws = slice(c * CHUNK, (c + 1) * CHUNK)
            mix = jnp.dot(w, vn_ref[rws, cols].astype(BF16), preferred_element_type=F32) + b_ref[:, cols]
            o_ref[rws, cols] = (u_ref[rws, cols].astype(F32) * mix).astype(o_ref.dtype)


def _sgate(z, w, b, u_sec, vn_sec, n_chunk):
    T = z.shape[0]
    width = b.shape[1]
    rb = n_chunk * CHUNK
    return pl.pallas_call(
        functools.partial(_sgate_kernel, seg=CHUNK, n_chunk=n_chunk),
        grid=(T // rb,),
        in_specs=[
            pl.BlockSpec((rb, width), lambda i: (i, u_sec)),
            pl.BlockSpec((rb, width), lambda i: (i, vn_sec)),
            pl.BlockSpec((SG_GROUPS, CHUNK, CHUNK), lambda i: (0, 0, 0)),
            pl.BlockSpec((CHUNK, width), lambda i: (0, 0)),
        ],
        out_specs=pl.BlockSpec((rb, width), lambda i: (i, 0)),
        out_shape=jax.ShapeDtypeStruct((T, width), BF16),
        compiler_params=_cparams("parallel"),
        name="sgate",
    )(z, z, w, b)


def _sgate_seq_kernel(w_ref, b_ref, u_ref, vn_ref, o_ref, wbd_ref, bias_ref, *, seg):
    gd = u_ref.shape[1] // SG_GROUPS

    @pl.when(pl.program_id(0) == 0)
    def _():
        i = lax.broadcasted_iota(jnp.int32, (CHUNK, CHUNK), 0)
        j = lax.broadcasted_iota(jnp.int32, (CHUNK, CHUNK), 1)
        same = _div_pow2(i, seg) == _div_pow2(j, seg)
        ti, tj = i & (seg - 1), j & (seg - 1)
        for g in range(SG_GROUPS):
            w = jnp.zeros((CHUNK, CHUNK), F32)
            bias = jnp.zeros((CHUNK, gd), F32)
            for a in range(seg):
                bias = jnp.where(ti == a, b_ref[g, a], bias)
                for b in range(a + 1):
                    w = jnp.where((ti == a) & (tj == b), w_ref[g * seg + a, b], w)
            wbd_ref[g] = jnp.where(same, w, 0.0).astype(BF16)
            bias_ref[:, g * gd:(g + 1) * gd] = bias

    for g in range(SG_GROUPS):
        cols = slice(g * gd, (g + 1) * gd)
        mix = jnp.dot(wbd_ref[g], vn_ref[:, cols].astype(BF16), preferred_element_type=F32) + bias_ref[:, cols]
        o_ref[:, cols] = (u_ref[:, cols].astype(F32) * mix).astype(o_ref.dtype)


def _sgate_seq(z, w, b, seg, u_sec, vn_sec):
    T = z.shape[0]
    width = SG_GROUPS * CHUNK
    assert CHUNK % seg == 0 and seg & (seg - 1) == 0 and T % CHUNK == 0
    return pl.pallas_call(
        functools.partial(_sgate_seq_kernel, seg=seg),
        grid=(T // CHUNK,),
        in_specs=[
            pl.BlockSpec(memory_space=pltpu.SMEM), pl.BlockSpec(memory_space=pltpu.SMEM),
            pl.BlockSpec((CHUNK, width), lambda i: (i, u_sec)),
            pl.BlockSpec((CHUNK, width), lambda i: (i, vn_sec)),
        ],
        out_specs=pl.BlockSpec((CHUNK, width), lambda i: (i, 0)),
        out_shape=jax.ShapeDtypeStruct((T, width), BF16),
        scratch_shapes=[pltpu.VMEM((SG_GROUPS, CHUNK, CHUNK), BF16), pltpu.VMEM((CHUNK, width), F32)],
        compiler_params=_cparams("arbitrary"),
        name="sgate_seq",
    )(w[:, :seg, :seg].reshape(SG_GROUPS * seg, seg), b[:, :seg], z, z)


def _xattn_prompt_kernel(q_ref, k_ref, v_ref, o_ref):
    heads = [slice(h * MX_DH, (h + 1) * MX_DH) for h in range(MX_HEADS)]
    scores = [_nt_dot(q_ref[:, cols], k_ref[:, cols]) for cols in heads]
    exps = [jnp.exp(s - jnp.max(s, axis=-1, keepdims=True)) for s in scores]
    for cols, e in zip(heads, exps):
        o = jnp.dot(e.astype(BF16), v_ref[:, cols], preferred_element_type=F32)
        o_ref[:, cols] = (o / jnp.sum(e, axis=-1, keepdims=True)).astype(o_ref.dtype)


def _xattn_prompt(z, mem_kv, B, S, tq, q_sec):
    nq = S // tq
    n_mem = mem_kv.shape[0] // B
    width = MX_HEADS * MX_DH
    return pl.pallas_call(
        _xattn_prompt_kernel,
        grid=(B, nq),
        in_specs=[
            pl.BlockSpec((tq, width), lambda b, i: (b * nq + i, q_sec)),
            pl.BlockSpec((n_mem, width), lambda b, i: (b, 0)),
            pl.BlockSpec((n_mem, width), lambda b, i: (b, 1)),
        ],
        out_specs=pl.BlockSpec((tq, width), lambda b, i: (b * nq + i, 0)),
        out_shape=jax.ShapeDtypeStruct((B * S, width), BF16),
        compiler_params=_cparams("parallel", "arbitrary"),
        name="xattn_prompt",
    )(z, mem_kv, mem_kv)


def _xattn_sample_kernel(q_ref, k_ref, v_ref, o_ref, *, spb, n_tok):
    n_mem = k_ref.shape[1]
    rows, flat = MX_HEADS * n_tok, n_mem * MX_HEADS
    r = lax.broadcasted_iota(jnp.int32, (rows, flat), 0)
    c = lax.broadcasted_iota(jnp.int32, (rows, flat), 1)
    own = (c & (MX_HEADS - 1)) == _div_pow2(r, n_tok)
    for j in range(spb):
        toks = slice(j * n_tok, (j + 1) * n_tok)
        q = q_ref[toks, :]
        qq = jnp.concatenate([q[:, h * MX_DH:(h + 1) * MX_DH] for h in range(MX_HEADS)], axis=0).astype(BF16)
        s = jnp.where(own, _nt_dot(qq, k_ref[j].reshape(flat, MX_DH).astype(BF16)), NEG_INF)
        e = jnp.exp(s - jnp.max(s, axis=-1, keepdims=True))
        o = jnp.dot(e.astype(BF16), v_ref[j].reshape(flat, MX_DH).astype(BF16), preferred_element_type=F32)
        o = o / jnp.sum(e, axis=-1, keepdims=True)
        for h in range(MX_HEADS):
            o_ref[toks, h * MX_DH:(h + 1) * MX_DH] = o[h * n_tok:(h + 1) * n_tok].astype(o_ref.dtype)


def _xattn_sample(z, mem_k, mem_v, N, n_tok, q_sec, seq_off, spb):
    n_mem = mem_k.shape[1]
    width = MX_HEADS * MX_DH
    assert (MX_HEADS * n_tok) % BF16_SUBLANES == 0 and N % spb == 0 and seq_off % spb == 0
    mem_spec = pl.BlockSpec((spb, n_mem, MX_HEADS, MX_DH), lambda n: (seq_off // spb + n, 0, 0, 0))
    return pl.pallas_call(
        functools.partial(_xattn_sample_kernel, spb=spb, n_tok=n_tok),
        grid=(N // spb,),
        in_specs=[pl.BlockSpec((spb * n_tok, width), lambda n: (n, q_sec)), mem_spec, mem_spec],
        out_specs=pl.BlockSpec((spb * n_tok, width), lambda n: (n, 0)),
        out_shape=jax.ShapeDtypeStruct((N * n_tok, width), F32),
        compiler_params=_cparams("parallel"),
        name="xattn_sample",
    )(z, mem_k, mem_v)


def _post_kernel(x_ref, da_ref, sg_ref, mx_ref, gate_ref, wb_ref, wo_ref, gf_ref, wu_ref, wd_ref, gfin_ref,
                 o_ref, y_ref, *, d, d_ff, fc, n_sub, final_norm):
    branches = (da_ref, sg_ref, mx_ref)
    rs = x_ref.shape[0] // n_sub
    subs = [slice(sb * rs, (sb + 1) * rs) for sb in range(n_sub)]
    merged = [None] * n_sub
    for k in range(N_BRANCH):
        for i, rws in enumerate(subs):
            br = jnp.dot(branches[k][rws, :].astype(BF16), wb_ref[k * d:(k + 1) * d, :], preferred_element_type=F32)
            term = gate_ref[rws, k * d:(k + 1) * d].astype(F32) * br
            merged[i] = term if merged[i] is None else merged[i] + term
    h2 = []
    for i, rws in enumerate(subs):
        x1 = x_ref[rws, :] + jnp.dot(merged[i].astype(BF16), wo_ref[...], preferred_element_type=F32)
        y_ref[rws, :] = x1
        h2.append(_rms(x1, gf_ref[...]).astype(BF16))
    for c in range(d_ff // fc):
        up = [(jnp.dot(h, wu_ref[:, c * fc:(c + 1) * fc], preferred_element_type=F32),
               jnp.dot(h, wu_ref[:, d_ff + c * fc:d_ff + (c + 1) * fc], preferred_element_type=F32)) for h in h2]
        for (a, b), rws in zip(up, subs):
            act = (jax.nn.silu(a) * b).astype(BF16)
            y_ref[rws, :] += jnp.dot(act, wd_ref[c * fc:(c + 1) * fc, :], preferred_element_type=F32)
    for rws in subs:
        y = y_ref[rws, :]
        o_ref[rws, :] = _rms(y, gfin_ref[...]) if final_norm else y


def _post(x, o_da, o_sg, o_mx, z, gate_sec, wb, wo, g_ffn, wu, wd, g_final, tm, final_norm):
    T, d = x.shape
    d_ff = wd.shape[0]
    fc = 256
    n_sub = 2 if tm >= 512 else 1
    row = lambda i: (i, 0)
    whole = lambda i: (0, 0)

    def resident(shape):
        return pl.BlockSpec(shape, whole, pipeline_mode=pl.Buffered(1))

    return pl.pallas_call(
        functools.partial(_post_kernel, d=d, d_ff=d_ff, fc=fc, n_sub=n_sub, final_norm=final_norm),
        grid=(T // tm,),
        in_specs=[
            pl.BlockSpec((tm, d), row), pl.BlockSpec((tm, d), row), pl.BlockSpec((tm, d), row), pl.BlockSpec((tm, d), row),
            pl.BlockSpec((tm, N_BRANCH * d), lambda i: (i, gate_sec)),
            resident(wb.shape), resident(wo.shape), resident((1, d)), resident(wu.shape), resident(wd.shape),
            resident((1, d)),
        ],
        out_specs=pl.BlockSpec((tm, d), row),
        out_shape=jax.ShapeDtypeStruct((T, d), F32),
        scratch_shapes=[pltpu.VMEM((tm, d), F32)],
        compiler_params=_cparams("parallel"),
        name="post",
    )(x, o_da, o_sg, o_mx, z, wb, wo, g_ffn, wu, wd, g_final)


_IN_SECTIONS = ("q", "k", "v", "u", "s", "m", "g0", "g1", "g2")
_IN_ACTS = {
    "q": ("scale", DA_DK ** -0.5 * LOG2E), "k": ("none",), "v": ("none",), "u": ("gelu",), "s": ("gelu_rms",),
    "m": ("scale", MX_DH ** -0.5), "g0": ("sigmoid",), "g1": ("sigmoid",), "g2": ("sigmoid",),
}
_IN_COPIES = {"k": 0, "v": 1}


def kernel(x_prompt, x_sample, mem_prompt, cache_da_k, cache_da_v, cache_mem_k, cache_mem_v, page_table, g_attn, w_in,
           da_lam, da_subln, rel_bias, sg_norm, sg_w, sg_b, g_mem, w_mem_kv, w_branch, w_out, g_ffn, w_up, w_down,
           g_final):
    B, S, D = x_prompt.shape
    N, n_tok, _ = x_sample.shape
    depth, n_pool, page = cache_da_k.shape[:3]
    n_mem = mem_prompt.shape[1]
    n_pages = page_table.shape[1]
    width = DA_HEADS * DA_DV
    assert D == width == SG_GROUPS * CHUNK == MX_HEADS * MX_DH and page == CHUNK and n_tok <= CHUNK

    t_attn = min(512, S)
    pps = max(d for d in range(1, MAX_PAGES_PER_STEP + 1) if n_pages % d == 0)
    spb_da = 1
    spb_mx = 4 if N % 4 == 0 else 1
    ride_mx = N % (B * DA_HEADS) == 0 and N // (B * DA_HEADS) <= spb_mx
    ride_sg = S % (DA_HEADS * CHUNK) == 0
    tm_p = 256
    tm_s = min(256, N * n_tok)
    tm_post = 512
    sec = {n: i for i, n in enumerate(_IN_SECTIONS)}
    in_sections = [(_IN_ACTS[n], _IN_COPIES.get(n)) for n in _IN_SECTIONS]

    xp = x_prompt.reshape(B * S, D)
    xs = x_sample.reshape(N * n_tok, D)
    mem = mem_prompt.reshape(B * n_mem, D)
    ck = cache_da_k.reshape(depth * n_pool, page, DA_HEADS, DA_DV)
    cv = cache_da_v.reshape(depth * n_pool, page, DA_HEADS, DA_DV)
    cmk = cache_mem_k.reshape(depth * N, n_mem, MX_HEADS, MX_DH)
    cmv = cache_mem_v.reshape(depth * N, n_mem, MX_HEADS, MX_DH)
    row = lambda a: a.reshape(1, -1)

    near_bias = _prompt_bias(rel_bias)
    seg_s = n_tok
    outs = {k: [] for k in ("dkp", "dvp", "dks", "dvs", "mkp", "mvp", "sgs")}
    for l in range(depth):
        lam_init = 0.8 - 0.6 * math.exp(-0.3 * l)
        w_in_l = w_in[l].astype(BF16)
        far, last, new, lam = _sample_bias(rel_bias, da_lam[l], n_tok, page, lam_init)
        subln = row(da_subln[l])
        sgb_p = jnp.repeat(sg_b[l].T, CHUNK, axis=1)

        zs, ks32, vs32 = _norm_matmul(xs, row(g_attn[l]), w_in_l, row(sg_norm[l]), in_sections, F32, tm_s,
                                      "inproj_sample", copy_heads=DA_HEADS)
        zs, xp_l = lax.optimization_barrier((zs, xp))
        zp, kp32, vp32 = _norm_matmul(xp_l, row(g_attn[l]), w_in_l, row(sg_norm[l]), in_sections, BF16, tm_p,
                                      "inproj_prompt")
        steps = B * DA_HEADS
        riders = []
        if ride_mx:
            riders.append(_xattn_rider(zs, cmk, cmv, N, n_tok, sec["m"], l * N, steps))
        if ride_sg:
            riders.append(_sgate_rider(zp, sg_w[l], sgb_p, sec["u"], sec["s"], steps))
        later_w = [w_branch[l], w_out[l], w_up[l], w_down[l], w_mem_kv[l]]
        casts = [_cast_rider(w, steps) for w in later_w]
        oda_p, *rode = _attn_prompt(zp, lam, near_bias, subln, B, S, t_attn, lam_init,
                                    riders + [c for c in casts if c is not None])
        omx_s = rode.pop(0) if ride_mx else _xattn_sample(zs, cmk, cmv, N, n_tok, sec["m"], l * N, spb_mx)
        osg_p = rode.pop(0) if ride_sg else _sgate(zp, sg_w[l], sgb_p, sec["u"], sec["s"], 8)
        wb_l, wo_l, wu_l, wd_l, wm_l = [rode.pop(0) if c is not None else w.astype(BF16)
                                        for c, w in zip(casts, later_w)]
        oda_p, omx_s, osg_p, zp, zs, mem_l = lax.optimization_barrier((oda_p, omx_s, osg_p, zp, zs, mem))
        oda_s = _attn_sample(zs, ks32, vs32, page_table + l * n_pool, ck, cv, lam, far, last, new, subln, N, n_tok,
                             lam_init, spb_da, pps)

        mkv16, mk32, mv32 = _norm_matmul(mem_l, row(g_mem[l]), wm_l, row(sg_norm[l]),
                                         [(("none",), 0), (("none",), 1)], BF16, min(512, B * n_mem), "mem_kv",
                                         copy_heads=MX_HEADS)
        omx_p = _xattn_prompt(zp, mkv16, B, S, 1024 if S % 1024 == 0 else 512, sec["m"])
        osg_s = _sgate_seq(zs, sg_w[l], sg_b[l], seg_s, sec["u"], sec["s"])

        xs = _post(xs, oda_s, osg_s, omx_s, zs, sec["g0"] // N_BRANCH, wb_l, wo_l, row(g_ffn[l]), wu_l, wd_l,
                   row(g_final), tm_s, l == depth - 1)
        xp = _post(xp, oda_p, osg_p, omx_p, zp, sec["g0"] // N_BRANCH, wb_l, wo_l, row(g_ffn[l]), wu_l, wd_l,
                   row(g_final), tm_post, l == depth - 1)

        outs["dkp"].append(kp32.reshape(B, S, DA_HEADS, 2 * DA_DK))
        outs["dvp"].append(vp32.reshape(B, S, DA_HEADS, DA_DV))
        outs["dks"].append(ks32.reshape(N, n_tok, DA_HEADS, 2 * DA_DK))
        outs["dvs"].append(vs32.reshape(N, n_tok, DA_HEADS, DA_DV))
        outs["mkp"].append(mk32.reshape(B, n_mem, MX_HEADS, MX_DH))
        outs["mvp"].append(mv32.reshape(B, n_mem, MX_HEADS, MX_DH))
        outs["sgs"].append(zs[:, sec["s"] * D:(sec["s"] + 1) * D].reshape(N, n_tok, D))

    return (xp.reshape(B, S, D), xs.reshape(N, n_tok, D), jnp.stack(outs["dkp"]), jnp.stack(outs["dvp"]),
            jnp.stack(outs["dks"]), jnp.stack(outs["dvs"]), jnp.stack(outs["mkp"]), jnp.stack(outs["mvp"]),
            jnp.stack(outs["sgs"]))
```

```python
import functools
import math

import jax
import jax.numpy as jnp
from jax import lax
from jax.experimental import pallas as pl
from jax.experimental.pallas import tpu as pltpu

F32 = jnp.float32
BF16 = jnp.bfloat16

DA_HEADS = 8
DA_DK = 64
DA_DV = 2 * DA_DK
SG_GROUPS = 8
CHUNK = 128
MX_HEADS = 4
MX_DH = 256
N_BUCKETS = 32
MAX_DISTANCE = 128
N_BRANCH = 3
RMS_EPS = 1e-6
NEG_INF = -1e30
LOG2E = math.log2(math.e)

LANES = 128
BF16_SUBLANES = 16
VMEM_LIMIT_BYTES = 56 * 1024 * 1024
MAX_PAGES_PER_STEP = 16


def _cparams(*sem):
    return pltpu.CompilerParams(dimension_semantics=sem, vmem_limit_bytes=VMEM_LIMIT_BYTES)


def _rms(x, g):
    ms = jnp.mean(x * x, axis=-1, keepdims=True)
    return x * lax.rsqrt(ms + RMS_EPS) * g


def _div_pow2(x, n):
    assert n > 0 and n & (n - 1) == 0, n
    return x >> (n.bit_length() - 1)


def _nt_dot(a, b):
    return lax.dot_general(a, b, (((1,), (1,)), ((), ())), preferred_element_type=F32)


def _lane_tile(x, n):
    return x if n == 1 else jnp.concatenate([x] * n, axis=1)


def _norm_matmul_kernel(x_ref, g_ref, w_ref, sgn_ref, z_ref, *copy_refs, sections, tn):
    hn = _rms(x_ref[...], g_ref[...]).astype(BF16)
    for sec, (act, copy_idx) in enumerate(sections):
        cols = slice(sec * tn, (sec + 1) * tn)
        acc = jnp.dot(hn, w_ref[:, cols], preferred_element_type=F32)
        if act[0] == "scale":
            val = acc * act[1]
        elif act[0] == "gelu":
            val = jax.nn.gelu(acc)
        elif act[0] == "gelu_rms":
            val = _rms(jax.nn.gelu(acc), sgn_ref[...])
        elif act[0] == "sigmoid":
            val = jax.nn.sigmoid(acc)
        else:
            val = acc
        z_ref[:, cols] = val.astype(z_ref.dtype)
        if copy_idx is not None:
            cref = copy_refs[copy_idx]
            if len(cref.shape) == 2:
                cref[...] = val
            else:
                hd = cref.shape[2]
                for hh in range(cref.shape[1]):
                    cref[:, hh, :] = val[:, hh * hd:(hh + 1) * hd]


def _norm_matmul(x, g, w, sgn, sections, z_dtype, tm, name, copy_heads=1):
    T, D = x.shape
    n_sec = len(sections)
    tn = w.shape[1] // n_sec
    n_copy = sum(1 for _, c in sections if c is not None)
    out_shape = [jax.ShapeDtypeStruct((T, n_sec * tn), z_dtype)]
    out_specs = [pl.BlockSpec((tm, n_sec * tn), lambda i: (i, 0))]
    for _ in range(n_copy):
        if copy_heads == 1:
            out_shape.append(jax.ShapeDtypeStruct((T, tn), F32))
            out_specs.append(pl.BlockSpec((tm, tn), lambda i: (i, 0)))
        else:
            out_shape.append(jax.ShapeDtypeStruct((T, copy_heads, tn // copy_heads), F32))
            out_specs.append(pl.BlockSpec((tm, copy_heads, tn // copy_heads), lambda i: (i, 0, 0)))

    def resident(shape):
        return pl.BlockSpec(shape, lambda i: (0, 0), pipeline_mode=pl.Buffered(1))

    return pl.pallas_call(
        functools.partial(_norm_matmul_kernel, sections=tuple(sections), tn=tn),
        grid=(T // tm,),
        in_specs=[pl.BlockSpec((tm, D), lambda i: (i, 0)), resident((1, D)), resident(w.shape), resident((1, tn))],
        out_specs=out_specs,
        out_shape=out_shape,
        compiler_params=_cparams("parallel"),
        name=name,
    )(x, g, w, sgn)


def _t5_bias(dist, rb_ref, h):
    n = jnp.maximum(dist, 0)
    max_exact = N_BUCKETS // 2
    nf = jnp.maximum(n, 1).astype(F32)
    rel = jnp.log(nf / max_exact) / math.log(MAX_DISTANCE / max_exact) * (N_BUCKETS - max_exact)
    out = jnp.zeros(dist.shape, F32)
    for b in range(max_exact):
        out = jnp.where(n == b, rb_ref[b, h], out)
    out = jnp.where(n >= max_exact, rb_ref[max_exact, h], out)
    for b in range(max_exact + 1, N_BUCKETS):
        out = jnp.where(rel >= b - max_exact, rb_ref[b, h], out)
    return jnp.where(dist >= 0, out * LOG2E, NEG_INF)


def _far_bias(rb_ref, h):
    return rb_ref[N_BUCKETS - 1, h] * LOG2E


def _prompt_bias_kernel(rb_ref, near_ref):
    d = lax.broadcasted_iota(jnp.int32, (CHUNK, CHUNK), 0) - lax.broadcasted_iota(jnp.int32, (CHUNK, CHUNK), 1)
    for h in range(near_ref.shape[0]):
        far = _far_bias(rb_ref, h)
        near_ref[h, 0] = _t5_bias(d, rb_ref, h) - far
        near_ref[h, 1] = _t5_bias(d + CHUNK, rb_ref, h) - far


def _prompt_bias(rel_bias):
    assert MAX_DISTANCE <= CHUNK + 1
    H = rel_bias.shape[1]
    return pl.pallas_call(
        _prompt_bias_kernel,
        in_specs=[pl.BlockSpec(memory_space=pltpu.SMEM)],
        out_shape=jax.ShapeDtypeStruct((H, 2, CHUNK, CHUNK), F32),
        name="prompt_bias",
    )(rel_bias)


def _tile_bias(near_ref, rows, cols):
    fill = {-1: jnp.full((CHUNK, CHUNK), NEG_INF, F32), 2: jnp.zeros((CHUNK, CHUNK), F32)}

    def block(bd):
        return near_ref[bd] if 0 <= bd < 2 else fill[max(min(bd, 2), -1)]

    return jnp.concatenate([jnp.concatenate([block(bi - bj) for bj in range(cols // CHUNK)], axis=1)
                            for bi in range(rows // CHUNK)], axis=0)


def _sample_bias_kernel(rb_ref, lam_p_ref, far_ref, last_ref, new_ref, lam_ref, *, n_tok, page, lam_init):
    hr = 2 * n_tok

    def grid(cols):
        tok = lax.broadcasted_iota(jnp.int32, (hr, cols), 0) & (n_tok - 1)
        col = lax.broadcasted_iota(jnp.int32, (hr, cols), 1)
        return tok, col, _div_pow2(col, DA_HEADS), col & (DA_HEADS - 1)

    tok, _, key, kh = grid(page * DA_HEADS)
    tok_n, col_n, key_n, kh_n = grid(page)
    for h in range(DA_HEADS):
        sl = slice(h * hr, (h + 1) * hr)
        far_ref[sl, :] = jnp.where(kh == h, _far_bias(rb_ref, h), NEG_INF)
        last_ref[sl, :] = jnp.where(kh == h, _t5_bias(page + tok - key, rb_ref, h), NEG_INF)
        new_ref[sl, :] = jnp.where((kh_n == h) & (col_n < n_tok * DA_HEADS), _t5_bias(tok_n - key_n, rb_ref, h), NEG_INF)
    lp = lam_p_ref[...]
    s1 = jnp.sum(lp[0:1, :] * lp[1:2, :], axis=-1, keepdims=True)
    s2 = jnp.sum(lp[2:3, :] * lp[3:4, :], axis=-1, keepdims=True)
    lam = jnp.exp(s1) - jnp.exp(s2) + lam_init
    lam_ref[...] = jnp.broadcast_to(lam, lam_ref.shape)


def _sample_bias(rel_bias, da_lam_l, n_tok, page, lam_init):
    assert n_tok & (n_tok - 1) == 0 and n_tok * DA_HEADS <= page
    rows = DA_HEADS * 2 * n_tok
    wide = jax.ShapeDtypeStruct((rows, page * DA_HEADS), F32)
    return pl.pallas_call(
        functools.partial(_sample_bias_kernel, n_tok=n_tok, page=page, lam_init=lam_init),
        in_specs=[pl.BlockSpec(memory_space=pltpu.SMEM), pl.BlockSpec(memory_space=pltpu.VMEM)],
        out_shape=[wide, wide, jax.ShapeDtypeStruct((rows, page), F32), jax.ShapeDtypeStruct((8, LANES), F32)],
        name="sample_bias",
    )(rel_bias, da_lam_l)


def _split_components(q):
    lane = lax.broadcasted_iota(jnp.int32, q.shape, 1)
    zero = jnp.zeros_like(q)
    return jnp.concatenate([jnp.where(lane < DA_DK, q, zero), jnp.where(lane >= DA_DK, q, zero)], axis=0)


def _diff_out(o1, o2, lam_row, subln, lam_init):
    o = o1 - lam_row * o2
    return _rms(o, subln) * (1.0 - lam_init)


def _softmax_update(m, acc, s_list, v_list, mxu_sums):
    m_new = m
    for s in s_list:
        m_new = jnp.maximum(m_new, jnp.max(s, axis=-1, keepdims=True))
    acc = acc * _lane_tile(jnp.exp2(m - m_new), 2)
    for s, v in zip(s_list, v_list):
        e = jnp.exp2(s - _lane_tile(m_new, s.shape[1] // LANES))
        if mxu_sums:
            v1 = jnp.concatenate([v, jnp.ones(v.shape, BF16)], axis=1)
            acc = acc + jnp.dot(e.astype(BF16), v1, preferred_element_type=F32)
        else:
            pv = jnp.dot(e.astype(BF16), v, preferred_element_type=F32)
            l = jnp.broadcast_to(jnp.sum(e, axis=-1, keepdims=True), pv.shape)
            acc = acc + jnp.concatenate([pv, l], axis=1)
    return m_new, acc


def _attn_prompt_kernel(lam_ref, q_ref, k_ref, v_ref, near_ref, sub_ref, *rest, t, nq, lam_init, riders):
    n_in = sum(n for _, n in riders)
    pos = 0
    for idx, (body, n) in enumerate(riders):
        body(*rest[pos:pos + n], rest[n_in + 1 + idx])
        pos += n
    _prompt_tiles(lam_ref, q_ref, k_ref, v_ref, near_ref, sub_ref, rest[n_in], t, range(nq), lam_init)


def _prompt_tiles(lam_ref, q_ref, k_ref, v_ref, near_ref, sub_ref, o_ref, t, tiles, lam_init):
    hb = t // 2
    near1 = near_ref[1]
    bias_a = _tile_bias(near_ref, t, hb)
    bias_b = _tile_bias(near_ref, hb, hb)

    def both(s, b):
        n = s.shape[0] // 2
        return jnp.concatenate([s[:n] + b, s[n:] + b], axis=0)

    def sub_fix(sr):
        return jnp.concatenate([sr[:, :t - CHUNK], sr[:, t - CHUNK:] + near1], axis=1)

    def second_half(x):
        return jnp.concatenate([x[hb:t], x[t + hb:]], axis=0)

    def put_second_half(x, xb):
        return jnp.concatenate([x[:hb], xb[:hb], x[t:t + hb], xb[hb:]], axis=0)

    for qi in tiles:
        q0 = qi * t
        r0 = (qi - tiles[0]) * t
        qq = _split_components(q_ref[r0:r0 + t, :])
        m = jnp.full((2 * t, LANES), -jnp.inf, F32)
        acc = jnp.zeros((2 * t, 2 * LANES), F32)
        for kt in range(qi):
            keys = slice(kt * t, (kt + 1) * t)
            s = _nt_dot(qq, k_ref[keys, :])
            if kt == qi - 1:
                s = jnp.concatenate([sub_fix(s[:CHUNK]), s[CHUNK:t], sub_fix(s[t:t + CHUNK]), s[t + CHUNK:]], axis=0)
            m, acc = _softmax_update(m, acc, [s], [v_ref[keys, :]], True)
        keys = slice(q0, q0 + hb)
        m, acc = _softmax_update(m, acc, [both(_nt_dot(qq, k_ref[keys, :]), bias_a)], [v_ref[keys, :]], True)
        keys = slice(q0 + hb, q0 + t)
        s = both(_nt_dot(second_half(qq), k_ref[keys, :]), bias_b)
        _, acc_b = _softmax_update(second_half(m), second_half(acc), [s], [v_ref[keys, :]], True)
        acc = put_second_half(acc, acc_b)
        o = acc[:, :DA_DV] / acc[:, DA_DV:]
        out = _diff_out(o[:t], o[t:], lam_ref[0:1, :], sub_ref[...], lam_init)
        o_ref[r0:r0 + t, :] = out.astype(o_ref.dtype)


def _xattn_rider(zs, mem_k, mem_v, N, n_tok, q_sec, seq_off, steps):
    spb = N // steps
    n_mem = mem_k.shape[1]
    width = MX_HEADS * MX_DH
    assert N == spb * steps and seq_off % spb == 0 and (MX_HEADS * n_tok) % BF16_SUBLANES == 0
    mem_spec = pl.BlockSpec((spb, n_mem, MX_HEADS, MX_DH), lambda b, h: (seq_off // spb + b * DA_HEADS + h, 0, 0, 0))
    return dict(
        body=functools.partial(_xattn_sample_kernel, spb=spb, n_tok=n_tok),
        in_specs=[pl.BlockSpec((spb * n_tok, width), lambda b, h: (b * DA_HEADS + h, q_sec)), mem_spec, mem_spec],
        operands=[zs, mem_k, mem_v],
        out_spec=pl.BlockSpec((spb * n_tok, width), lambda b, h: (b * DA_HEADS + h, 0)),
        out_shape=jax.ShapeDtypeStruct((N * n_tok, width), F32))


def _sgate_rider(z, w, b, u_sec, vn_sec, steps):
    T = z.shape[0]
    width = b.shape[1]
    rb = T // steps
    assert T == rb * steps and rb % CHUNK == 0
    return dict(
        body=functools.partial(_sgate_kernel, seg=CHUNK, n_chunk=rb // CHUNK),
        in_specs=[pl.BlockSpec((rb, width), lambda b_, h: (b_ * DA_HEADS + h, u_sec)),
                  pl.BlockSpec((rb, width), lambda b_, h: (b_ * DA_HEADS + h, vn_sec)),
                  pl.BlockSpec((SG_GROUPS, CHUNK, CHUNK), lambda b_, h: (0, 0, 0)),
                  pl.BlockSpec((CHUNK, width), lambda b_, h: (0, 0))],
        operands=[z, z, w, b],
        out_spec=pl.BlockSpec((rb, width), lambda b_, h: (b_ * DA_HEADS + h, 0)),
        out_shape=jax.ShapeDtypeStruct((T, width), BF16))


def _xattn_prompt_rider(z, mem_kv, B, S, q_sec, steps):
    rb = B * S // steps
    n_mem = mem_kv.shape[0] // B
    width = MX_HEADS * MX_DH
    assert steps == B * DA_HEADS and S == rb * DA_HEADS and rb % BF16_SUBLANES == 0
    return dict(
        body=_xattn_prompt_kernel,
        in_specs=[pl.BlockSpec((rb, width), lambda b, h: (b * DA_HEADS + h, q_sec)),
                  pl.BlockSpec((n_mem, width), lambda b, h: (b, 0)),
                  pl.BlockSpec((n_mem, width), lambda b, h: (b, 1))],
        operands=[z, mem_kv, mem_kv],
        out_spec=pl.BlockSpec((rb, width), lambda b, h: (b * DA_HEADS + h, 0)),
        out_shape=jax.ShapeDtypeStruct((B * S, width), BF16))


def _cast_kernel(x_ref, o_ref):
    o_ref[...] = x_ref[...].astype(o_ref.dtype)


def _cast_rider(w, steps):
    R, C = w.shape
    blocks = [n for n in range(1, steps + 1) if R % n == 0 and (R // n) % BF16_SUBLANES == 0]
    if not blocks:
        return None
    nblk = blocks[-1]

    def index(b, h):
        return (jnp.minimum(b * DA_HEADS + h, nblk - 1), 0)

    return dict(body=_cast_kernel, in_specs=[pl.BlockSpec((R // nblk, C), index)], operands=[w],
                out_spec=pl.BlockSpec((R // nblk, C), index), out_shape=jax.ShapeDtypeStruct((R, C), BF16))


def _attn_prompt(z, lam, near, subln, B, S, t, lam_init, riders=()):
    assert S % t == 0 and t % (2 * CHUNK) == 0 and MAX_DISTANCE <= CHUNK + 1
    H = DA_HEADS
    in_specs = [
        pl.BlockSpec((8, LANES), lambda b, h: (0, 0)),
        pl.BlockSpec((S, DA_DV), lambda b, h: (b, h)),
        pl.BlockSpec((S, DA_DV), lambda b, h: (b, H + h)),
        pl.BlockSpec((S, DA_DV), lambda b, h: (b, 2 * H + h)),
        pl.BlockSpec((None, 2, CHUNK, CHUNK), lambda b, h: (h, 0, 0, 0)),
        pl.BlockSpec((1, DA_DV), lambda b, h: (0, 0)),
    ]
    operands = [lam, z, z, z, near, subln]
    for r in riders:
        in_specs += r["in_specs"]
        operands += r["operands"]
    return pl.pallas_call(
        functools.partial(_attn_prompt_kernel, t=t, nq=S // t, lam_init=lam_init,
                          riders=tuple((r["body"], len(r["in_specs"])) for r in riders)),
        grid=(B, H),
        in_specs=in_specs,
        out_specs=[pl.BlockSpec((S, DA_DV), lambda b, h: (b, h))] + [r["out_spec"] for r in riders],
        out_shape=[jax.ShapeDtypeStruct((B * S, H * DA_DV), BF16)] + [r["out_shape"] for r in riders],
        compiler_params=_cparams("parallel", "arbitrary"),
        name="attn_prompt",
    )(*operands)


def _attn_sample_kernel(pt_ref, lam_ref, q_ref, kn_ref, vn_ref, far_ref, last_ref, new_ref, sub_ref, *rest,
                        spb, pps, n_tok, page, lam_init):
    n_in = spb * pps
    kp_refs, vp_refs = rest[:n_in], rest[n_in:2 * n_in]
    o_ref, qq_ref, m_ref, acc_ref = rest[2 * n_in:]
    p = pl.program_id(1)
    last = pl.num_programs(1) - 1
    hr = 2 * n_tok
    flat = page * DA_HEADS

    @pl.when(p == 0)
    def _():
        for j in range(spb):
            q = q_ref[j * n_tok:(j + 1) * n_tok, :]
            for h in range(DA_HEADS):
                qq_ref[j, h * hr:(h + 1) * hr, :] = _split_components(q[:, h * DA_DV:(h + 1) * DA_DV]).astype(BF16)
        m_ref[...] = jnp.full(m_ref.shape, -jnp.inf, F32)
        acc_ref[...] = jnp.zeros(acc_ref.shape, F32)

    far = far_ref[...]
    tail = jnp.where(p == last, last_ref[...], far)
    for j in range(spb):
        qq = qq_ref[j]
        s_list, v_list = [], []
        for r in range(pps):
            s = _nt_dot(qq, kp_refs[j * pps + r][...].reshape(flat, DA_DV).astype(BF16))
            s_list.append(s + (tail if r == pps - 1 else far))
            v_list.append(vp_refs[j * pps + r][...].reshape(flat, DA_DV).astype(BF16))
        m, acc = _softmax_update(m_ref[j], acc_ref[j], s_list, v_list, False)
        m_ref[j] = m
        acc_ref[j] = acc

    @pl.when(p == last)
    def _():
        pad = jnp.zeros((page - n_tok * DA_HEADS, DA_DV), F32)
        for j in range(spb):
            toks = slice(j * n_tok, (j + 1) * n_tok)

            def new_rows(ref):
                return jnp.concatenate([ref[toks].reshape(n_tok * DA_HEADS, DA_DV), pad], axis=0).astype(BF16)

            s_new = _nt_dot(qq_ref[j], new_rows(kn_ref)) + new_ref[...]
            _, acc2 = _softmax_update(m_ref[j], acc_ref[j], [s_new], [new_rows(vn_ref)], False)
            o = acc2[:, :DA_DV] / acc2[:, DA_DV:]
            for h in range(DA_HEADS):
                oh = o[h * hr:(h + 1) * hr]
                out = _diff_out(oh[:n_tok], oh[n_tok:], lam_ref[0:1, :], sub_ref[...], lam_init)
                o_ref[toks, h * DA_DV:(h + 1) * DA_DV] = out.astype(o_ref.dtype)


def _attn_sample(z, k_new, v_new, page_idx, cache_k, cache_v, lam, far, last, new, subln, N, n_tok, lam_init, spb, pps):
    n_pages = page_idx.shape[1]
    page = cache_k.shape[1]
    width = DA_HEADS * DA_DV
    rows = DA_HEADS * 2 * n_tok
    assert (2 * n_tok) % BF16_SUBLANES == 0 and n_pages % pps == 0 and N % spb == 0

    def const(shape):
        return pl.BlockSpec(shape, lambda n, p, pt: (0,) * len(shape))

    def page_spec(j, r):
        return pl.BlockSpec((None, page, DA_HEADS, DA_DV), lambda n, p, pt: (pt[n * spb + j, p * pps + r], 0, 0, 0))

    page_specs = [page_spec(j, r) for j in range(spb) for r in range(pps)]
    new_spec = pl.BlockSpec((spb * n_tok, DA_HEADS, DA_DV), lambda n, p, pt: (n, 0, 0))
    grid_spec = pltpu.PrefetchScalarGridSpec(
        num_scalar_prefetch=1,
        grid=(N // spb, n_pages // pps),
        in_specs=[
            const((8, LANES)),
            pl.BlockSpec((spb * n_tok, width), lambda n, p, pt: (n, 0)),
            new_spec, new_spec,
            const((rows, page * DA_HEADS)), const((rows, page * DA_HEADS)), const((rows, page)),
            const((1, DA_DV)),
        ] + page_specs + page_specs,
        out_specs=pl.BlockSpec((spb * n_tok, width), lambda n, p, pt: (n, 0)),
        scratch_shapes=[pltpu.VMEM((spb, rows, DA_DV), BF16), pltpu.VMEM((spb, rows, LANES), F32),
                        pltpu.VMEM((spb, rows, 2 * LANES), F32)],
    )
    n_in = spb * pps
    return pl.pallas_call(
        functools.partial(_attn_sample_kernel, spb=spb, pps=pps, n_tok=n_tok, page=page, lam_init=lam_init),
        grid_spec=grid_spec,
        out_shape=jax.ShapeDtypeStruct((N * n_tok, width), F32),
        compiler_params=_cparams("parallel", "arbitrary"),
        name="attn_sample",
    )(page_idx, lam, z, k_new, v_new, far, last, new, subln, *([cache_k] * n_in), *([cache_v] * n_in))


def _sgate_kernel(u_ref, vn_ref, w_ref, b_ref, o_ref, *, seg, n_chunk):
    i = lax.broadcasted_iota(jnp.int32, (CHUNK, CHUNK), 0)
    j = lax.broadcasted_iota(jnp.int32, (CHUNK, CHUNK), 1)
    keep = (_div_pow2(i, seg) == _div_pow2(j, seg)) & (j <= i)
    gd = u_ref.shape[1] // SG_GROUPS
    for g in range(SG_GROUPS):
        w = jnp.where(keep, w_ref[g], 0.0).astype(BF16)
        cols = slice(g * gd, (g + 1) * gd)
        for c in range(n_chunk):
            rws = slice(c * CHUNK, (c + 1) * CHUNK)
            mix = jnp.dot(w, vn_ref[rws, cols].astype(BF16), preferred_element_type=F32) + b_ref[:, cols]
            o_ref[rws, cols] = (u_ref[rws, cols].astype(F32) * mix).astype(o_ref.dtype)


def _sgate(z, w, b, u_sec, vn_sec, n_chunk):
    T = z.shape[0]
    width = b.shape[1]
    rb = n_chunk * CHUNK
    return pl.pallas_call(
        functools.partial(_sgate_kernel, seg=CHUNK, n_chunk=n_chunk),
        grid=(T // rb,),
        in_specs=[
            pl.BlockSpec((rb, width), lambda i: (i, u_sec)),
            pl.BlockSpec((rb, width), lambda i: (i, vn_sec)),
            pl.BlockSpec((SG_GROUPS, CHUNK, CHUNK), lambda i: (0, 0, 0)),
            pl.BlockSpec((CHUNK, width), lambda i: (0, 0)),
        ],
        out_specs=pl.BlockSpec((rb, width), lambda i: (i, 0)),
        out_shape=jax.ShapeDtypeStruct((T, width), BF16),
        compiler_params=_cparams("parallel"),
        name="sgate",
    )(z, z, w, b)


def _sgate_seq_kernel(w_ref, b_ref, u_ref, vn_ref, o_ref, wbd_ref, bias_ref, *, seg):
    gd = u_ref.shape[1] // SG_GROUPS

    @pl.when(pl.program_id(0) == 0)
    def _():
        i = lax.broadcasted_iota(jnp.int32, (CHUNK, CHUNK), 0)
        j = lax.broadcasted_iota(jnp.int32, (CHUNK, CHUNK), 1)
        same = _div_pow2(i, seg) == _div_pow2(j, seg)
        ti, tj = i & (seg - 1), j & (seg - 1)
        for g in range(SG_GROUPS):
            w = jnp.zeros((CHUNK, CHUNK), F32)
            bias = jnp.zeros((CHUNK, gd), F32)
            for a in range(seg):
                bias = jnp.where(ti == a, b_ref[g, a], bias)
                for b in range(a + 1):
                    w = jnp.where((ti == a) & (tj == b), w_ref[g * seg + a, b], w)
            wbd_ref[g] = jnp.where(same, w, 0.0).astype(BF16)
            bias_ref[:, g * gd:(g + 1) * gd] = bias

    for g in range(SG_GROUPS):
        cols = slice(g * gd, (g + 1) * gd)
        mix = jnp.dot(wbd_ref[g], vn_ref[:, cols].astype(BF16), preferred_element_type=F32) + bias_ref[:, cols]
        o_ref[:, cols] = (u_ref[:, cols].astype(F32) * mix).astype(o_ref.dtype)


def _sgate_seq(z, w, b, seg, u_sec, vn_sec):
    T = z.shape[0]
    width = SG_GROUPS * CHUNK
    assert CHUNK % seg == 0 and seg & (seg - 1) == 0 and T % CHUNK == 0
    return pl.pallas_call(
        functools.partial(_sgate_seq_kernel, seg=seg),
        grid=(T // CHUNK,),
        in_specs=[
            pl.BlockSpec(memory_space=pltpu.SMEM), pl.BlockSpec(memory_space=pltpu.SMEM),
            pl.BlockSpec((CHUNK, width), lambda i: (i, u_sec)),
            pl.BlockSpec((CHUNK, width), lambda i: (i, vn_sec)),
        ],
        out_specs=pl.BlockSpec((CHUNK, width), lambda i: (i, 0)),
        out_shape=jax.ShapeDtypeStruct((T, width), BF16),
        scratch_shapes=[pltpu.VMEM((SG_GROUPS, CHUNK, CHUNK), BF16), pltpu.VMEM((CHUNK, width), F32)],
        compiler_params=_cparams("arbitrary"),
        name="sgate_seq",
    )(w[:, :seg, :seg].reshape(SG_GROUPS * seg, seg), b[:, :seg], z, z)


def _xattn_prompt_kernel(q_ref, k_ref, v_ref, o_ref):
    heads = [slice(h * MX_DH, (h + 1) * MX_DH) for h in range(MX_HEADS)]
    scores = [_nt_dot(q_ref[:, cols], k_ref[:, cols]) for cols in heads]
    exps = [jnp.exp(s - jnp.max(s, axis=-1, keepdims=True)) for s in scores]
    for cols, e in zip(heads, exps):
        o = jnp.dot(e.astype(BF16), v_ref[:, cols], preferred_element_type=F32)
        o_ref[:, cols] = (o / jnp.sum(e, axis=-1, keepdims=True)).astype(o_ref.dtype)


def _xattn_prompt(z, mem_kv, B, S, tq, q_sec):
    nq = S // tq
    n_mem = mem_kv.shape[0] // B
    width = MX_HEADS * MX_DH
    return pl.pallas_call(
        _xattn_prompt_kernel,
        grid=(B, nq),
        in_specs=[
            pl.BlockSpec((tq, width), lambda b, i: (b * nq + i, q_sec)),
            pl.BlockSpec((n_mem, width), lambda b, i: (b, 0)),
            pl.BlockSpec((n_mem, width), lambda b, i: (b, 1)),
        ],
        out_specs=pl.BlockSpec((tq, width), lambda b, i: (b * nq + i, 0)),
        out_shape=jax.ShapeDtypeStruct((B * S, width), BF16),
        compiler_params=_cparams("parallel", "arbitrary"),
        name="xattn_prompt",
    )(z, mem_kv, mem_kv)


def _xattn_sample_kernel(q_ref, k_ref, v_ref, o_ref, *, spb, n_tok):
    n_mem = k_ref.shape[1]
    rows, flat = MX_HEADS * n_tok, n_mem * MX_HEADS
    r = lax.broadcasted_iota(jnp.int32, (rows, flat), 0)
    c = lax.broadcasted_iota(jnp.int32, (rows, flat), 1)
    own = (c & (MX_HEADS - 1)) == _div_pow2(r, n_tok)
    for j in range(spb):
        toks = slice(j * n_tok, (j + 1) * n_tok)
        q = q_ref[toks, :]
        qq = jnp.concatenate([q[:, h * MX_DH:(h + 1) * MX_DH] for h in range(MX_HEADS)], axis=0).astype(BF16)
        s = jnp.where(own, _nt_dot(qq, k_ref[j].reshape(flat, MX_DH).astype(BF16)), NEG_INF)
        e = jnp.exp(s - jnp.max(s, axis=-1, keepdims=True))
        o = jnp.dot(e.astype(BF16), v_ref[j].reshape(flat, MX_DH).astype(BF16), preferred_element_type=F32)
        o = o / jnp.sum(e, axis=-1, keepdims=True)
        for h in range(MX_HEADS):
            o_ref[toks, h * MX_DH:(h + 1) * MX_DH] = o[h * n_tok:(h + 1) * n_tok].astype(o_ref.dtype)


def _xattn_sample(z, mem_k, mem_v, N, n_tok, q_sec, seq_off, spb):
    n_mem = mem_k.shape[1]
    width = MX_HEADS * MX_DH
    assert (MX_HEADS * n_tok) % BF16_SUBLANES == 0 and N % spb == 0 and seq_off % spb == 0
    mem_spec = pl.BlockSpec((spb, n_mem, MX_HEADS, MX_DH), lambda n: (seq_off // spb + n, 0, 0, 0))
    return pl.pallas_call(
        functools.partial(_xattn_sample_kernel, spb=spb, n_tok=n_tok),
        grid=(N // spb,),
        in_specs=[pl.BlockSpec((spb * n_tok, width), lambda n: (n, q_sec)), mem_spec, mem_spec],
        out_specs=pl.BlockSpec((spb * n_tok, width), lambda n: (n, 0)),
        out_shape=jax.ShapeDtypeStruct((N * n_tok, width), F32),
        compiler_params=_cparams("parallel"),
        name="xattn_sample",
    )(z, mem_k, mem_v)


def _post_kernel(x_ref, da_ref, sg_ref, mx_ref, gate_ref, wb_ref, wo_ref, gf_ref, wu_ref, wd_ref, gfin_ref,
                 o_ref, y_ref, *, d, d_ff, fc, n_sub, final_norm):
    branches = (da_ref, sg_ref, mx_ref)
    rs = x_ref.shape[0] // n_sub
    subs = [slice(sb * rs, (sb + 1) * rs) for sb in range(n_sub)]
    merged = [None] * n_sub
    for k in range(N_BRANCH):
        for i, rws in enumerate(subs):
            br = jnp.dot(branches[k][rws, :].astype(BF16), wb_ref[k * d:(k + 1) * d, :], preferred_element_type=F32)
            term = gate_ref[rws, k * d:(k + 1) * d].astype(F32) * br
            merged[i] = term if merged[i] is None else merged[i] + term
    h2 = []
    for i, rws in enumerate(subs):
        x1 = x_ref[rws, :] + jnp.dot(merged[i].astype(BF16), wo_ref[...], preferred_element_type=F32)
        y_ref[rws, :] = x1
        h2.append(_rms(x1, gf_ref[...]).astype(BF16))
    for c in range(d_ff // fc):
        up = [(jnp.dot(h, wu_ref[:, c * fc:(c + 1) * fc], preferred_element_type=F32),
               jnp.dot(h, wu_ref[:, d_ff + c * fc:d_ff + (c + 1) * fc], preferred_element_type=F32)) for h in h2]
        for (a, b), rws in zip(up, subs):
            act = (jax.nn.silu(a) * b).astype(BF16)
            y_ref[rws, :] += jnp.dot(act, wd_ref[c * fc:(c + 1) * fc, :], preferred_element_type=F32)
    for rws in subs:
        y = y_ref[rws, :]
        o_ref[rws, :] = _rms(y, gfin_ref[...]) if final_norm else y


def _post(x, o_da, o_sg, o_mx, z, gate_sec, wb, wo, g_ffn, wu, wd, g_final, tm, final_norm):
    T, d = x.shape
    d_ff = wd.shape[0]
    fc = 256
    n_sub = 2 if tm >= 512 else 1
    row = lambda i: (i, 0)
    whole = lambda i: (0, 0)

    def resident(shape):
        return pl.BlockSpec(shape, whole, pipeline_mode=pl.Buffered(1))

    return pl.pallas_call(
        functools.partial(_post_kernel, d=d, d_ff=d_ff, fc=fc, n_sub=n_sub, final_norm=final_norm),
        grid=(T // tm,),
        in_specs=[
            pl.BlockSpec((tm, d), row), pl.BlockSpec((tm, d), row), pl.BlockSpec((tm, d), row), pl.BlockSpec((tm, d), row),
            pl.BlockSpec((tm, N_BRANCH * d), lambda i: (i, gate_sec)),
            resident(wb.shape), resident(wo.shape), resident((1, d)), resident(wu.shape), resident(wd.shape),
            resident((1, d)),
        ],
        out_specs=pl.BlockSpec((tm, d), row),
        out_shape=jax.ShapeDtypeStruct((T, d), F32),
        scratch_shapes=[pltpu.VMEM((tm, d), F32)],
        compiler_params=_cparams("parallel"),
        name="post",
    )(x, o_da, o_sg, o_mx, z, wb, wo, g_ffn, wu, wd, g_final)


_IN_SECTIONS = ("q", "k", "v", "u", "s", "m", "g0", "g1", "g2")
_IN_ACTS = {
    "q": ("scale", DA_DK ** -0.5 * LOG2E), "k": ("none",), "v": ("none",), "u": ("gelu",), "s": ("gelu_rms",),
    "m": ("scale", MX_DH ** -0.5), "g0": ("sigmoid",), "g1": ("sigmoid",), "g2": ("sigmoid",),
}
_IN_COPIES = {"k": 0, "v": 1}


def kernel(x_prompt, x_sample, mem_prompt, cache_da_k, cache_da_v, cache_mem_k, cache_mem_v, page_table, g_attn, w_in,
           da_lam, da_subln, rel_bias, sg_norm, sg_w, sg_b, g_mem, w_mem_kv, w_branch, w_out, g_ffn, w_up, w_down,
           g_final):
    B, S, D = x_prompt.shape
    N, n_tok, _ = x_sample.shape
    depth, n_pool, page = cache_da_k.shape[:3]
    n_mem = mem_prompt.shape[1]
    n_pages = page_table.shape[1]
    width = DA_HEADS * DA_DV
    assert D == width == SG_GROUPS * CHUNK == MX_HEADS * MX_DH and page == CHUNK and n_tok <= CHUNK

    t_attn = min(512, S)
    pps = max(d for d in range(1, MAX_PAGES_PER_STEP + 1) if n_pages % d == 0)
    spb_da = 1
    spb_mx = 4 if N % 4 == 0 else 1
    ride_mx = N % (B * DA_HEADS) == 0 and N // (B * DA_HEADS) <= spb_mx
    ride_sg = S % (DA_HEADS * CHUNK) == 0
    tm_p = 256
    tm_s = min(256, N * n_tok)
    tm_post = 512
    sec = {n: i for i, n in enumerate(_IN_SECTIONS)}
    in_sections = [(_IN_ACTS[n], _IN_COPIES.get(n)) for n in _IN_SECTIONS]

    xp = x_prompt.reshape(B * S, D)
    xs = x_sample.reshape(N * n_tok, D)
    mem = mem_prompt.reshape(B * n_mem, D)
    ck = cache_da_k.reshape(depth * n_pool, page, DA_HEADS, DA_DV)
    cv = cache_da_v.reshape(depth * n_pool, page, DA_HEADS, DA_DV)
    cmk = cache_mem_k.reshape(depth * N, n_mem, MX_HEADS, MX_DH)
    cmv = cache_mem_v.reshape(depth * N, n_mem, MX_HEADS, MX_DH)
    row = lambda a: a.reshape(1, -1)

    near_bias = _prompt_bias(rel_bias)
    seg_s = n_tok
    outs = {k: [] for k in ("dkp", "dvp", "dks", "dvs", "mkp", "mvp", "sgs")}
    for l in range(depth):
        lam_init = 0.8 - 0.6 * math.exp(-0.3 * l)
        w_in_l = w_in[l].astype(BF16)
        far, last, new, lam = _sample_bias(rel_bias, da_lam[l], n_tok, page, lam_init)
        subln = row(da_subln[l])
        sgb_p = jnp.repeat(sg_b[l].T, CHUNK, axis=1)

        zs, ks32, vs32 = _norm_matmul(xs, row(g_attn[l]), w_in_l, row(sg_norm[l]), in_sections, F32, tm_s,
                                      "inproj_sample", copy_heads=DA_HEADS)
        mkv16, mk32, mv32 = _norm_matmul(mem, row(g_mem[l]), w_mem_kv[l].astype(BF16), row(sg_norm[l]),
                                         [(("none",), 0), (("none",), 1)], BF16, min(512, B * n_mem), "mem_kv",
                                         copy_heads=MX_HEADS)
        zs, mkv16, xp_l = lax.optimization_barrier((zs, mkv16, xp))
        zp, kp32, vp32 = _norm_matmul(xp_l, row(g_attn[l]), w_in_l, row(sg_norm[l]), in_sections, BF16, tm_p,
                                      "inproj_prompt")
        steps = B * DA_HEADS
        riders = []
        if ride_mx:
            riders.append(_xattn_rider(zs, cmk, cmv, N, n_tok, sec["m"], l * N, steps))
        if ride_sg:
            riders.append(_sgate_rider(zp, sg_w[l], sgb_p, sec["u"], sec["s"], steps))
            riders.append(_xattn_prompt_rider(zp, mkv16, B, S, sec["m"], steps))
        later_w = [w_branch[l], w_out[l], w_up[l], w_down[l]]
        casts = [_cast_rider(w, steps) for w in later_w]
        oda_p, *rode = _attn_prompt(zp, lam, near_bias, subln, B, S, t_attn, lam_init,
                                    riders + [c for c in casts if c is not None])
        omx_s = rode.pop(0) if ride_mx else _xattn_sample(zs, cmk, cmv, N, n_tok, sec["m"], l * N, spb_mx)
        osg_p = rode.pop(0) if ride_sg else _sgate(zp, sg_w[l], sgb_p, sec["u"], sec["s"], 8)
        omx_p = (rode.pop(0) if ride_sg
                 else _xattn_prompt(zp, mkv16, B, S, 1024 if S % 1024 == 0 else 512, sec["m"]))
        wb_l, wo_l, wu_l, wd_l = [rode.pop(0) if c is not None else w.astype(BF16) for c, w in zip(casts, later_w)]
        oda_p, omx_s, osg_p, omx_p, zs = lax.optimization_barrier((oda_p, omx_s, osg_p, omx_p, zs))
        oda_s = _attn_sample(zs, ks32, vs32, page_table + l * n_pool, ck, cv, lam, far, last, new, subln, N, n_tok,
                             lam_init, spb_da, pps)
        osg_s = _sgate_seq(zs, sg_w[l], sg_b[l], seg_s, sec["u"], sec["s"])

        xs = _post(xs, oda_s, osg_s, omx_s, zs, sec["g0"] // N_BRANCH, wb_l, wo_l, row(g_ffn[l]), wu_l, wd_l,
                   row(g_final), tm_s, l == depth - 1)
        xp = _post(xp, oda_p, osg_p, omx_p, zp, sec["g0"] // N_BRANCH, wb_l, wo_l, row(g_ffn[l]), wu_l, wd_l,
                   row(g_final), tm_post, l == depth - 1)

        outs["dkp"].append(kp32.reshape(B, S, DA_HEADS, 2 * DA_DK))
        outs["dvp"].append(vp32.reshape(B, S, DA_HEADS, DA_DV))
        outs["dks"].append(ks32.reshape(N, n_tok, DA_HEADS, 2 * DA_DK))
        outs["dvs"].append(vs32.reshape(N, n_tok, DA_HEADS, DA_DV))
        outs["mkp"].append(mk32.reshape(B, n_mem, MX_HEADS, MX_DH))
        outs["mvp"].append(mv32.reshape(B, n_mem, MX_HEADS, MX_DH))
        outs["sgs"].append(zs[:, sec["s"] * D:(sec["s"] + 1) * D].reshape(N, n_tok, D))

    return (xp.reshape(B, S, D), xs.reshape(N, n_tok, D), jnp.stack(outs["dkp"]), jnp.stack(outs["dvp"]),
            jnp.stack(outs["dks"]), jnp.stack(outs["dvs"]), jnp.stack(outs["mkp"]), jnp.stack(outs["mvp"]),
            jnp.stack(outs["sgs"]))
```
